```python
import math, functools
import jax, jax.numpy as jnp
from jax import lax
import numpy as np

D_MODEL = 1024
BATCH = 16
SEQ = 256
DEPTH = 4
DEC_BATCH = 8
DEC_SEQ = 4096
PAST_LEN = 512

GRID_W = 64
N_MIXERS = 2
N_ATTN_LAYERS = (DEPTH + 1) // 2
N_SSM_LAYERS = DEPTH // 2
N_HEADS = 16
N_KV_HEADS = 4
HEAD_DIM = 64
Q_WIDTH = N_HEADS * HEAD_DIM
KV_WIDTH = N_KV_HEADS * HEAD_DIM
WINDOW = 128
BLOCK = 128
ROPE_BASE = 10000.0
SSM_WIDTH = D_MODEL
SSM_GROUP = 16
N_GROUPS = SSM_WIDTH // SSM_GROUP
STATE = 64
SSM_CHUNK = 128
D_FF = -(-8 * D_MODEL // (3 * 256)) * 256
DN_ALPHA = (2 * DEPTH) ** 0.25
DN_BETA = (8 * DEPTH) ** -0.25
LN_EPS = 1e-5
NEG_INF = -1e30

kernel_name = "hybrid_swa_s5_diffusion_step"


def layer_norm(x, g, b):
    xf = x.astype(jnp.float32)
    mu = jnp.mean(xf, -1, keepdims=True)
    var = jnp.mean(jnp.square(xf - mu), -1, keepdims=True)
    return ((xf - mu) * lax.rsqrt(var + LN_EPS) * g + b).astype(x.dtype)


def ada_params(cvec, w, b):
    m = (jax.nn.silu(cvec) @ w + b).reshape(cvec.shape[0], 6, 1, D_MODEL)
    return [m[:, i] for i in range(6)]


def modulate(x, shift, scale):
    return x * (1 + scale) + shift


def swiglu(h, w1, w3, w2):
    return (jax.nn.silu(h @ w1) * (h @ w3)) @ w2


def trunk_layer(x, mods, mixer, ln_g, ln_b, w1, w3, w2):
    sh1, sc1, g1, sh2, sc2, g2 = mods
    out, extra = mixer(modulate(x, sh1, sc1))
    x = layer_norm(DN_ALPHA * x + g1 * out, ln_g[0], ln_b[0])
    x = layer_norm(DN_ALPHA * x + g2 * swiglu(modulate(x, sh2, sc2), w1, w3, w2), ln_g[1], ln_b[1])
    return x, extra


def split_qkv(h, w):
    n, t, _ = h.shape
    qkv = h @ w
    q = qkv[..., :Q_WIDTH].reshape(n, t, N_HEADS, HEAD_DIM)
    k = qkv[..., Q_WIDTH:Q_WIDTH + KV_WIDTH].reshape(n, t, N_KV_HEADS, HEAD_DIM)
    v = qkv[..., Q_WIDTH + KV_WIDTH:].reshape(n, t, N_KV_HEADS, HEAD_DIM)
    return q, k, v


def sink_softmax(s, sink):
    m = jnp.maximum(jnp.max(s, -1, keepdims=True), sink)
    p = jnp.exp(s - m)
    return p / (jnp.sum(p, -1, keepdims=True) + jnp.exp(sink - m))


def axial_rope_angles(t):
    n_rows = t // GRID_W
    row = jnp.repeat(jnp.arange(n_rows, dtype=jnp.float32), GRID_W)
    col = (jnp.arange(t) % GRID_W).astype(jnp.float32)
    nfreq = HEAD_DIM // 4
    inv = jnp.power(ROPE_BASE, -jnp.arange(nfreq, dtype=jnp.float32) / nfreq)
    return row[:, None] * inv, col[:, None] * inv


def rotate_half(x, ang):
    x1, x2 = jnp.split(x, 2, -1)
    c = jnp.cos(ang)[None, :, None, :].astype(x.dtype)
    s = jnp.sin(ang)[None, :, None, :].astype(x.dtype)
    return jnp.concatenate([x1 * c - x2 * s, x1 * s + x2 * c], -1)


def apply_axial_rope(x, ang_row, ang_col):
    xr, xc = jnp.split(x, 2, -1)
    return jnp.concatenate([rotate_half(xr, ang_row), rotate_half(xc, ang_col)], -1)


def context_attention(q, k, v, sink):
    n, L = q.shape[:2]
    g = N_HEADS // N_KV_HEADS
    qb = q.reshape(n, L // BLOCK, BLOCK, N_KV_HEADS, g, HEAD_DIM).swapaxes(0, 1)
    sink_b = sink.astype(jnp.float32).reshape(1, N_KV_HEADS, g, 1, 1)
    scale = HEAD_DIM ** -0.5

    def one_block(qj):
        s = jnp.einsum('nqkgd,nskd->nkgqs', qj, k, preferred_element_type=jnp.float32) * scale
        p = sink_softmax(s, sink_b).astype(v.dtype)
        return jnp.einsum('nkgqs,nskd->nqkgd', p, v)

    o = lax.map(one_block, qb)
    return o.swapaxes(0, 1).reshape(n, L, Q_WIDTH)


def latent_attention(q_rot, q_plain, k_rot, v, k_ctx, v_ctx, sink):
    n, t = q_rot.shape[:2]
    g = N_HEADS // N_KV_HEADS
    L = k_ctx.shape[1]
    kp = jnp.pad(k_rot, ((0, 0), (BLOCK, BLOCK), (0, 0), (0, 0)))
    vp = jnp.pad(v, ((0, 0), (BLOCK, BLOCK), (0, 0), (0, 0)))
    sink_b = sink.astype(jnp.float32).reshape(1, N_KV_HEADS, g, 1, 1)
    scale = HEAD_DIM ** -0.5
    q_offs = jnp.arange(BLOCK)
    k_offs = jnp.arange(3 * BLOCK) - BLOCK

    def one_block(j):
        start = j * BLOCK
        qr = lax.dynamic_slice_in_dim(q_rot, start, BLOCK, axis=1).reshape(n, BLOCK, N_KV_HEADS, g, HEAD_DIM)
        qp = lax.dynamic_slice_in_dim(q_plain, start, BLOCK, axis=1).reshape(n, BLOCK, N_KV_HEADS, g, HEAD_DIM)
        kj = lax.dynamic_slice_in_dim(kp, start, 3 * BLOCK, axis=1)
        vj = lax.dynamic_slice_in_dim(vp, start, 3 * BLOCK, axis=1)
        qpos = start + q_offs
        kpos = start + k_offs
        valid = (kpos[None, :] >= 0) & (kpos[None, :] < t) & (jnp.abs(qpos[:, None] - kpos[None, :]) <= WINDOW)
        s_loc = jnp.einsum('nqkgd,nskd->nkgqs', qr, kj, preferred_element_type=jnp.float32) * scale
        s_loc = jnp.where(valid, s_loc, NEG_INF)
        s_ctx = jnp.einsum('nqkgd,nskd->nkgqs', qp, k_ctx, preferred_element_type=jnp.float32) * scale
        p = sink_softmax(jnp.concatenate([s_ctx, s_loc], -1), sink_b).astype(v.dtype)
        return (jnp.einsum('nkgqs,nskd->nqkgd', p[..., :L], v_ctx)
                + jnp.einsum('nkgqs,nskd->nqkgd', p[..., L:], vj))

    o = lax.map(one_block, jnp.arange(t // BLOCK))
    return o.swapaxes(0, 1).reshape(n, t, Q_WIDTH)


def ctx_attn_mixer(h, w_qkv_a, w_o_a, sink_a):
    q, k, v = split_qkv(h, w_qkv_a)
    return context_attention(q, k, v, sink_a) @ w_o_a, (k, v)


def lat_attn_mixer(h, k_ctx, v_ctx, ang_row, ang_col, w_qkv_a, w_o_a, sink_a):
    q, k, v = split_qkv(h, w_qkv_a)
    o = latent_attention(apply_axial_rope(q, ang_row, ang_col), q,
                         apply_axial_rope(k, ang_row, ang_col), v, k_ctx, v_ctx, sink_a)
    return o @ w_o_a, None


def s5_discretise(lam_re, lam_im, log_dt, b_re, b_im):
    lam_re = lam_re.astype(jnp.float32)
    lam_im = lam_im.astype(jnp.float32)
    dt = jnp.exp(log_dt.astype(jnp.float32))[:, None]
    mag = jnp.exp(lam_re * dt)
    a_re = mag * jnp.cos(lam_im * dt)
    a_im = mag * jnp.sin(lam_im * dt)
    den = lam_re ** 2 + lam_im ** 2
    nr, ni = a_re - 1.0, a_im
    coef_re = ((nr * lam_re + ni * lam_im) / den)[..., None]
    coef_im = ((ni * lam_re - nr * lam_im) / den)[..., None]
    b_re = b_re.astype(jnp.float32)
    b_im = b_im.astype(jnp.float32)
    bb_re = coef_re * b_re - coef_im * b_im
    bb_im = coef_re * b_im + coef_im * b_re
    return a_re, a_im, bb_re, bb_im


def complex_linear_combine(e1, e2):
    a1r, a1i, b1r, b1i = e1
    a2r, a2i, b2r, b2i = e2
    return (a2r * a1r - a2i * a1i, a2r * a1i + a2i * a1r,
            a2r * b1r - a2i * b1i + b2r, a2r * b1i + a2i * b1r + b2i)


def s5_direction(u, h0_re, h0_im, lam_re, lam_im, log_dt, b_re, b_im, c_re, c_im, reverse):
    a_re, a_im, bb_re, bb_im = s5_discretise(lam_re, lam_im, log_dt, b_re, b_im)
    c_re = c_re.astype(jnp.float32)
    c_im = c_im.astype(jnp.float32)
    n, t = u.shape[:2]
    if reverse:
        u = jnp.flip(u, 1)
    uc = u.reshape(n, t // SSM_CHUNK, SSM_CHUNK, N_GROUPS, SSM_GROUP).swapaxes(0, 1)
    a_r = jnp.broadcast_to(a_re[None, None], (SSM_CHUNK, 1, N_GROUPS, STATE))
    a_i = jnp.broadcast_to(a_im[None, None], (SSM_CHUNK, 1, N_GROUPS, STATE))

    def chunk(carry, u_blk):
        hr, hi = carry
        bu_r = jnp.einsum('ncgm,gpm->cngp', u_blk, bb_re)
        bu_i = jnp.einsum('ncgm,gpm->cngp', u_blk, bb_im)
        bu_r = bu_r.at[0].add(a_re * hr - a_im * hi)
        bu_i = bu_i.at[0].add(a_re * hi + a_im * hr)
        _, _, sr, si = lax.associative_scan(complex_linear_combine, (a_r, a_i, bu_r, bu_i), axis=0)
        y = jnp.einsum('cngp,gmp->ncgm', sr, c_re) - jnp.einsum('cngp,gmp->ncgm', si, c_im)
        return (sr[-1], si[-1]), y

    (hr, hi), ys = lax.scan(chunk, (h0_re.astype(jnp.float32), h0_im.astype(jnp.float32)), uc)
    y = ys.swapaxes(0, 1).reshape(n, t, N_GROUPS, SSM_GROUP)
    if reverse:
        y = jnp.flip(y, 1)
    return y, hr, hi


def s5_mixer(h, h0_re, h0_im, w_in, lam_re, lam_im, log_dt, b_re, b_im, c_re, c_im, d_skip, w_glu, w_out):
    n, t, _ = h.shape
    u = (h @ w_in).astype(jnp.float32).reshape(n, t, N_GROUPS, SSM_GROUP)
    y_f, hfr, hfi = s5_direction(u, h0_re[:, 0], h0_im[:, 0], lam_re[0], lam_im[0], log_dt[0],
                                 b_re[0], b_im[0], c_re[0], c_im[0], False)
    y_b, hbr, hbi = s5_direction(u, h0_re[:, 1], h0_im[:, 1], lam_re[1], lam_im[1], log_dt[1],
                                 b_re[1], b_im[1], c_re[1], c_im[1], True)
    y = y_f + y_b + d_skip.astype(jnp.float32).reshape(N_GROUPS, SSM_GROUP) * u
    y = jax.nn.gelu(y.reshape(n, t, SSM_WIDTH)).astype(h.dtype)
    val, gate = jnp.split(y @ w_glu, 2, -1)
    out = (val * jax.nn.sigmoid(gate)) @ w_out
    return out, (jnp.stack([hfr, hbr], 1), jnp.stack([hfi, hbi], 1))


def setup_inputs(seed: int = 0) -> dict:
    key = jax.random.key(seed)
    ks = jax.random.split(key, 32)
    f32 = jnp.float32

    def nrm(k, shape, std):
        return std * jax.random.normal(k, shape, f32)

    n_idx = jnp.arange(STATE, dtype=f32)
    lam_shape = (N_SSM_LAYERS, 2, N_GROUPS, STATE)
    return {
        "x_prompt": nrm(ks[0], (BATCH, SEQ, D_MODEL), 1.0),
        "x_sample": nrm(ks[1], (DEC_BATCH, DEC_SEQ, D_MODEL), 1.0),
        "cache_k": nrm(ks[2], (DEC_BATCH, N_ATTN_LAYERS, PAST_LEN, N_KV_HEADS, HEAD_DIM), 1.0),
        "cache_v": nrm(ks[3], (DEC_BATCH, N_ATTN_LAYERS, PAST_LEN, N_KV_HEADS, HEAD_DIM), 1.0),
        "state_ssm_re": nrm(ks[4], (DEC_BATCH, N_SSM_LAYERS, 2, N_GROUPS, STATE), 0.3),
        "state_ssm_im": nrm(ks[5], (DEC_BATCH, N_SSM_LAYERS, 2, N_GROUPS, STATE), 0.3),
        "c": nrm(ks[6], (DEC_BATCH, D_MODEL), 1.0),
        "c_ctx": nrm(ks[7], (D_MODEL,), 1.0),
        "w_ada": nrm(ks[8], (DEPTH, D_MODEL, 6 * D_MODEL), 0.5 * D_MODEL ** -0.5),
        "b_ada": nrm(ks[9], (DEPTH, 6 * D_MODEL), 0.02),
        "ln_g": 1.0 + nrm(ks[10], (DEPTH, 2, D_MODEL), 0.02),
        "ln_b": nrm(ks[11], (DEPTH, 2, D_MODEL), 0.02),
        "w_qkv": nrm(ks[12], (N_ATTN_LAYERS, D_MODEL, Q_WIDTH + 2 * KV_WIDTH), D_MODEL ** -0.5),
        "w_o": nrm(ks[13], (N_ATTN_LAYERS, Q_WIDTH, D_MODEL), DN_BETA * Q_WIDTH ** -0.5),
        "attn_sink": nrm(ks[14], (N_ATTN_LAYERS, N_HEADS), 0.5),
        "ssm_w_in": nrm(ks[15], (N_SSM_LAYERS, D_MODEL, SSM_WIDTH), D_MODEL ** -0.5),
        "ssm_lam_re": -0.5 + nrm(ks[16], lam_shape, 0.01),
        "ssm_lam_im": math.pi * n_idx + nrm(ks[17], lam_shape, 0.01),
        "ssm_log_dt": jax.random.uniform(ks[18], (N_SSM_LAYERS, 2, N_GROUPS), f32,
                                         minval=math.log(1e-3), maxval=math.log(1e-1)),
        "ssm_b_re": nrm(ks[19], (N_SSM_LAYERS, 2, N_GROUPS, STATE, SSM_GROUP), (2 * SSM_GROUP) ** -0.5),
        "ssm_b_im": nrm(ks[20], (N_SSM_LAYERS, 2, N_GROUPS, STATE, SSM_GROUP), (2 * SSM_GROUP) ** -0.5),
        "ssm_c_re": nrm(ks[21], (N_SSM_LAYERS, 2, N_GROUPS, SSM_GROUP, STATE), STATE ** -0.5),
        "ssm_c_im": nrm(ks[22], (N_SSM_LAYERS, 2, N_GROUPS, SSM_GROUP, STATE), STATE ** -0.5),
        "ssm_d": nrm(ks[23], (N_SSM_LAYERS, SSM_WIDTH), 1.0),
        "ssm_w_glu": nrm(ks[24], (N_SSM_LAYERS, SSM_WIDTH, 2 * SSM_WIDTH), SSM_WIDTH ** -0.5),
        "ssm_w_out": nrm(ks[25], (N_SSM_LAYERS, SSM_WIDTH, D_MODEL), DN_BETA * SSM_WIDTH ** -0.5),
        "ffn_w1": nrm(ks[26], (DEPTH, D_MODEL, D_FF), D_MODEL ** -0.5),
        "ffn_w3": nrm(ks[27], (DEPTH, D_MODEL, D_FF), D_MODEL ** -0.5),
        "ffn_w2": nrm(ks[28], (DEPTH, D_FF, D_MODEL), DN_BETA * D_FF ** -0.5),
    }


def reference(x_prompt, x_sample, cache_k, cache_v, state_ssm_re, state_ssm_im, c, c_ctx,
              w_ada, b_ada, ln_g, ln_b, w_qkv, w_o, attn_sink,
              ssm_w_in, ssm_lam_re, ssm_lam_im, ssm_log_dt, ssm_b_re, ssm_b_im,
              ssm_c_re, ssm_c_im, ssm_d, ssm_w_glu, ssm_w_out,
              ffn_w1, ffn_w3, ffn_w2):
    def ssm_params(s):
        return (ssm_w_in[s], ssm_lam_re[s], ssm_lam_im[s], ssm_log_dt[s], ssm_b_re[s], ssm_b_im[s],
                ssm_c_re[s], ssm_c_im[s], ssm_d[s], ssm_w_glu[s], ssm_w_out[s])

    y = x_prompt
    zeros_state = jnp.zeros((x_prompt.shape[0], 2, N_GROUPS, STATE), jnp.float32)
    new_k, new_v, new_sr, new_si = [], [], [], []
    for l in range(DEPTH):
        mods = ada_params(c_ctx[None, :], w_ada[l], b_ada[l])
        if l % N_MIXERS == 0:
            a = l // N_MIXERS
            mixer = functools.partial(ctx_attn_mixer, w_qkv_a=w_qkv[a], w_o_a=w_o[a], sink_a=attn_sink[a])
        else:
            s = l // N_MIXERS
            mixer = functools.partial(s5_mixer, h0_re=zeros_state, h0_im=zeros_state,
                                      **dict(zip(('w_in', 'lam_re', 'lam_im', 'log_dt', 'b_re', 'b_im',
                                                  'c_re', 'c_im', 'd_skip', 'w_glu', 'w_out'), ssm_params(s))))
        y, extra = trunk_layer(y, mods, mixer, ln_g[l], ln_b[l], ffn_w1[l], ffn_w3[l], ffn_w2[l])
        if l % N_MIXERS == 0:
            new_k.append(extra[0])
            new_v.append(extra[1])
        else:
            new_sr.append(extra[0])
            new_si.append(extra[1])
    y_prompt = y

    z = x_sample
    ang_row, ang_col = axial_rope_angles(x_sample.shape[1])
    for l in range(DEPTH):
        mods = ada_params(c, w_ada[l], b_ada[l])
        if l % N_MIXERS == 0:
            a = l // N_MIXERS
            mixer = functools.partial(lat_attn_mixer, k_ctx=cache_k[:, a], v_ctx=cache_v[:, a],
                                      ang_row=ang_row, ang_col=ang_col,
                                      w_qkv_a=w_qkv[a], w_o_a=w_o[a], sink_a=attn_sink[a])
        else:
            s = l // N_MIXERS
            mixer = functools.partial(s5_mixer, h0_re=state_ssm_re[:, s], h0_im=state_ssm_im[:, s],
                                      **dict(zip(('w_in', 'lam_re', 'lam_im', 'log_dt', 'b_re', 'b_im',
                                                  'c_re', 'c_im', 'd_skip', 'w_glu', 'w_out'), ssm_params(s))))
        z, _ = trunk_layer(z, mods, mixer, ln_g[l], ln_b[l], ffn_w1[l], ffn_w3[l], ffn_w2[l])
    y_sample = z

    new_cache_k = jnp.stack(new_k, axis=1)
    new_cache_v = jnp.stack(new_v, axis=1)
    new_state_re = jnp.stack(new_sr, axis=1)
    new_state_im = jnp.stack(new_si, axis=1)
    return (y_prompt, y_sample, new_cache_k, new_cache_v, new_state_re, new_state_im)
```

```python
import functools
import math

import jax
import jax.numpy as jnp
import numpy as np
from jax import lax
from jax.experimental import pallas as pl
from jax.experimental.pallas import tpu as pltpu

F32 = jnp.float32
BF16 = jnp.bfloat16

D_MODEL = 1024
DEPTH = 4
N_HEADS = 16
N_KV_HEADS = 4
HEAD_DIM = 64
HEADS_PER_KV = N_HEADS // N_KV_HEADS
Q_WIDTH = N_HEADS * HEAD_DIM
KV_WIDTH = N_KV_HEADS * HEAD_DIM
GRID_W = 64
ATT_BLOCK = 128
ROPE_BASE = 10000.0
SSM_GROUP = 16
N_GROUPS = D_MODEL // SSM_GROUP
STATE = 64
SSM_CHUNK = 16
SSM_FLAT = SSM_CHUNK * SSM_GROUP
D_FF = -(-8 * D_MODEL // (3 * 256)) * 256
FF_CHUNK = 256
DN_ALPHA = (2 * DEPTH) ** 0.25
LN_EPS = 1e-5
NEG_INF = -1e30
N_MODS = 6
VMEM_LIMIT = 56 * 1024 * 1024
HIGHEST = lax.Precision.HIGHEST


def _params(n_axes):
    return pltpu.CompilerParams(dimension_semantics=("arbitrary",) * n_axes,
                                vmem_limit_bytes=VMEM_LIMIT)


def _const_spec(shape):
    return pl.BlockSpec(shape, lambda *_: (0,) * len(shape))


def _layer_norm(y, g, b):
    mu = jnp.mean(y, -1, keepdims=True)
    d = y - mu
    var = jnp.mean(d * d, -1, keepdims=True)
    return d * lax.rsqrt(var + LN_EPS) * g + b


def _silu(x):
    return x * jax.nn.sigmoid(x)


def _gelu_tanh(x):
    return 0.5 * x * (1.0 + jnp.tanh(math.sqrt(2.0 / math.pi) * (x + 0.044715 * (x * x * x))))


def _dot(a, b):
    return jnp.dot(a, b, preferred_element_type=F32)


def _dot_nt(a, b):
    return lax.dot_general(a, b, (((1,), (1,)), ((), ())), preferred_element_type=F32)


def _mods_kernel(c_ref, w_ref, b_ref, o_ref):
    a = _silu(c_ref[...]).astype(BF16)
    o_ref[0] = _dot(a, w_ref[0].astype(BF16)) + b_ref[0]


def _ada_mods(cvec, w_ada, b_ada):
    r = cvec.shape[0]
    tn = 1536
    out = pl.pallas_call(
        _mods_kernel,
        out_shape=jax.ShapeDtypeStruct((DEPTH, r, N_MODS * D_MODEL), F32),
        grid=(DEPTH, N_MODS * D_MODEL // tn),
        in_specs=[pl.BlockSpec((r, D_MODEL), lambda l, j: (0, 0)),
                  pl.BlockSpec((1, D_MODEL, tn), lambda l, j: (l, 0, j)),
                  pl.BlockSpec((1, 1, tn), lambda l, j: (l, 0, j))],
        out_specs=pl.BlockSpec((1, r, tn), lambda l, j: (l, 0, j)),
        compiler_params=_params(2),
        name="ada_mods",
    )(cvec, w_ada, b_ada.reshape(DEPTH, 1, N_MODS * D_MODEL))
    return out.reshape(DEPTH, r, N_MODS, D_MODEL)


def _rope(x, cos, sin, first_half):
    outs = []
    for c in range(x.shape[1] // 128):
        xc = x[:, c * 128:(c + 1) * 128]
        partner = jnp.where(first_half, pltpu.roll(xc, 128 - 16, 1), pltpu.roll(xc, 16, 1))
        outs.append(xc * cos + partner * sin)
    return jnp.concatenate(outs, axis=1)


def _qkv_ctx_kernel(x_ref, mod_ref, w_ref, q_ref, kb_ref, vb_ref, k_ref, v_ref):
    h = (x_ref[...] * (1.0 + mod_ref[0, 1:2, :]) + mod_ref[0, 0:1, :]).astype(BF16)
    qkv = _dot(h, w_ref[...])
    q_ref[...] = (qkv[:, :Q_WIDTH] * HEAD_DIM ** -0.5).astype(BF16)
    k = qkv[:, Q_WIDTH:Q_WIDTH + KV_WIDTH]
    v = qkv[:, Q_WIDTH + KV_WIDTH:]
    k_ref[...] = k
    v_ref[...] = v
    kb_ref[...] = k.astype(BF16)
    vb_ref[...] = v.astype(BF16)


def _qkv_lat_kernel(x_ref, mod_ref, w_ref, cos_ref, sin_ref, qp_ref, qr_ref, kr_ref, v_ref):
    h = (x_ref[...] * (1.0 + mod_ref[0, 1:2, :]) + mod_ref[0, 0:1, :]).astype(BF16)
    qkv = _dot(h, w_ref[...])
    cos = cos_ref[...]
    sin = sin_ref[...]
    first_half = (lax.broadcasted_iota(jnp.int32, cos.shape, 1) & 16) == 0
    q = qkv[:, :Q_WIDTH] * HEAD_DIM ** -0.5
    qp_ref[...] = q.astype(BF16)
    qr_ref[...] = _rope(q, cos, sin, first_half).astype(BF16)
    kr_ref[...] = _rope(qkv[:, Q_WIDTH:Q_WIDTH + KV_WIDTH], cos, sin, first_half).astype(BF16)
    v_ref[...] = qkv[:, Q_WIDTH + KV_WIDTH:].astype(BF16)


def _row_tile(rows_per_mod, want):
    tm = min(want, rows_per_mod)
    assert rows_per_mod % tm == 0
    return tm


def _qkv_ctx(x2, mods, w_qkv):
    r = x2.shape[0]
    tm = _row_tile(r, 512)
    row = lambda w: pl.BlockSpec((tm, w), lambda i: (i, 0))
    return pl.pallas_call(
        _qkv_ctx_kernel,
        out_shape=(jax.ShapeDtypeStruct((r, Q_WIDTH), BF16),
                   jax.ShapeDtypeStruct((r, KV_WIDTH), BF16),
                   jax.ShapeDtypeStruct((r, KV_WIDTH), BF16),
                   jax.ShapeDtypeStruct((r, KV_WIDTH), F32),
                   jax.ShapeDtypeStruct((r, KV_WIDTH), F32)),
        grid=(r // tm,),
        in_specs=[row(D_MODEL),
                  pl.BlockSpec((1, N_MODS, D_MODEL), lambda i: (0, 0, 0)),
                  _const_spec(w_qkv.shape)],
        out_specs=(row(Q_WIDTH), row(KV_WIDTH), row(KV_WIDTH), row(KV_WIDTH), row(KV_WIDTH)),
        compiler_params=_params(1),
        name="qkv_ctx",
    )(x2, mods, w_qkv)


def _qkv_lat(x2, mods, w_qkv, cos, sin, t):
    r = x2.shape[0]
    tm = _row_tile(t, 512)
    per_seq = t // tm
    row = lambda w: pl.BlockSpec((tm, w), lambda i: (i, 0))
    tab = pl.BlockSpec((tm, 128), lambda i: (i % per_seq, 0))
    return pl.pallas_call(
        _qkv_lat_kernel,
        out_shape=(jax.ShapeDtypeStruct((r, Q_WIDTH), BF16),
                   jax.ShapeDtypeStruct((r, Q_WIDTH), BF16),
                   jax.ShapeDtypeStruct((r, KV_WIDTH), BF16),
                   jax.ShapeDtypeStruct((r, KV_WIDTH), BF16)),
        grid=(r // tm,),
        in_specs=[row(D_MODEL),
                  pl.BlockSpec((1, N_MODS, D_MODEL), lambda i: (1 + i // per_seq, 0, 0)),
                  _const_spec(w_qkv.shape), tab, tab],
        out_specs=(row(Q_WIDTH), row(Q_WIDTH), row(KV_WIDTH), row(KV_WIDTH)),
        compiler_params=_params(1),
        name="qkv_lat",
    )(x2, mods, w_qkv, cos, sin)


def _proj_kernel(x_ref, mod_ref, w_ref, o_ref):
    h = (x_ref[...] * (1.0 + mod_ref[0, 1:2, :]) + mod_ref[0, 0:1, :]).astype(BF16)
    o_ref[...] = _dot(h, w_ref[...])


def _mod_proj(x2, mods, mod_base, rows_per_mod, w):
    r = x2.shape[0]
    tm = _row_tile(rows_per_mod, 512)
    per_mod = rows_per_mod // tm
    return pl.pallas_call(
        _proj_kernel,
        out_shape=jax.ShapeDtypeStruct((r, w.shape[1]), F32),
        grid=(r // tm,),
        in_specs=[pl.BlockSpec((tm, D_MODEL), lambda i: (i, 0)),
                  pl.BlockSpec((1, N_MODS, D_MODEL), lambda i: (mod_base + i // per_mod, 0, 0)),
                  _const_spec(w.shape)],
        out_specs=pl.BlockSpec((tm, w.shape[1]), lambda i: (i, 0)),
        compiler_params=_params(1),
        name="ssm_in_proj",
    )(x2, mods, w)


def _stack_heads(ref, kv):
    base = kv * HEADS_PER_KV * HEAD_DIM
    return jnp.concatenate(
        [ref[:, base + g * HEAD_DIM: base + (g + 1) * HEAD_DIM] for g in range(HEADS_PER_KV)], axis=0)


def _sink_column(sink_ref, kv, rows):
    return jnp.concatenate(
        [jnp.full((rows, 1), sink_ref[kv * HEADS_PER_KV + g], F32) for g in range(HEADS_PER_KV)], axis=0)


def _store_heads(o_scr, o, kv, rows):
    base = kv * HEADS_PER_KV * HEAD_DIM
    for g in range(HEADS_PER_KV):
        o_scr[:, base + g * HEAD_DIM: base + (g + 1) * HEAD_DIM] = o[g * rows:(g + 1) * rows].astype(BF16)


def _attn_epilogue(o_scr, x_ref, mod_ref, ln_ref, wo_ref, out_ref):
    y = _dot(o_scr[...], wo_ref[...])
    z = DN_ALPHA * x_ref[...] + mod_ref[0, 2:3, :] * y
    out_ref[...] = _layer_norm(z, ln_ref[0:1, :], ln_ref[1:2, :])


def _attn_ctx_kernel(sink_ref, q_ref, k_ref, v_ref, x_ref, mod_ref, ln_ref, wo_ref, out_ref, o_scr):
    rows = q_ref.shape[0]
    for kv in range(N_KV_HEADS):
        q4 = _stack_heads(q_ref, kv)
        kh = k_ref[:, kv * HEAD_DIM:(kv + 1) * HEAD_DIM]
        vh = v_ref[:, kv * HEAD_DIM:(kv + 1) * HEAD_DIM]
        sink = _sink_column(sink_ref, kv, rows)
        s = _dot_nt(q4, kh)
        m = jnp.maximum(jnp.max(s, -1, keepdims=True), sink)
        p = jnp.exp(s - m)
        den = jnp.sum(p, -1, keepdims=True) + jnp.exp(sink - m)
        o = _dot(p.astype(BF16), vh) / den
        _store_heads(o_scr, o, kv, rows)
    _attn_epilogue(o_scr, x_ref, mod_ref, ln_ref, wo_ref, out_ref)


def _attn_ctx(sink, q, k, v, x2, mods, ln, w_o, seq):
    r = x2.shape[0]
    row = lambda w: pl.BlockSpec((seq, w), lambda i: (i, 0))
    return pl.pallas_call(
        _attn_ctx_kernel,
        out_shape=jax.ShapeDtypeStruct((r, D_MODEL), F32),
        grid=(r // seq,),
        in_specs=[pl.BlockSpec(memory_space=pltpu.SMEM),
                  row(Q_WIDTH), row(KV_WIDTH), row(KV_WIDTH), row(D_MODEL),
                  pl.BlockSpec((1, N_MODS, D_MODEL), lambda i: (0, 0, 0)),
                  _const_spec(ln.shape), _const_spec(w_o.shape)],
        out_specs=row(D_MODEL),
        scratch_shapes=[pltpu.VMEM((seq, Q_WIDTH), BF16)],
        compiler_params=_params(1),
        name="attn_ctx",
    )(sink, q, k, v, x2, mods, ln, w_o)


def _attn_lat_kernel(sink_ref, qp_ref, qr_ref, kp_ref, kc_ref, kn_ref, vp_ref, vc_ref, vn_ref,
                     kx_ref, vx_ref, x_ref, mod_ref, ln_ref, wo_ref, out_ref, o_scr):
    j = pl.program_id(1)
    nblk = pl.num_programs(1)
    rows = ATT_BLOCK
    r_idx = lax.broadcasted_iota(jnp.int32, (HEADS_PER_KV * rows, rows), 0) % rows
    c_idx = lax.broadcasted_iota(jnp.int32, (HEADS_PER_KV * rows, rows), 1)
    prev_ok = c_idx >= r_idx + jnp.where(j > 0, 0, rows)
    next_ok = c_idx <= r_idx - jnp.where(j < nblk - 1, 0, rows)
    for kv in range(N_KV_HEADS):
        lo, hi = kv * HEAD_DIM, (kv + 1) * HEAD_DIM
        qp4 = _stack_heads(qp_ref, kv)
        qr4 = _stack_heads(qr_ref, kv)
        sink = _sink_column(sink_ref, kv, rows)
        s_x = _dot_nt(qp4, kx_ref[0, :, lo:hi])
        s_p = jnp.where(prev_ok, _dot_nt(qr4, kp_ref[:, lo:hi]), NEG_INF)
        s_c = _dot_nt(qr4, kc_ref[:, lo:hi])
        s_n = jnp.where(next_ok, _dot_nt(qr4, kn_ref[:, lo:hi]), NEG_INF)
        m = jnp.maximum(jnp.max(s_x, -1, keepdims=True), jnp.max(s_c, -1, keepdims=True))
        m = jnp.maximum(m, jnp.maximum(jnp.max(s_p, -1, keepdims=True), jnp.max(s_n, -1, keepdims=True)))
        m = jnp.maximum(m, sink)
        p_x = jnp.exp(s_x - m)
        p_p = jnp.exp(s_p - m)
        p_c = jnp.exp(s_c - m)
        p_n = jnp.exp(s_n - m)
        den = (jnp.sum(p_x, -1, keepdims=True) + jnp.sum(p_p, -1, keepdims=True)
               + jnp.sum(p_c, -1, keepdims=True) + jnp.sum(p_n, -1, keepdims=True) + jnp.exp(sink - m))
        o = (_dot(p_x.astype(BF16), vx_ref[0, :, lo:hi]) + _dot(p_p.astype(BF16), vp_ref[:, lo:hi])
             + _dot(p_c.astype(BF16), vc_ref[:, lo:hi]) + _dot(p_n.astype(BF16), vn_ref[:, lo:hi])) / den
        _store_heads(o_scr, o, kv, rows)
    _attn_epilogue(o_scr, x_ref, mod_ref, ln_ref, wo_ref, out_ref)


def _attn_lat(sink, qp, qr, kr, v, k_ctx, v_ctx, x2, mods, ln, w_o, n, t):
    nblk = t // ATT_BLOCK
    blk = ATT_BLOCK
    row = lambda w: pl.BlockSpec((blk, w), lambda b, j: (b * nblk + j, 0))
    prev = pl.BlockSpec((blk, KV_WIDTH), lambda b, j: (b * nblk + jnp.maximum(j - 1, 0), 0))
    nxt = pl.BlockSpec((blk, KV_WIDTH), lambda b, j: (b * nblk + jnp.minimum(j + 1, nblk - 1), 0))
    ctx = pl.BlockSpec((1,) + k_ctx.shape[1:], lambda b, j: (b, 0, 0))
    return pl.pallas_call(
        _attn_lat_kernel,
        out_shape=jax.ShapeDtypeStruct((n * t, D_MODEL), F32),
        grid=(n, nblk),
        in_specs=[pl.BlockSpec(memory_space=pltpu.SMEM),
                  row(Q_WIDTH), row(Q_WIDTH),
                  prev, row(KV_WIDTH), nxt, prev, row(KV_WIDTH), nxt,
                  ctx, ctx, row(D_MODEL),
                  pl.BlockSpec((1, N_MODS, D_MODEL), lambda b, j: (1 + b, 0, 0)),
                  _const_spec(ln.shape), _const_spec(w_o.shape)],
        out_specs=row(D_MODEL),
        scratch_shapes=[pltpu.VMEM((blk, Q_WIDTH), BF16)],
        compiler_params=_params(2),
        name="attn_lat",
    )(sink, qp, qr, kr, kr, kr, v, v, v, k_ctx, v_ctx, x2, mods, ln, w_o)


def _ffn_kernel(x_ref, mod_ref, ln_ref, w1_ref, w3_ref, w2_ref, out_ref, acc_ref):
    x = x_ref[...]
    h = (x * (1.0 + mod_ref[0, 4:5, :]) + mod_ref[0, 3:4, :]).astype(BF16)
    for c in range(D_FF // FF_CHUNK):
        lo, hi = c * FF_CHUNK, (c + 1) * FF_CHUNK
        a = _dot(h, w1_ref[:, lo:hi])
        b = _dot(h, w3_ref[:, lo:hi])
        y = _dot((_silu(a) * b).astype(BF16), w2_ref[lo:hi, :])
        if c == 0:
            acc_ref[...] = y
        else:
            acc_ref[...] += y
    z = DN_ALPHA * x + mod_ref[0, 5:6, :] * acc_ref[...]
    out_ref[...] = _layer_norm(z, ln_ref[0:1, :], ln_ref[1:2, :])


def _ffn(x2, mods, mod_base, rows_per_mod, ln, w1, w3, w2):
    r = x2.shape[0]
    tm = _row_tile(rows_per_mod, 512)
    per_mod = rows_per_mod // tm
    return pl.pallas_call(
        _ffn_kernel,
        out_shape=jax.ShapeDtypeStruct((r, D_MODEL), F32),
        grid=(r // tm,),
        in_specs=[pl.BlockSpec((tm, D_MODEL), lambda i: (i, 0)),
                  pl.BlockSpec((1, N_MODS, D_MODEL), lambda i: (mod_base + i // per_mod, 0, 0)),
                  _const_spec(ln.shape), _const_spec(w1.shape), _const_spec(w3.shape),
                  _const_spec(w2.shape)],
        out_specs=pl.BlockSpec((tm, D_MODEL), lambda i: (i, 0)),
        scratch_shapes=[pltpu.VMEM((tm, D_MODEL), F32)],
        compiler_params=_params(1),
        name="ffn",
    )(x2, mods, ln, w1, w3, w2)


def _complex_powers(a_re, a_im, n):
    pr, pi = [jnp.ones_like(a_re)], [jnp.zeros_like(a_im)]
    for _ in range(n):
        r, i = pr[-1], pi[-1]
        pr.append(r * a_re - i * a_im)
        pi.append(r * a_im + i * a_re)
    return jnp.stack(pr), jnp.stack(pi)


def _s5_operators(lam_re, lam_im, log_dt, b_re, b_im, c_re, c_im, d_skip, n_steps):
    lc = SSM_CHUNK
    dt = jnp.exp(log_dt)[..., None]
    mag = jnp.exp(lam_re * dt)
    a_re = mag * jnp.cos(lam_im * dt)
    a_im = mag * jnp.sin(lam_im * dt)
    den = lam_re ** 2 + lam_im ** 2
    nr, ni = a_re - 1.0, a_im
    coef_re = ((nr * lam_re + ni * lam_im) / den)[..., None]
    coef_im = ((ni * lam_re - nr * lam_im) / den)[..., None]
    bb_re = coef_re * b_re - coef_im * b_im
    bb_im = coef_re * b_im + coef_im * b_re
    pw_re, pw_im = _complex_powers(a_re, a_im, lc)

    ab_re = pw_re[:lc, ..., None] * bb_re - pw_im[:lc, ..., None] * bb_im
    ab_im = pw_re[:lc, ..., None] * bb_im + pw_im[:lc, ..., None] * bb_re
    k_resp = (jnp.einsum('dgmp,kdgpn->dkgmn', c_re, ab_re, precision=HIGHEST)
              - jnp.einsum('dgmp,kdgpn->dkgmn', c_im, ab_im, precision=HIGHEST))
    s_idx = np.arange(lc)[:, None]
    t_idx = np.arange(lc)[None, :]
    lag_f = np.clip(t_idx - s_idx, 0, lc - 1)
    lag_b = np.clip(s_idx - t_idx, 0, lc - 1)
    resp = (jnp.where((t_idx >= s_idx)[..., None, None, None], k_resp[0][lag_f], 0.0)
            + jnp.where((s_idx >= t_idx)[..., None, None, None], k_resp[1][lag_b], 0.0))
    m_op = resp.transpose(2, 0, 4, 1, 3).reshape(N_GROUPS, SSM_FLAT, SSM_FLAT)

    sf_re, sf_im = ab_re[::-1, 0], ab_im[::-1, 0]
    sb_re, sb_im = ab_re[:, 1], ab_im[:, 1]
    s_op = jnp.stack([sf_re, sb_re, sf_im, sb_im], 0)
    s_op = s_op.transpose(2, 1, 4, 0, 3).reshape(N_GROUPS, SSM_FLAT, 4 * STATE)

    def readout(d, pr, pi):
        w_re = c_re[d][None] * pr[:, :, None, :] - c_im[d][None] * pi[:, :, None, :]
        w_im = c_re[d][None] * pi[:, :, None, :] + c_im[d][None] * pr[:, :, None, :]
        return w_re, -w_im
    rf_re, rf_im = readout(0, pw_re[1:lc + 1, 0], pw_im[1:lc + 1, 0])
    rb_re, rb_im = readout(1, pw_re[lc:0:-1, 1], pw_im[lc:0:-1, 1])
    r_op = jnp.stack([rf_re, rb_re, rf_im, rb_im], 0)
    r_op = r_op.transpose(2, 0, 4, 1, 3).reshape(N_GROUPS, 4 * STATE, SSM_FLAT)

    sq_re, sq_im = [pw_re[lc]], [pw_im[lc]]
    for _ in range(n_steps - 1):
        r, i = sq_re[-1], sq_im[-1]
        sq_re.append(r * r - i * i)
        sq_im.append(2.0 * r * i)
    lanes = lambda x: jnp.concatenate([x[0], x[1]], -1)
    a_pow = jnp.stack([lanes(x) for x in sq_re] + [lanes(x) for x in sq_im], 1)
    d_row = jnp.tile(d_skip.reshape(N_GROUPS, 1, SSM_GROUP), (1, 1, lc))
    return m_op.astype(BF16), s_op.astype(BF16), r_op.astype(BF16), a_pow, d_row


def _ssm_core_kernel(*refs, seq_chunks, n_steps, groups, has_h0, emit_state):
    refs = list(refs)
    u_ref, m_ref, s_ref, r_ref, ap_ref, d_ref = refs[:6]
    refs = refs[6:]
    h0_ref = refs.pop(0) if has_h0 else None
    y_ref = refs.pop(0)
    e_ref = refs.pop(0) if emit_state else None
    rows = u_ref.shape[2]
    half = 2 * STATE
    pos = lax.broadcasted_iota(jnp.int32, (rows, half), 0) % seq_chunks
    is_fwd = lax.broadcasted_iota(jnp.int32, (rows, half), 1) < STATE
    n_upstream = jnp.where(is_fwd, pos, seq_chunks - 1 - pos)

    def neighbour(x, d):
        return jnp.where(is_fwd, pltpu.roll(x, d, 0), pltpu.roll(x, rows - d, 0))

    def upstream_ok(d):
        return n_upstream >= d

    for g in range(groups):
        u = u_ref[0, g]
        ub = u.astype(BF16)
        y = _dot(ub, m_ref[g]) + d_ref[g] * u
        h = _dot(ub, s_ref[g])
        hr, hi = h[:, :half], h[:, half:]
        a_re, a_im = ap_ref[g, 0:1, :], ap_ref[g, n_steps:n_steps + 1, :]
        if has_h0:
            h0r, h0i = h0_ref[0, g, 0:1, :half], h0_ref[0, g, 0:1, half:]
            first = n_upstream == 0
            hr = hr + jnp.where(first, a_re * h0r - a_im * h0i, 0.0)
            hi = hi + jnp.where(first, a_re * h0i + a_im * h0r, 0.0)
        for i in range(n_steps):
            d = 1 << i
            ok = upstream_ok(d)
            sr = jnp.where(ok, neighbour(hr, d), 0.0)
            si = jnp.where(ok, neighbour(hi, d), 0.0)
            p_re, p_im = ap_ref[g, i:i + 1, :], ap_ref[g, n_steps + i:n_steps + i + 1, :]
            hr, hi = hr + p_re * sr - p_im * si, hi + p_re * si + p_im * sr
        if emit_state:
            e_ref[0, g] = jnp.concatenate([hr, hi], axis=1)
        ok = upstream_ok(1)
        if has_h0:
            in_r = jnp.where(ok, neighbour(hr, 1), h0r)
            in_i = jnp.where(ok, neighbour(hi, 1), h0i)
        else:
            in_r = jnp.where(ok, neighbour(hr, 1), 0.0)
            in_i = jnp.where(ok, neighbour(hi, 1), 0.0)
        y = y + _dot(jnp.concatenate([in_r, in_i], axis=1).astype(BF16), r_ref[g])
        y_ref[0, g] = _gelu_tanh(y).astype(BF16)


def _ssm_core(u_flat, ops, h0, seq_chunks, emit_state):
    nb, _, rows, _ = u_flat.shape
    m_op, s_op, r_op, a_pow, d_row = ops
    n_steps = a_pow.shape[1] // 2
    assert (1 << n_steps) == seq_chunks
    gt = 8
    blk = pl.BlockSpec((1, gt, rows, SSM_FLAT), lambda b, g: (b, g, 0, 0))
    grp = lambda a: pl.BlockSpec((gt,) + a.shape[1:], lambda b, g: (g, 0, 0))
    in_specs = [blk, grp(m_op), grp(s_op), grp(r_op), grp(a_pow), grp(d_row)]
    args = [u_flat, m_op, s_op, r_op, a_pow, d_row]
    if h0 is not None:
        in_specs.append(pl.BlockSpec((1, gt, 1, 4 * STATE), lambda b, g: (b, g, 0, 0)))
        args.append(h0)
    out_shape = [jax.ShapeDtypeStruct(u_flat.shape, BF16)]
    out_specs = [blk]
    if emit_state:
        out_shape.append(jax.ShapeDtypeStruct(u_flat.shape, F32))
        out_specs.append(blk)
    kern = functools.partial(_ssm_core_kernel, seq_chunks=seq_chunks, n_steps=n_steps, groups=gt,
                             has_h0=h0 is not None, emit_state=emit_state)
    return pl.pallas_call(
        kern, out_shape=tuple(out_shape), grid=(nb, N_GROUPS // gt),
        in_specs=in_specs, out_specs=tuple(out_specs),
        compiler_params=_params(2), name="ssm_core",
    )(*args)


def _ssm_out_kernel(y_ref, x_ref, mod_ref, ln_ref, wg_ref, wo_ref, out_ref):
    vg = _dot(y_ref[...], wg_ref[...])
    glu = (vg[:, :D_MODEL] * jax.nn.sigmoid(vg[:, D_MODEL:])).astype(BF16)
    z = DN_ALPHA * x_ref[...] + mod_ref[0, 2:3, :] * _dot(glu, wo_ref[...])
    out_ref[...] = _layer_norm(z, ln_ref[0:1, :], ln_ref[1:2, :])


def _ssm_out(y2, x2, mods, mod_base, rows_per_mod, ln, w_glu, w_out):
    r = x2.shape[0]
    tm = _row_tile(rows_per_mod, 512)
    per_mod = rows_per_mod // tm
    row = pl.BlockSpec((tm, D_MODEL), lambda i: (i, 0))
    return pl.pallas_call(
        _ssm_out_kernel,
        out_shape=jax.ShapeDtypeStruct((r, D_MODEL), F32),
        grid=(r // tm,),
        in_specs=[row, row,
                  pl.BlockSpec((1, N_MODS, D_MODEL), lambda i: (mod_base + i // per_mod, 0, 0)),
                  _const_spec(ln.shape), _const_spec(w_glu.shape), _const_spec(w_out.shape)],
        out_specs=row,
        compiler_params=_params(1),
        name="ssm_out",
    )(y2, x2, mods, ln, w_glu, w_out)


def _to_chunk_rows(u, seqs_as_rows):
    n, t, _ = u.shape
    u5 = u.reshape(n, t // SSM_CHUNK, SSM_CHUNK, N_GROUPS, SSM_GROUP)
    if seqs_as_rows:
        return u5.transpose(3, 0, 1, 2, 4).reshape(1, N_GROUPS, n * (t // SSM_CHUNK), SSM_FLAT)
    return u5.transpose(0, 3, 1, 2, 4).reshape(n, N_GROUPS, t // SSM_CHUNK, SSM_FLAT)


def _from_chunk_rows(y, n, t, seqs_as_rows):
    if seqs_as_rows:
        y5 = y.reshape(N_GROUPS, n, t // SSM_CHUNK, SSM_CHUNK, SSM_GROUP).transpose(1, 2, 3, 0, 4)
    else:
        y5 = y.reshape(n, N_GROUPS, t // SSM_CHUNK, SSM_CHUNK, SSM_GROUP).transpose(0, 2, 3, 1, 4)
    return y5.reshape(n * t, D_MODEL)


def _ssm_mixer(x2, n, t, mods, mod_base, rows_per_mod, ln, w_in, ops, w_glu, w_out, h0, ctx):
    u = _mod_proj(x2, mods, mod_base, rows_per_mod, w_in)
    u_flat = _to_chunk_rows(u.reshape(n, t, D_MODEL), ctx)
    outs = _ssm_core(u_flat, ops, h0, t // SSM_CHUNK, emit_state=ctx)
    y2 = _from_chunk_rows(outs[0], n, t, ctx)
    x_new = _ssm_out(y2, x2, mods, mod_base, rows_per_mod, ln, w_glu, w_out)
    return x_new, (outs[1] if ctx else None)


def _rope_tables(t):
    n_rows = t // GRID_W
    row = jnp.repeat(jnp.arange(n_rows, dtype=F32), GRID_W)
    col = (jnp.arange(t) % GRID_W).astype(F32)
    nfreq = HEAD_DIM // 4
    inv = jnp.power(ROPE_BASE, -jnp.arange(nfreq, dtype=F32) / nfreq)
    ang_r, ang_c = row[:, None] * inv, col[:, None] * inv
    cos = jnp.concatenate([jnp.cos(ang_r)] * 2 + [jnp.cos(ang_c)] * 2, -1)
    sin = jnp.concatenate([-jnp.sin(ang_r), jnp.sin(ang_r), -jnp.sin(ang_c), jnp.sin(ang_c)], -1)
    return jnp.tile(cos, (1, 2)), jnp.tile(sin, (1, 2))


def kernel(x_prompt, x_sample, cache_k, cache_v, state_ssm_re, state_ssm_im, c, c_ctx, w_ada, b_ada, ln_g, ln_b, w_qkv, w_o, attn_sink, ssm_w_in, ssm_lam_re, ssm_lam_im, ssm_log_dt, ssm_b_re, ssm_b_im, ssm_c_re, ssm_c_im, ssm_d, ssm_w_glu, ssm_w_out, ffn_w1, ffn_w3, ffn_w2):
    nc, tc, _ = x_prompt.shape
    nl, tl, _ = x_sample.shape
    n_attn = w_qkv.shape[0]
    n_ssm = ssm_w_in.shape[0]

    n_mod_rows = -(-(1 + nl) // 8) * 8
    cvec = jnp.zeros((n_mod_rows, D_MODEL), F32).at[0].set(c_ctx).at[1:1 + nl].set(c)
    mods_all = _ada_mods(cvec, w_ada, b_ada)
    ln_all = jnp.stack([ln_g, ln_b], axis=2)

    bf = lambda w: w.astype(BF16)
    w_qkv_b, w_o_b = bf(w_qkv), bf(w_o)
    w_in_b, w_glu_b, w_out_b = bf(ssm_w_in), bf(ssm_w_glu), bf(ssm_w_out)
    w1_b, w3_b, w2_b = bf(ffn_w1), bf(ffn_w3), bf(ffn_w2)
    cos, sin = _rope_tables(tl)
    kx = bf(cache_k).reshape(nl, n_attn, -1, KV_WIDTH)
    vx = bf(cache_v).reshape(nl, n_attn, -1, KV_WIDTH)

    steps_c = (tc // SSM_CHUNK).bit_length() - 1
    steps_l = (tl // SSM_CHUNK).bit_length() - 1
    ssm_ops = []
    for s in range(n_ssm):
        prm = (ssm_lam_re[s], ssm_lam_im[s], ssm_log_dt[s], ssm_b_re[s], ssm_b_im[s],
               ssm_c_re[s], ssm_c_im[s], ssm_d[s])
        full = _s5_operators(*prm, max(steps_c, steps_l))
        ssm_ops.append(full)

    def ops_for(s, n_steps):
        m_op, s_op, r_op, a_pow, d_row = ssm_ops[s]
        total = a_pow.shape[1] // 2
        a_sel = jnp.concatenate([a_pow[:, :n_steps], a_pow[:, total:total + n_steps]], 1)
        return m_op, s_op, r_op, a_sel, d_row

    xc = x_prompt.reshape(nc * tc, D_MODEL)
    xl = x_sample.reshape(nl * tl, D_MODEL)
    new_k, new_v, new_sr, new_si = [], [], [], []
    for l in range(DEPTH):
        mods = mods_all[l]
        ln1, ln2 = ln_all[l, 0], ln_all[l, 1]
        if l % 2 == 0:
            a = l // 2
            q, kb, vb, k32, v32 = _qkv_ctx(xc, mods, w_qkv_b[a])
            new_k.append(k32.reshape(nc, tc, N_KV_HEADS, HEAD_DIM))
            new_v.append(v32.reshape(nc, tc, N_KV_HEADS, HEAD_DIM))
            xc = _attn_ctx(attn_sink[a], q, kb, vb, xc, mods, ln1, w_o_b[a], tc)
            qp, qr, kr, v = _qkv_lat(xl, mods, w_qkv_b[a], cos, sin, tl)
            xl = _attn_lat(attn_sink[a], qp, qr, kr, v, kx[:, a], vx[:, a], xl, mods, ln1, w_o_b[a], nl, tl)
        else:
            s = l // 2
            xc, e_c = _ssm_mixer(xc, nc, tc, mods, 0, nc * tc, ln1, w_in_b[s], ops_for(s, steps_c),
                                 w_glu_b[s], w_out_b[s], None, True)
            e5 = e_c.reshape(N_GROUPS, nc, tc // SSM_CHUNK, 4, STATE)
            fin = jnp.stack([e5[:, :, -1, 0], e5[:, :, 0, 1], e5[:, :, -1, 2], e5[:, :, 0, 3]], 0)
            fin = fin.transpose(0, 2, 1, 3)
            new_sr.append(jnp.stack([fin[0], fin[1]], 1))
            new_si.append(jnp.stack([fin[2], fin[3]], 1))
            h0 = jnp.concatenate([state_ssm_re[:, s, 0], state_ssm_re[:, s, 1],
                                  state_ssm_im[:, s, 0], state_ssm_im[:, s, 1]], -1)
            xl, _ = _ssm_mixer(xl, nl, tl, mods, 1, tl, ln1, w_in_b[s], ops_for(s, steps_l),
                               w_glu_b[s], w_out_b[s], h0[:, :, None, :], False)
        xc = _ffn(xc, mods, 0, nc * tc, ln2, w1_b[l], w3_b[l], w2_b[l])
        xl = _ffn(xl, mods, 1, tl, ln2, w1_b[l], w3_b[l], w2_b[l])

    return (xc.reshape(nc, tc, D_MODEL), xl.reshape(nl, tl, D_MODEL),
            jnp.stack(new_k, axis=1), jnp.stack(new_v, axis=1),
            jnp.stack(new_sr, axis=1), jnp.stack(new_si, axis=1))
```

```python
import functools
import math

import jax
import jax.numpy as jnp
from jax import lax
from jax.experimental import pallas as pl
from jax.experimental.pallas import tpu as pltpu

F32 = jnp.float32
BF16 = jnp.bfloat16

D_MODEL = 1024
DEPTH = 4
N_HEADS = 16
N_KV_HEADS = 4
HEAD_DIM = 64
HEADS_PER_KV = N_HEADS // N_KV_HEADS
Q_WIDTH = N_HEADS * HEAD_DIM
KV_WIDTH = N_KV_HEADS * HEAD_DIM
GRID_W = 64
ATT_BLOCK = 128
ROPE_BASE = 10000.0
SSM_GROUP = 16
N_GROUPS = D_MODEL // SSM_GROUP
STATE = 64
SSM_CHUNK = 16
SSM_FLAT = SSM_CHUNK * SSM_GROUP
D_FF = -(-8 * D_MODEL // (3 * 256)) * 256
FF_CHUNK = 256
DN_ALPHA = (2 * DEPTH) ** 0.25
LN_EPS = 1e-5
NEG_INF = -1e30
N_MODS = 6
VMEM_LIMIT = 56 * 1024 * 1024
HIGHEST = lax.Precision.HIGHEST


def _params(n_axes):
    return pltpu.CompilerParams(dimension_semantics=("arbitrary",) * n_axes,
                                vmem_limit_bytes=VMEM_LIMIT)


def _const_spec(shape):
    return pl.BlockSpec(shape, lambda *_: (0,) * len(shape))


def _layer_norm(y, g, b):
    mu = jnp.mean(y, -1, keepdims=True)
    d = y - mu
    var = jnp.mean(d * d, -1, keepdims=True)
    return d * lax.rsqrt(var + LN_EPS) * g + b


def _silu(x):
    return x * jax.nn.sigmoid(x)


def _gelu_tanh(x):
    return 0.5 * x * (1.0 + jnp.tanh(math.sqrt(2.0 / math.pi) * (x + 0.044715 * (x * x * x))))


def _dot(a, b):
    return jnp.dot(a, b, preferred_element_type=F32)


def _dot_nt(a, b):
    return lax.dot_general(a, b, (((1,), (1,)), ((), ())), preferred_element_type=F32)


def _row_tile(rows_per_mod, want):
    tm = min(want, rows_per_mod)
    assert rows_per_mod % tm == 0
    return tm


def _mods_kernel(c_ref, w_ref, b_ref, o_ref):
    a = _silu(c_ref[...]).astype(BF16)
    o_ref[0] = _dot(a, w_ref[0].astype(BF16)) + b_ref[0]


def _ada_mods(cvec, w_ada, b_ada):
    r = cvec.shape[0]
    tn = 1536
    out = pl.pallas_call(
        _mods_kernel,
        out_shape=jax.ShapeDtypeStruct((DEPTH, r, N_MODS * D_MODEL), F32),
        grid=(DEPTH, N_MODS * D_MODEL // tn),
        in_specs=[pl.BlockSpec((r, D_MODEL), lambda l, j: (0, 0)),
                  pl.BlockSpec((1, D_MODEL, tn), lambda l, j: (l, 0, j)),
                  pl.BlockSpec((1, 1, tn), lambda l, j: (l, 0, j))],
        out_specs=pl.BlockSpec((1, r, tn), lambda l, j: (l, 0, j)),
        compiler_params=_params(2),
        name="ada_mods",
    )(cvec, w_ada, b_ada.reshape(DEPTH, 1, N_MODS * D_MODEL))
    return out.reshape(DEPTH, r, N_MODS, D_MODEL)


Q_SCALE = HEAD_DIM ** -0.5 * math.log2(math.e)
V_EXT = N_KV_HEADS * 128


def _rope(x, cos, sin, first_half):
    outs = []
    for c in range(x.shape[1] // 128):
        xc = x[:, c * 128:(c + 1) * 128]
        partner = jnp.where(first_half, pltpu.roll(xc, 128 - 16, 1), pltpu.roll(xc, 16, 1))
        outs.append(xc * cos + partner * sin)
    return jnp.concatenate(outs, axis=1)


def _ones_lanes(width):
    return jnp.where((lax.broadcasted_iota(jnp.int32, (1, width), 1) % 128) >= HEAD_DIM, 1.0, 0.0)


def _qkv_ctx_kernel(x_ref, mod_ref, w_ref, q_ref, kb_ref, vb_ref, k_ref, v_ref):
    h = (x_ref[...] * (1.0 + mod_ref[0, 1:2, :]) + mod_ref[0, 0:1, :]).astype(BF16)
    qkv = _dot(h, w_ref[...])
    q_ref[...] = (qkv[:, :Q_WIDTH] * Q_SCALE).astype(BF16)
    k = qkv[:, Q_WIDTH:Q_WIDTH + KV_WIDTH]
    k_ref[...] = k
    v_ref[...] = qkv[:, Q_WIDTH + KV_WIDTH:Q_WIDTH + 2 * KV_WIDTH]
    kb_ref[...] = k.astype(BF16)
    vb_ref[...] = (qkv[:, Q_WIDTH + 2 * KV_WIDTH:] + _ones_lanes(V_EXT)).astype(BF16)


def _qkv_lat_kernel(x_ref, mod_ref, w_ref, cos_ref, sin_ref, qp_ref, qr_ref, kr_ref, v_ref):
    h = (x_ref[...] * (1.0 + mod_ref[0, 1:2, :]) + mod_ref[0, 0:1, :]).astype(BF16)
    qkv = _dot(h, w_ref[...])
    cos = cos_ref[...]
    sin = sin_ref[...]
    first_half = (lax.broadcasted_iota(jnp.int32, cos.shape, 1) & 16) == 0
    q = qkv[:, :Q_WIDTH] * Q_SCALE
    qp_ref[...] = q.astype(BF16)
    qr_ref[...] = _rope(q, cos, sin, first_half).astype(BF16)
    kr_ref[...] = _rope(qkv[:, Q_WIDTH:Q_WIDTH + KV_WIDTH], cos, sin, first_half).astype(BF16)
    v_ref[...] = (qkv[:, Q_WIDTH + KV_WIDTH:] + _ones_lanes(V_EXT)).astype(BF16)


def _qkv_ctx(x2, mods, w_ext):
    r = x2.shape[0]
    tm = _row_tile(r, 512)
    row = lambda w: pl.BlockSpec((tm, w), lambda i: (i, 0))
    return pl.pallas_call(
        _qkv_ctx_kernel,
        out_shape=(jax.ShapeDtypeStruct((r, Q_WIDTH), BF16),
                   jax.ShapeDtypeStruct((r, KV_WIDTH), BF16),
                   jax.ShapeDtypeStruct((r, V_EXT), BF16),
                   jax.ShapeDtypeStruct((r, KV_WIDTH), F32),
                   jax.ShapeDtypeStruct((r, KV_WIDTH), F32)),
        grid=(r // tm,),
        in_specs=[row(D_MODEL),
                  pl.BlockSpec((1, N_MODS, D_MODEL), lambda i: (0, 0, 0)),
                  _const_spec(w_ext.shape)],
        out_specs=(row(Q_WIDTH), row(KV_WIDTH), row(V_EXT), row(KV_WIDTH), row(KV_WIDTH)),
        compiler_params=_params(1),
        name="qkv_ctx",
    )(x2, mods, w_ext)


def _qkv_lat(x2, mods, w_ext, cos, sin, t):
    r = x2.shape[0]
    tm = _row_tile(t, 512)
    per_seq = t // tm
    row = lambda w: pl.BlockSpec((tm, w), lambda i: (i, 0))
    tab = pl.BlockSpec((tm, 128), lambda i: (i % per_seq, 0))
    return pl.pallas_call(
        _qkv_lat_kernel,
        out_shape=(jax.ShapeDtypeStruct((r, Q_WIDTH), BF16),
                   jax.ShapeDtypeStruct((r, Q_WIDTH), BF16),
                   jax.ShapeDtypeStruct((r, KV_WIDTH), BF16),
                   jax.ShapeDtypeStruct((r, V_EXT), BF16)),
        grid=(r // tm,),
        in_specs=[row(D_MODEL),
                  pl.BlockSpec((1, N_MODS, D_MODEL), lambda i: (1 + i // per_seq, 0, 0)),
                  _const_spec(w_ext.shape), tab, tab],
        out_specs=(row(Q_WIDTH), row(Q_WIDTH), row(KV_WIDTH), row(V_EXT)),
        compiler_params=_params(1),
        name="qkv_lat",
    )(x2, mods, w_ext, cos, sin)


def _stack_heads(ref, kv):
    base = kv * HEADS_PER_KV * HEAD_DIM
    return jnp.concatenate(
        [ref[:, base + g * HEAD_DIM: base + (g + 1) * HEAD_DIM] for g in range(HEADS_PER_KV)], axis=0)


def _sink_column(sink_ref, kv, rows):
    return jnp.concatenate(
        [jnp.full((rows, 1), sink_ref[kv * HEADS_PER_KV + g] * math.log2(math.e), F32)
         for g in range(HEADS_PER_KV)], axis=0)


def _normalise_and_store(o_scr, o_ext, sink_weight, kv, rows):
    den = pltpu.roll(o_ext, HEAD_DIM, 1) + sink_weight
    o = o_ext / den
    base = kv * HEADS_PER_KV * HEAD_DIM
    for g in range(HEADS_PER_KV):
        o_scr[:, base + g * HEAD_DIM: base + (g + 1) * HEAD_DIM] = (
            o[g * rows:(g + 1) * rows, :HEAD_DIM].astype(BF16))


def _attn_epilogue(o_scr, x_ref, mod_ref, ln_ref, wo_ref, out_ref):
    y = _dot(o_scr[...], wo_ref[...])
    z = DN_ALPHA * x_ref[...] + mod_ref[0, 2:3, :] * y
    out_ref[...] = _layer_norm(z, ln_ref[0:1, :], ln_ref[1:2, :])


def _attn_ctx_kernel(sink_ref, q_ref, k_ref, v_ref, x_ref, mod_ref, ln_ref, wo_ref, out_ref, o_scr):
    rows = q_ref.shape[0]
    for kv in range(N_KV_HEADS):
        q4 = _stack_heads(q_ref, kv)
        sink = _sink_column(sink_ref, kv, rows)
        s = _dot_nt(q4, k_ref[:, kv * HEAD_DIM:(kv + 1) * HEAD_DIM])
        m = jnp.maximum(jnp.max(s, -1, keepdims=True), sink)
        p = jnp.exp2(s - m).astype(BF16)
        o_ext = _dot(p, v_ref[:, kv * 128:(kv + 1) * 128])
        _normalise_and_store(o_scr, o_ext, jnp.exp2(sink - m), kv, rows)
    _attn_epilogue(o_scr, x_ref, mod_ref, ln_ref, wo_ref, out_ref)


def _attn_ctx(sink, q, k, v_ext, x2, mods, ln, w_o, seq):
    r = x2.shape[0]
    row = lambda w: pl.BlockSpec((seq, w), lambda i: (i, 0))
    return pl.pallas_call(
        _attn_ctx_kernel,
        out_shape=jax.ShapeDtypeStruct((r, D_MODEL), F32),
        grid=(r // seq,),
        in_specs=[pl.BlockSpec(memory_space=pltpu.SMEM),
                  row(Q_WIDTH), row(KV_WIDTH), row(V_EXT), row(D_MODEL),
                  pl.BlockSpec((1, N_MODS, D_MODEL), lambda i: (0, 0, 0)),
                  _const_spec(ln.shape), _const_spec(w_o.shape)],
        out_specs=row(D_MODEL),
        scratch_shapes=[pltpu.VMEM((seq, Q_WIDTH), BF16)],
        compiler_params=_params(1),
        name="attn_ctx",
    )(sink, q, k, v_ext, x2, mods, ln, w_o)


LOCAL_KEYS = 3 * ATT_BLOCK


def _attn_lat_kernel(sink_ref, qp_ref, qr_ref, k_ref, v_ref, kx_ref, vx_ref, x_ref, mod_ref, ln_ref,
                     wo_ref, out_ref, o_scr):
    j = pl.program_id(1)
    rows = ATT_BLOCK
    seq = k_ref.shape[0]
    start = pl.multiple_of(jnp.clip((j - 1) * rows, 0, seq - LOCAL_KEYS), rows)
    r_idx = lax.broadcasted_iota(jnp.int32, (HEADS_PER_KV * rows, LOCAL_KEYS), 0) % rows
    c_idx = lax.broadcasted_iota(jnp.int32, (HEADS_PER_KV * rows, LOCAL_KEYS), 1)
    in_window = jnp.abs(r_idx - c_idx + (j * rows - start)) <= ATT_BLOCK
    k_loc = k_ref[pl.ds(start, LOCAL_KEYS), :]
    v_loc = v_ref[pl.ds(start, LOCAL_KEYS), :]
    for kv in range(N_KV_HEADS):
        lo, hi = kv * HEAD_DIM, (kv + 1) * HEAD_DIM
        sink = _sink_column(sink_ref, kv, rows)
        s_x = _dot_nt(_stack_heads(qp_ref, kv), kx_ref[0, :, lo:hi])
        s_l = jnp.where(in_window, _dot_nt(_stack_heads(qr_ref, kv), k_loc[:, lo:hi]), NEG_INF)
        m = jnp.maximum(jnp.maximum(jnp.max(s_x, -1, keepdims=True), jnp.max(s_l, -1, keepdims=True)), sink)
        p_x = jnp.exp2(s_x - m).astype(BF16)
        p_l = jnp.exp2(s_l - m).astype(BF16)
        o_ext = _dot(p_x, vx_ref[0, :, kv * 128:(kv + 1) * 128]) + _dot(p_l, v_loc[:, kv * 128:(kv + 1) * 128])
        _normalise_and_store(o_scr, o_ext, jnp.exp2(sink - m), kv, rows)
    _attn_epilogue(o_scr, x_ref, mod_ref, ln_ref, wo_ref, out_ref)


def _attn_lat(sink, qp, qr, kr, v_ext, k_ctx, v_ctx_ext, x2, mods, ln, w_o, n, t):
    assert t >= LOCAL_KEYS
    nblk = t // ATT_BLOCK
    blk = ATT_BLOCK
    row = lambda w: pl.BlockSpec((blk, w), lambda b, j: (b * nblk + j, 0))
    seq = lambda w: pl.BlockSpec((t, w), lambda b, j: (b, 0))
    ctx = lambda a: pl.BlockSpec((1,) + a.shape[1:], lambda b, j: (b, 0, 0))
    return pl.pallas_call(
        _attn_lat_kernel,
        out_shape=jax.ShapeDtypeStruct((n * t, D_MODEL), F32),
        grid=(n, nblk),
        in_specs=[pl.BlockSpec(memory_space=pltpu.SMEM),
                  row(Q_WIDTH), row(Q_WIDTH), seq(KV_WIDTH), seq(V_EXT),
                  ctx(k_ctx), ctx(v_ctx_ext), row(D_MODEL),
                  pl.BlockSpec((1, N_MODS, D_MODEL), lambda b, j: (1 + b, 0, 0)),
                  _const_spec(ln.shape), _const_spec(w_o.shape)],
        out_specs=row(D_MODEL),
        scratch_shapes=[pltpu.VMEM((blk, Q_WIDTH), BF16)],
        compiler_params=_params(2),
        name="attn_lat",
    )(sink, qp, qr, kr, v_ext, k_ctx, v_ctx_ext, x2, mods, ln, w_o)


def _ffn_kernel(x_ref, mod_ref, ln_ref, w1_ref, w3_ref, w2_ref, out_ref, acc_ref):
    x = x_ref[...]
    h = (x * (1.0 + mod_ref[0, 4:5, :]) + mod_ref[0, 3:4, :]).astype(BF16)
    for c in range(D_FF // FF_CHUNK):
        lo, hi = c * FF_CHUNK, (c + 1) * FF_CHUNK
        a = _dot(h, w1_ref[:, lo:hi])
        b = _dot(h, w3_ref[:, lo:hi])
        y = _dot((_silu(a) * b).astype(BF16), w2_ref[lo:hi, :])
        if c == 0:
            acc_ref[...] = y
        else:
            acc_ref[...] += y
    z = DN_ALPHA * x + mod_ref[0, 5:6, :] * acc_ref[...]
    out_ref[...] = _layer_norm(z, ln_ref[0:1, :], ln_ref[1:2, :])


def _ffn(x2, mods, mod_base, rows_per_mod, ln, w1, w3, w2):
    r = x2.shape[0]
    tm = _row_tile(rows_per_mod, 512)
    per_mod = rows_per_mod // tm
    return pl.pallas_call(
        _ffn_kernel,
        out_shape=jax.ShapeDtypeStruct((r, D_MODEL), F32),
        grid=(r // tm,),
        in_specs=[pl.BlockSpec((tm, D_MODEL), lambda i: (i, 0)),
                  pl.BlockSpec((1, N_MODS, D_MODEL), lambda i: (mod_base + i // per_mod, 0, 0)),
                  _const_spec(ln.shape), _const_spec(w1.shape), _const_spec(w3.shape),
                  _const_spec(w2.shape)],
        out_specs=pl.BlockSpec((tm, D_MODEL), lambda i: (i, 0)),
        scratch_shapes=[pltpu.VMEM((tm, D_MODEL), F32)],
        compiler_params=_params(1),
        name="ffn",
    )(x2, mods, ln, w1, w3, w2)


LANE_BLOCKS = 128 // SSM_GROUP
LANE_TILES = D_MODEL // 128


def _block_transpose(xs):
    xs = list(xs)
    blk = lax.broadcasted_iota(jnp.int32, xs[0].shape, 1) // SSM_GROUP
    k = LANE_BLOCKS // 2
    while k >= 1:
        keep_lo = (blk & k) == 0
        for i in range(LANE_BLOCKS):
            if i & k:
                continue
            lo, hi = xs[i], xs[i + k]
            xs[i] = jnp.where(keep_lo, lo, pltpu.roll(hi, SSM_GROUP * k, 1))
            xs[i + k] = jnp.where(keep_lo, pltpu.roll(lo, 128 - SSM_GROUP * k, 1), hi)
        k //= 2
    return xs


def _ssm_in_kernel(x_ref, mod_ref, w_ref, o_ref, u_scr):
    h = (x_ref[...] * (1.0 + mod_ref[0, 1:2, :]) + mod_ref[0, 0:1, :]).astype(BF16)
    u = _dot(h, w_ref[...])
    for c in range(LANE_TILES):
        u_scr[c] = u[:, c * 128:(c + 1) * 128]
    n_chunks = x_ref.shape[0] // SSM_CHUNK
    for c in range(LANE_TILES):
        for half in range(SSM_CHUNK // LANE_BLOCKS):
            per_token = [u_scr[c, pl.ds(half * LANE_BLOCKS + s, n_chunks, stride=SSM_CHUNK), :]
                         for s in range(LANE_BLOCKS)]
            per_group = _block_transpose(per_token)
            for gl in range(LANE_BLOCKS):
                o_ref[0, c * LANE_BLOCKS + gl, :, half * 128:(half + 1) * 128] = per_group[gl]


def _ssm_in(x2, mods, mod_base, rows_per_mod, w, nb):
    r = x2.shape[0]
    t = r // nb
    tm = _row_tile(math.gcd(rows_per_mod, t), 512)
    per_mod = rows_per_mod // tm
    per_seq = t // tm
    nj = tm // SSM_CHUNK
    return pl.pallas_call(
        _ssm_in_kernel,
        out_shape=jax.ShapeDtypeStruct((nb, N_GROUPS, t // SSM_CHUNK, SSM_FLAT), F32),
        grid=(r // tm,),
        in_specs=[pl.BlockSpec((tm, D_MODEL), lambda i: (i, 0)),
                  pl.BlockSpec((1, N_MODS, D_MODEL), lambda i: (mod_base + i // per_mod, 0, 0)),
                  _const_spec(w.shape)],
        out_specs=pl.BlockSpec((1, N_GROUPS, nj, SSM_FLAT), lambda i: (i // per_seq, 0, i % per_seq, 0)),
        scratch_shapes=[pltpu.VMEM((LANE_TILES, tm, 128), F32)],
        compiler_params=_params(1),
        name="ssm_in",
    )(x2, mods, w)


def _s5_prep_kernel(logdt_ref, lam_row_ref, lam_col_ref, bt_ref, ct_ref,
                    m_ref, s_ref, r_ref, ap_ref, *, n_steps):
    g = pl.program_id(0)
    lc = SSM_CHUNK

    def transition(lam_re, lam_im, shape, axis):
        fwd = lax.broadcasted_iota(jnp.int32, shape, axis) < STATE
        dt = jnp.exp(jnp.where(fwd, jnp.full(shape, logdt_ref[0, g], F32), jnp.full(shape, logdt_ref[1, g], F32)))
        mag = jnp.exp(lam_re * dt)
        return mag * jnp.cos(lam_im * dt), mag * jnp.sin(lam_im * dt), fwd

    def powers(a_re, a_im, n):
        pr, pi = [jnp.ones_like(a_re)], [jnp.zeros_like(a_im)]
        for _ in range(n):
            r, i = pr[-1], pi[-1]
            pr.append(r * a_re - i * a_im)
            pi.append(r * a_im + i * a_re)
        return pr, pi

    row_shape = (SSM_GROUP, 2 * STATE)
    lam_re = jnp.broadcast_to(lam_row_ref[0, 0:1, :], row_shape)
    lam_im = jnp.broadcast_to(lam_row_ref[0, 1:2, :], row_shape)
    a_re, a_im, fwd_lane = transition(lam_re, lam_im, row_shape, 1)
    den = lam_re * lam_re + lam_im * lam_im
    nr, ni = a_re - 1.0, a_im
    coef_re = (nr * lam_re + ni * lam_im) / den
    coef_im = (ni * lam_re - nr * lam_im) / den
    bt_re, bt_im = bt_ref[0, 0], bt_ref[0, 1]
    bb_re = coef_re * bt_re - coef_im * bt_im
    bb_im = coef_re * bt_im + coef_im * bt_re
    pr, pi = powers(a_re, a_im, lc)
    ab_re = [pr[k] * bb_re - pi[k] * bb_im for k in range(lc)]
    ab_im = [pr[k] * bb_im + pi[k] * bb_re for k in range(lc)]
    s_re = jnp.concatenate([jnp.where(fwd_lane, ab_re[lc - 1 - s], ab_re[s]) for s in range(lc)], axis=0)
    s_im = jnp.concatenate([jnp.where(fwd_lane, ab_im[lc - 1 - s], ab_im[s]) for s in range(lc)], axis=0)
    s_ref[0] = jnp.concatenate([s_re, s_im], axis=1).astype(BF16)

    ct_re, ct_im = ct_ref[0, 0], ct_ref[0, 1]
    fwd_row = lax.broadcasted_iota(jnp.int32, ct_re.shape, 0) < STATE
    dot_hi = lambda a, b: jnp.dot(a, b, precision=HIGHEST, preferred_element_type=F32)
    zero = jnp.zeros_like(ct_re)
    kf = dot_hi(s_re, jnp.where(fwd_row, ct_re, zero)) - dot_hi(s_im, jnp.where(fwd_row, ct_im, zero))
    kb = dot_hi(s_re, jnp.where(fwd_row, zero, ct_re)) - dot_hi(s_im, jnp.where(fwd_row, zero, ct_im))
    t_blk = lax.broadcasted_iota(jnp.int32, kf.shape, 1) // SSM_GROUP
    m_op = jnp.zeros_like(kf)
    for t in range(lc):
        up = (lc - 1 - t) * SSM_GROUP
        f_sh = kf if up == 0 else jnp.concatenate([kf[up:], jnp.zeros((up, SSM_FLAT), F32)], axis=0)
        down = t * SSM_GROUP
        b_sh = kb if down == 0 else jnp.concatenate([jnp.zeros((down, SSM_FLAT), F32), kb[:SSM_FLAT - down]], axis=0)
        m_op = jnp.where(t_blk == t, f_sh + b_sh, m_op)
    m_ref[0] = m_op.astype(BF16)

    sq_re, sq_im = [pr[lc]], [pi[lc]]
    for _ in range(n_steps - 1):
        r, i = sq_re[-1], sq_im[-1]
        sq_re.append(r * r - i * i)
        sq_im.append(2.0 * r * i)
    for i in range(n_steps):
        ap_ref[0, i:i + 1, :] = sq_re[i][0:1, :]
        ap_ref[0, n_steps + i:n_steps + i + 1, :] = sq_im[i][0:1, :]

    lam_re_c = jnp.broadcast_to(lam_col_ref[0, :, 0:1], ct_re.shape)
    lam_im_c = jnp.broadcast_to(lam_col_ref[0, :, 1:2], ct_re.shape)
    ac_re, ac_im, fwd_sub = transition(lam_re_c, lam_im_c, ct_re.shape, 0)
    pcr, pci = powers(ac_re, ac_im, lc)
    pw_re = jnp.zeros_like(ct_re)
    pw_im = jnp.zeros_like(ct_re)
    t_blk_r = lax.broadcasted_iota(jnp.int32, ct_re.shape, 1) // SSM_GROUP
    for t in range(lc):
        sel_re = jnp.where(fwd_sub, pcr[t + 1], pcr[lc - t])
        sel_im = jnp.where(fwd_sub, pci[t + 1], pci[lc - t])
        pw_re = jnp.where(t_blk_r == t, sel_re, pw_re)
        pw_im = jnp.where(t_blk_r == t, sel_im, pw_im)
    w_re = ct_re * pw_re - ct_im * pw_im
    w_im = ct_re * pw_im + ct_im * pw_re
    r_ref[0] = jnp.concatenate([w_re, -w_im], axis=0).astype(BF16)


def _s5_operators(lam_re, lam_im, log_dt, b_re, b_im, c_re, c_im, d_skip, n_steps):
    both = lambda x: jnp.concatenate([x[0], x[1]], -1)
    lam_row = jnp.stack([both(lam_re), both(lam_im)], 1)
    lam_col = lam_row.transpose(0, 2, 1)
    bt = jnp.stack([both(b_re.swapaxes(-1, -2)), both(b_im.swapaxes(-1, -2))], 1)
    ct_cols = lambda x: jnp.tile(jnp.concatenate([x[0], x[1]], -1).swapaxes(-1, -2), (1, 1, SSM_CHUNK))
    ct = jnp.stack([ct_cols(c_re), ct_cols(c_im)], 1)
    grp = lambda a: pl.BlockSpec((1,) + a.shape[1:], lambda g: (g,) + (0,) * (a.ndim - 1))
    out_sq = jax.ShapeDtypeStruct((N_GROUPS, SSM_FLAT, SSM_FLAT), BF16)
    out_ap = jax.ShapeDtypeStruct((N_GROUPS, 2 * n_steps, 2 * STATE), F32)
    m_op, s_op, r_op, a_pow = pl.pallas_call(
        functools.partial(_s5_prep_kernel, n_steps=n_steps),
        out_shape=(out_sq, out_sq, out_sq, out_ap),
        grid=(N_GROUPS,),
        in_specs=[pl.BlockSpec(memory_space=pltpu.SMEM), grp(lam_row), grp(lam_col), grp(bt), grp(ct)],
        out_specs=(grp(out_sq), grp(out_sq), grp(out_sq), grp(out_ap)),
        compiler_params=_params(1),
        name="s5_prep",
    )(log_dt, lam_row, lam_col, bt, ct)
    d_row = jnp.tile(d_skip.reshape(N_GROUPS, 1, SSM_GROUP), (1, 1, SSM_CHUNK))
    return m_op, s_op, r_op, a_pow, d_row


def _ssm_core_kernel(*refs, seq_chunks, n_steps, groups, has_h0, emit_state):
    refs = list(refs)
    u_ref, m_ref, s_ref, r_ref, ap_ref, d_ref = refs[:6]
    refs = refs[6:]
    h0_ref = refs.pop(0) if has_h0 else None
    y_ref = refs.pop(0)
    e_ref = refs.pop(0) if emit_state else None
    rows = u_ref.shape[2]
    half = 2 * STATE
    pos = lax.broadcasted_iota(jnp.int32, (rows, half), 0) % seq_chunks
    is_fwd = lax.broadcasted_iota(jnp.int32, (rows, half), 1) < STATE
    n_upstream = jnp.where(is_fwd, pos, seq_chunks - 1 - pos)

    def neighbour(x, d):
        return jnp.where(is_fwd, pltpu.roll(x, d, 0), pltpu.roll(x, rows - d, 0))

    def upstream_ok(d):
        return n_upstream >= d

    for g in range(groups):
        u = u_ref[0, g]
        ub = u.astype(BF16)
        y = _dot(ub, m_ref[g]) + d_ref[g] * u
        h = _dot(ub, s_ref[g])
        hr, hi = h[:, :half], h[:, half:]
        a_re, a_im = ap_ref[g, 0:1, :], ap_ref[g, n_steps:n_steps + 1, :]
        if has_h0:
            h0r, h0i = h0_ref[0, g, 0:1, :half], h0_ref[0, g, 0:1, half:]
            first = n_upstream == 0
            hr = hr + jnp.where(first, a_re * h0r - a_im * h0i, 0.0)
            hi = hi + jnp.where(first, a_re * h0i + a_im * h0r, 0.0)
        for i in range(n_steps):
            d = 1 << i
            ok = upstream_ok(d)
            sr = jnp.where(ok, neighbour(hr, d), 0.0)
            si = jnp.where(ok, neighbour(hi, d), 0.0)
            p_re, p_im = ap_ref[g, i:i + 1, :], ap_ref[g, n_steps + i:n_steps + i + 1, :]
            hr, hi = hr + p_re * sr - p_im * si, hi + p_re * si + p_im * sr
        if emit_state:
            e_ref[0, g] = jnp.concatenate([hr, hi], axis=1)
        ok = upstream_ok(1)
        if has_h0:
            in_r = jnp.where(ok, neighbour(hr, 1), h0r)
            in_i = jnp.where(ok, neighbour(hi, 1), h0i)
        else:
            in_r = jnp.where(ok, neighbour(hr, 1), 0.0)
            in_i = jnp.where(ok, neighbour(hi, 1), 0.0)
        y_ref[0, g] = y + _dot(jnp.concatenate([in_r, in_i], axis=1).astype(BF16), r_ref[g])


def _ssm_core(u_flat, ops, h0, seq_chunks, emit_state):
    nb, _, rows, _ = u_flat.shape
    m_op, s_op, r_op, a_pow, d_row = ops
    n_steps = a_pow.shape[1] // 2
    assert (1 << n_steps) == seq_chunks
    gt = 8
    blk = pl.BlockSpec((1, gt, rows, SSM_FLAT), lambda b, g: (b, g, 0, 0))
    grp = lambda a: pl.BlockSpec((gt,) + a.shape[1:], lambda b, g: (g, 0, 0))
    in_specs = [blk, grp(m_op), grp(s_op), grp(r_op), grp(a_pow), grp(d_row)]
    args = [u_flat, m_op, s_op, r_op, a_pow, d_row]
    if h0 is not None:
        in_specs.append(pl.BlockSpec((1, gt, 1, 4 * STATE), lambda b, g: (b, g, 0, 0)))
        args.append(h0)
    out_shape = [jax.ShapeDtypeStruct(u_flat.shape, F32)]
    out_specs = [blk]
    if emit_state:
        out_shape.append(jax.ShapeDtypeStruct(u_flat.shape, F32))
        out_specs.append(blk)
    kern = functools.partial(_ssm_core_kernel, seq_chunks=seq_chunks, n_steps=n_steps, groups=gt,
                             has_h0=h0 is not None, emit_state=emit_state)
    return pl.pallas_call(
        kern, out_shape=tuple(out_shape), grid=(nb, N_GROUPS // gt),
        in_specs=in_specs, out_specs=tuple(out_specs),
        compiler_params=_params(2), name="ssm_core",
    )(*args)


def _ssm_out_kernel(y_ref, x_ref, mod_ref, ln_ref, wg_ref, wo_ref, out_ref, y_scr):
    n_chunks = x_ref.shape[0] // SSM_CHUNK
    for c in range(LANE_TILES):
        for half in range(SSM_CHUNK // LANE_BLOCKS):
            per_group = [y_ref[0, c * LANE_BLOCKS + gl, :, half * 128:(half + 1) * 128]
                         for gl in range(LANE_BLOCKS)]
            per_token = _block_transpose(per_group)
            for s in range(LANE_BLOCKS):
                y_scr[c, pl.ds(half * LANE_BLOCKS + s, n_chunks, stride=SSM_CHUNK), :] = per_token[s]
    y = jnp.concatenate([y_scr[c] for c in range(LANE_TILES)], axis=1)
    vg = _dot(_gelu_tanh(y).astype(BF16), wg_ref[...])
    glu = (vg[:, :D_MODEL] * jax.nn.sigmoid(vg[:, D_MODEL:])).astype(BF16)
    z = DN_ALPHA * x_ref[...] + mod_ref[0, 2:3, :] * _dot(glu, wo_ref[...])
    out_ref[...] = _layer_norm(z, ln_ref[0:1, :], ln_ref[1:2, :])


def _ssm_out(y_flat, x2, mods, mod_base, rows_per_mod, ln, w_glu, w_out):
    r = x2.shape[0]
    nb = y_flat.shape[0]
    t = r // nb
    tm = _row_tile(math.gcd(rows_per_mod, t), 512)
    per_mod = rows_per_mod // tm
    per_seq = t // tm
    nj = tm // SSM_CHUNK
    row = pl.BlockSpec((tm, D_MODEL), lambda i: (i, 0))
    return pl.pallas_call(
        _ssm_out_kernel,
        out_shape=jax.ShapeDtypeStruct((r, D_MODEL), F32),
        grid=(r // tm,),
        in_specs=[pl.BlockSpec((1, N_GROUPS, nj, SSM_FLAT), lambda i: (i // per_seq, 0, i % per_seq, 0)),
                  row,
                  pl.BlockSpec((1, N_MODS, D_MODEL), lambda i: (mod_base + i // per_mod, 0, 0)),
                  _const_spec(ln.shape), _const_spec(w_glu.shape), _const_spec(w_out.shape)],
        out_specs=row,
        scratch_shapes=[pltpu.VMEM((LANE_TILES, tm, 128), F32)],
        compiler_params=_params(1),
        name="ssm_out",
    )(y_flat, x2, mods, ln, w_glu, w_out)


def _ssm_mixer(x2, nb, seq_chunks, mods, mod_base, rows_per_mod, ln, w_in, ops, w_glu, w_out, h0, emit_state):
    u_flat = _ssm_in(x2, mods, mod_base, rows_per_mod, w_in, nb)
    outs = _ssm_core(u_flat, ops, h0, seq_chunks, emit_state)
    x_new = _ssm_out(outs[0], x2, mods, mod_base, rows_per_mod, ln, w_glu, w_out)
    return x_new, (outs[1] if emit_state else None)


def _rope_tables(t):
    n_rows = t // GRID_W
    row = jnp.repeat(jnp.arange(n_rows, dtype=F32), GRID_W)
    col = (jnp.arange(t) % GRID_W).astype(F32)
    nfreq = HEAD_DIM // 4
    inv = jnp.power(ROPE_BASE, -jnp.arange(nfreq, dtype=F32) / nfreq)
    ang_r, ang_c = row[:, None] * inv, col[:, None] * inv
    cos = jnp.concatenate([jnp.cos(ang_r)] * 2 + [jnp.cos(ang_c)] * 2, -1)
    sin = jnp.concatenate([-jnp.sin(ang_r), jnp.sin(ang_r), -jnp.sin(ang_c), jnp.sin(ang_c)], -1)
    return jnp.tile(cos, (1, 2)), jnp.tile(sin, (1, 2))


def kernel(x_prompt, x_sample, cache_k, cache_v, state_ssm_re, state_ssm_im, c, c_ctx, w_ada, b_ada, ln_g, ln_b, w_qkv, w_o, attn_sink, ssm_w_in, ssm_lam_re, ssm_lam_im, ssm_log_dt, ssm_b_re, ssm_b_im, ssm_c_re, ssm_c_im, ssm_d, ssm_w_glu, ssm_w_out, ffn_w1, ffn_w3, ffn_w2):
    nc, tc, _ = x_prompt.shape
    nl, tl, _ = x_sample.shape
    n_attn = w_qkv.shape[0]
    n_ssm = ssm_w_in.shape[0]

    n_mod_rows = -(-(1 + nl) // 8) * 8
    cvec = jnp.zeros((n_mod_rows, D_MODEL), F32).at[0].set(c_ctx).at[1:1 + nl].set(c)
    mods_all = _ada_mods(cvec, w_ada, b_ada)
    ln_all = jnp.stack([ln_g, ln_b], axis=2)

    bf = lambda w: w.astype(BF16)
    w_v = w_qkv[:, :, Q_WIDTH + KV_WIDTH:].reshape(n_attn, D_MODEL, N_KV_HEADS, HEAD_DIM)
    w_v_ext = jnp.pad(w_v, ((0, 0), (0, 0), (0, 0), (0, 128 - HEAD_DIM))).reshape(n_attn, D_MODEL, V_EXT)
    w_qkv_ctx = bf(jnp.concatenate([w_qkv, w_v_ext], axis=-1))
    w_qkv_lat = bf(jnp.concatenate([w_qkv[:, :, :Q_WIDTH + KV_WIDTH], w_v_ext], axis=-1))
    w_o_b = bf(w_o)
    w_in_b, w_glu_b, w_out_b = bf(ssm_w_in), bf(ssm_w_glu), bf(ssm_w_out)
    w1_b, w3_b, w2_b = bf(ffn_w1), bf(ffn_w3), bf(ffn_w2)
    cos, sin = _rope_tables(tl)
    kx = bf(cache_k).reshape(nl, n_attn, -1, KV_WIDTH)
    vx = jnp.concatenate([bf(cache_v), jnp.ones(cache_v.shape[:-1] + (128 - HEAD_DIM,), BF16)], -1)
    vx = vx.reshape(nl, n_attn, -1, V_EXT)

    steps_c = (tc // SSM_CHUNK).bit_length() - 1
    steps_l = (tl // SSM_CHUNK).bit_length() - 1
    ssm_ops = []
    for s in range(n_ssm):
        prm = (ssm_lam_re[s], ssm_lam_im[s], ssm_log_dt[s], ssm_b_re[s], ssm_b_im[s],
               ssm_c_re[s], ssm_c_im[s], ssm_d[s])
        ssm_ops.append(_s5_operators(*prm, max(steps_c, steps_l)))

    def ops_for(s, n_steps):
        m_op, s_op, r_op, a_pow, d_row = ssm_ops[s]
        total = a_pow.shape[1] // 2
        a_sel = jnp.concatenate([a_pow[:, :n_steps], a_pow[:, total:total + n_steps]], 1)
        return m_op, s_op, r_op, a_sel, d_row

    xc = x_prompt.reshape(nc * tc, D_MODEL)
    xl = x_sample.reshape(nl * tl, D_MODEL)
    new_k, new_v, new_sr, new_si = [], [], [], []
    for l in range(DEPTH):
        mods = mods_all[l]
        ln1, ln2 = ln_all[l, 0], ln_all[l, 1]
        if l % 2 == 0:
            a = l // 2
            q, kb, vb, k32, v32 = _qkv_ctx(xc, mods, w_qkv_ctx[a])
            new_k.append(k32.reshape(nc, tc, N_KV_HEADS, HEAD_DIM))
            new_v.append(v32.reshape(nc, tc, N_KV_HEADS, HEAD_DIM))
            xc = _attn_ctx(attn_sink[a], q, kb, vb, xc, mods, ln1, w_o_b[a], tc)
            qp, qr, kr, v = _qkv_lat(xl, mods, w_qkv_lat[a], cos, sin, tl)
            xl = _attn_lat(attn_sink[a], qp, qr, kr, v, kx[:, a], vx[:, a], xl, mods, ln1, w_o_b[a], nl, tl)
        else:
            s = l // 2
            xc, e_c = _ssm_mixer(xc, 1, tc // SSM_CHUNK, mods, 0, nc * tc, ln1, w_in_b[s],
                                 ops_for(s, steps_c), w_glu_b[s], w_out_b[s], None, True)
            e5 = e_c.reshape(N_GROUPS, nc, tc // SSM_CHUNK, 4, STATE)
            fin = jnp.stack([e5[:, :, -1, 0], e5[:, :, 0, 1], e5[:, :, -1, 2], e5[:, :, 0, 3]], 0)
            fin = fin.transpose(0, 2, 1, 3)
            new_sr.append(jnp.stack([fin[0], fin[1]], 1))
            new_si.append(jnp.stack([fin[2], fin[3]], 1))
            h0 = jnp.concatenate([state_ssm_re[:, s, 0], state_ssm_re[:, s, 1],
                                  state_ssm_im[:, s, 0], state_ssm_im[:, s, 1]], -1)
            xl, _ = _ssm_mixer(xl, nl, tl // SSM_CHUNK, mods, 1, tl, ln1, w_in_b[s],
                               ops_for(s, steps_l), w_glu_b[s], w_out_b[s], h0[:, :, None, :], False)
        xc = _ffn(xc, mods, 0, nc * tc, ln2, w1_b[l], w3_b[l], w2_b[l])
        xl = _ffn(xl, mods, 1, tl, ln2, w1_b[l], w3_b[l], w2_b[l])

    return (xc.reshape(nc, tc, D_MODEL), xl.reshape(nl, tl, D_MODEL),
            jnp.stack(new_k, axis=1), jnp.stack(new_v, axis=1),
            jnp.stack(new_sr, axis=1), jnp.stack(new_si, axis=1))
```

```python
import functools
import math

import jax
import jax.numpy as jnp
from jax import lax
from jax.experimental import pallas as pl
from jax.experimental.pallas import tpu as pltpu

F32 = jnp.float32
BF16 = jnp.bfloat16

D_MODEL = 1024
DEPTH = 4
N_HEADS = 16
N_KV_HEADS = 4
HEAD_DIM = 64
HEADS_PER_KV = N_HEADS // N_KV_HEADS
Q_WIDTH = N_HEADS * HEAD_DIM
KV_WIDTH = N_KV_HEADS * HEAD_DIM
GRID_W = 64
ATT_BLOCK = 128
ROPE_BASE = 10000.0
SSM_GROUP = 16
N_GROUPS = D_MODEL // SSM_GROUP
STATE = 64
SSM_CHUNK = 16
SSM_FLAT = SSM_CHUNK * SSM_GROUP
D_FF = -(-8 * D_MODEL // (3 * 256)) * 256
FF_CHUNK = 256
DN_ALPHA = (2 * DEPTH) ** 0.25
LN_EPS = 1e-5
NEG_INF = -1e30
N_MODS = 6
VMEM_LIMIT = 56 * 1024 * 1024
HIGHEST = lax.Precision.HIGHEST


def _params(n_axes):
    return pltpu.CompilerParams(dimension_semantics=("arbitrary",) * n_axes,
                                vmem_limit_bytes=VMEM_LIMIT)


def _const_spec(shape):
    return pl.BlockSpec(shape, lambda *_: (0,) * len(shape))


def _layer_norm(y, g, b):
    mu = jnp.mean(y, -1, keepdims=True)
    d = y - mu
    var = jnp.mean(d * d, -1, keepdims=True)
    return d * lax.rsqrt(var + LN_EPS) * g + b


def _silu(x):
    return x * jax.nn.sigmoid(x)


def _gelu_tanh(x):
    return 0.5 * x * (1.0 + jnp.tanh(math.sqrt(2.0 / math.pi) * (x + 0.044715 * (x * x * x))))


def _dot(a, b):
    return jnp.dot(a, b, preferred_element_type=F32)


def _dot_nt(a, b):
    return lax.dot_general(a, b, (((1,), (1,)), ((), ())), preferred_element_type=F32)


def _row_tile(rows_per_mod, want):
    tm = min(want, rows_per_mod)
    assert rows_per_mod % tm == 0
    return tm


def _mods_kernel(c_ref, w_ref, b_ref, o_ref):
    a = _silu(c_ref[...]).astype(BF16)
    o_ref[0] = _dot(a, w_ref[0].astype(BF16)) + b_ref[0]


def _ada_mods(cvec, w_ada, b_ada):
    r = cvec.shape[0]
    tn = 1536
    out = pl.pallas_call(
        _mods_kernel,
        out_shape=jax.ShapeDtypeStruct((DEPTH, r, N_MODS * D_MODEL), F32),
        grid=(DEPTH, N_MODS * D_MODEL // tn),
        in_specs=[pl.BlockSpec((r, D_MODEL), lambda l, j: (0, 0)),
                  pl.BlockSpec((1, D_MODEL, tn), lambda l, j: (l, 0, j)),
                  pl.BlockSpec((1, 1, tn), lambda l, j: (l, 0, j))],
        out_specs=pl.BlockSpec((1, r, tn), lambda l, j: (l, 0, j)),
        compiler_params=_params(2),
        name="ada_mods",
    )(cvec, w_ada, b_ada.reshape(DEPTH, 1, N_MODS * D_MODEL))
    return out.reshape(DEPTH, r, N_MODS, D_MODEL)


Q_SCALE = HEAD_DIM ** -0.5 * math.log2(math.e)
V_EXT = N_KV_HEADS * 128


def _rope(x, cos, sin, first_half):
    outs = []
    for c in range(x.shape[1] // 128):
        xc = x[:, c * 128:(c + 1) * 128]
        partner = jnp.where(first_half, pltpu.roll(xc, 128 - 16, 1), pltpu.roll(xc, 16, 1))
        outs.append(xc * cos + partner * sin)
    return jnp.concatenate(outs, axis=1)


def _ones_lanes(width):
    return jnp.where((lax.broadcasted_iota(jnp.int32, (1, width), 1) % 128) >= HEAD_DIM, 1.0, 0.0)


def _qkv_ctx_kernel(x_ref, mod_ref, w_ref, q_ref, kb_ref, vb_ref, k_ref, v_ref):
    h = (x_ref[...] * (1.0 + mod_ref[0, 1:2, :]) + mod_ref[0, 0:1, :]).astype(BF16)
    qkv = _dot(h, w_ref[...])
    q_ref[...] = (qkv[:, :Q_WIDTH] * Q_SCALE).astype(BF16)
    k = qkv[:, Q_WIDTH:Q_WIDTH + KV_WIDTH]
    k_ref[...] = k
    v_ref[...] = qkv[:, Q_WIDTH + KV_WIDTH:Q_WIDTH + 2 * KV_WIDTH]
    kb_ref[...] = k.astype(BF16)
    vb_ref[...] = (qkv[:, Q_WIDTH + 2 * KV_WIDTH:] + _ones_lanes(V_EXT)).astype(BF16)


def _qkv_lat_kernel(x_ref, mod_ref, w_ref, cos_ref, sin_ref, qp_ref, qr_ref, kr_ref, v_ref):
    h = (x_ref[...] * (1.0 + mod_ref[0, 1:2, :]) + mod_ref[0, 0:1, :]).astype(BF16)
    qkv = _dot(h, w_ref[...])
    cos = cos_ref[...]
    sin = sin_ref[...]
    first_half = (lax.broadcasted_iota(jnp.int32, cos.shape, 1) & 16) == 0
    q = qkv[:, :Q_WIDTH] * Q_SCALE
    qp_ref[...] = q.astype(BF16)
    qr_ref[...] = _rope(q, cos, sin, first_half).astype(BF16)
    kr_ref[...] = _rope(qkv[:, Q_WIDTH:Q_WIDTH + KV_WIDTH], cos, sin, first_half).astype(BF16)
    v_ref[...] = (qkv[:, Q_WIDTH + KV_WIDTH:] + _ones_lanes(V_EXT)).astype(BF16)


def _qkv_ctx(x2, mods, w_ext):
    r = x2.shape[0]
    tm = _row_tile(r, 512)
    row = lambda w: pl.BlockSpec((tm, w), lambda i: (i, 0))
    return pl.pallas_call(
        _qkv_ctx_kernel,
        out_shape=(jax.ShapeDtypeStruct((r, Q_WIDTH), BF16),
                   jax.ShapeDtypeStruct((r, KV_WIDTH), BF16),
                   jax.ShapeDtypeStruct((r, V_EXT), BF16),
                   jax.ShapeDtypeStruct((r, KV_WIDTH), F32),
                   jax.ShapeDtypeStruct((r, KV_WIDTH), F32)),
        grid=(r // tm,),
        in_specs=[row(D_MODEL),
                  pl.BlockSpec((1, N_MODS, D_MODEL), lambda i: (0, 0, 0)),
                  _const_spec(w_ext.shape)],
        out_specs=(row(Q_WIDTH), row(KV_WIDTH), row(V_EXT), row(KV_WIDTH), row(KV_WIDTH)),
        compiler_params=_params(1),
        name="qkv_ctx",
    )(x2, mods, w_ext)


def _qkv_lat(x2, mods, w_ext, cos, sin, t):
    r = x2.shape[0]
    tm = _row_tile(t, 512)
    per_seq = t // tm
    row = lambda w: pl.BlockSpec((tm, w), lambda i: (i, 0))
    tab = pl.BlockSpec((tm, 128), lambda i: (i % per_seq, 0))
    return pl.pallas_call(
        _qkv_lat_kernel,
        out_shape=(jax.ShapeDtypeStruct((r, Q_WIDTH), BF16),
                   jax.ShapeDtypeStruct((r, Q_WIDTH), BF16),
                   jax.ShapeDtypeStruct((r, KV_WIDTH), BF16),
                   jax.ShapeDtypeStruct((r, V_EXT), BF16)),
        grid=(r // tm,),
        in_specs=[row(D_MODEL),
                  pl.BlockSpec((1, N_MODS, D_MODEL), lambda i: (1 + i // per_seq, 0, 0)),
                  _const_spec(w_ext.shape), tab, tab],
        out_specs=(row(Q_WIDTH), row(Q_WIDTH), row(KV_WIDTH), row(V_EXT)),
        compiler_params=_params(1),
        name="qkv_lat",
    )(x2, mods, w_ext, cos, sin)


def _stack_heads(ref, kv):
    base = kv * HEADS_PER_KV * HEAD_DIM
    return jnp.concatenate(
        [ref[:, base + g * HEAD_DIM: base + (g + 1) * HEAD_DIM] for g in range(HEADS_PER_KV)], axis=0)


def _sink_column(sink_ref, kv, rows):
    return jnp.concatenate(
        [jnp.full((rows, 1), sink_ref[kv * HEADS_PER_KV + g] * math.log2(math.e), F32)
         for g in range(HEADS_PER_KV)], axis=0)


def _normalise_and_store(o_scr, o_ext, sink_weight, kv, rows):
    den = pltpu.roll(o_ext, HEAD_DIM, 1) + sink_weight
    o = o_ext / den
    base = kv * HEADS_PER_KV * HEAD_DIM
    for g in range(HEADS_PER_KV):
        o_scr[:, base + g * HEAD_DIM: base + (g + 1) * HEAD_DIM] = (
            o[g * rows:(g + 1) * rows, :HEAD_DIM].astype(BF16))


def _attn_epilogue(o_scr, x_ref, mod_ref, ln_ref, wo_ref, out_ref):
    y = _dot(o_scr[...], wo_ref[...])
    z = DN_ALPHA * x_ref[...] + mod_ref[0, 2:3, :] * y
    out_ref[...] = _layer_norm(z, ln_ref[0:1, :], ln_ref[1:2, :])


def _attn_ctx_kernel(sink_ref, q_ref, k_ref, v_ref, x_ref, mod_ref, ln_ref, wo_ref, out_ref, o_scr):
    rows = q_ref.shape[0]
    for kv in range(N_KV_HEADS):
        q4 = _stack_heads(q_ref, kv)
        sink = _sink_column(sink_ref, kv, rows)
        s = _dot_nt(q4, k_ref[:, kv * HEAD_DIM:(kv + 1) * HEAD_DIM])
        m = jnp.maximum(jnp.max(s, -1, keepdims=True), sink)
        p = jnp.exp2(s - m).astype(BF16)
        o_ext = _dot(p, v_ref[:, kv * 128:(kv + 1) * 128])
        _normalise_and_store(o_scr, o_ext, jnp.exp2(sink - m), kv, rows)
    _attn_epilogue(o_scr, x_ref, mod_ref, ln_ref, wo_ref, out_ref)


def _attn_ctx(sink, q, k, v_ext, x2, mods, ln, w_o, seq):
    r = x2.shape[0]
    row = lambda w: pl.BlockSpec((seq, w), lambda i: (i, 0))
    return pl.pallas_call(
        _attn_ctx_kernel,
        out_shape=jax.ShapeDtypeStruct((r, D_MODEL), F32),
        grid=(r // seq,),
        in_specs=[pl.BlockSpec(memory_space=pltpu.SMEM),
                  row(Q_WIDTH), row(KV_WIDTH), row(V_EXT), row(D_MODEL),
                  pl.BlockSpec((1, N_MODS, D_MODEL), lambda i: (0, 0, 0)),
                  _const_spec(ln.shape), _const_spec(w_o.shape)],
        out_specs=row(D_MODEL),
        scratch_shapes=[pltpu.VMEM((seq, Q_WIDTH), BF16)],
        compiler_params=_params(1),
        name="attn_ctx",
    )(sink, q, k, v_ext, x2, mods, ln, w_o)


LOCAL_KEYS = 3 * ATT_BLOCK


def _attn_lat_kernel(sink_ref, qp_ref, qr_ref, k_ref, v_ref, kx_ref, vx_ref, x_ref, mod_ref, ln_ref,
                     wo_ref, out_ref, o_scr, s_scr, m_scr, bias_scr):
    j = pl.program_id(1)
    rows = ATT_BLOCK
    seq = k_ref.shape[0]
    n_ctx = kx_ref.shape[1]
    n_tiles = (n_ctx + LOCAL_KEYS) // 128
    start = pl.multiple_of(jnp.clip((j - 1) * rows, 0, seq - LOCAL_KEYS), rows)
    r_idx = lax.broadcasted_iota(jnp.int32, (rows, LOCAL_KEYS), 0)
    c_idx = lax.broadcasted_iota(jnp.int32, (rows, LOCAL_KEYS), 1)
    bias_scr[...] = jnp.where(jnp.abs(r_idx - c_idx + (j * rows - start)) <= ATT_BLOCK, 0.0, NEG_INF)

    def scores(kv, slot):
        lo, hi = kv * HEAD_DIM, (kv + 1) * HEAD_DIM
        kx = kx_ref[0, :, lo:hi]
        kl = k_ref[pl.ds(start, LOCAL_KEYS), lo:hi]
        for g in range(HEADS_PER_KV):
            h = kv * HEADS_PER_KV + g
            r0, r1 = g * rows, (g + 1) * rows
            s_x = _dot_nt(qp_ref[:, h * HEAD_DIM:(h + 1) * HEAD_DIM], kx)
            s_l = _dot_nt(qr_ref[:, h * HEAD_DIM:(h + 1) * HEAD_DIM], kl) + bias_scr[...]
            m = jnp.maximum(jnp.maximum(jnp.max(s_x, -1, keepdims=True), jnp.max(s_l, -1, keepdims=True)),
                            sink_ref[h] * math.log2(math.e))
            s_scr[slot, r0:r1, :n_ctx] = s_x
            s_scr[slot, r0:r1, n_ctx:] = s_l
            m_scr[slot, r0:r1, :] = jnp.broadcast_to(m, (rows, 128))

    def values(kv, slot):
        vx = vx_ref[0, :, kv * 128:(kv + 1) * 128]
        vl = v_ref[pl.ds(start, LOCAL_KEYS), kv * 128:(kv + 1) * 128]
        for g in range(HEADS_PER_KV):
            h = kv * HEADS_PER_KV + g
            r0, r1 = g * rows, (g + 1) * rows
            m = m_scr[slot, r0:r1, :]
            p = [jnp.exp2(s_scr[slot, r0:r1, c * 128:(c + 1) * 128] - m).astype(BF16) for c in range(n_tiles)]
            o_ext = (_dot(jnp.concatenate(p[:n_ctx // 128], axis=1), vx)
                     + _dot(jnp.concatenate(p[n_ctx // 128:], axis=1), vl))
            den = pltpu.roll(o_ext, HEAD_DIM, 1) + jnp.exp2(sink_ref[h] * math.log2(math.e) - m)
            o_scr[:, h * HEAD_DIM:(h + 1) * HEAD_DIM] = (o_ext / den)[:, :HEAD_DIM].astype(BF16)

    scores(0, 0)
    for kv in range(N_KV_HEADS):
        if kv + 1 < N_KV_HEADS:
            scores(kv + 1, (kv + 1) % 2)
        values(kv, kv % 2)
    _attn_epilogue(o_scr, x_ref, mod_ref, ln_ref, wo_ref, out_ref)


def _attn_lat(sink, qp, qr, kr, v_ext, k_ctx, v_ctx_ext, x2, mods, ln, w_o, n, t):
    assert t >= LOCAL_KEYS
    nblk = t // ATT_BLOCK
    blk = ATT_BLOCK
    row = lambda w: pl.BlockSpec((blk, w), lambda b, j: (b * nblk + j, 0))
    seq = lambda w: pl.BlockSpec((t, w), lambda b, j: (b, 0))
    ctx = lambda a: pl.BlockSpec((1,) + a.shape[1:], lambda b, j: (b, 0, 0))
    return pl.pallas_call(
        _attn_lat_kernel,
        out_shape=jax.ShapeDtypeStruct((n * t, D_MODEL), F32),
        grid=(n, nblk),
        in_specs=[pl.BlockSpec(memory_space=pltpu.SMEM),
                  row(Q_WIDTH), row(Q_WIDTH), seq(KV_WIDTH), seq(V_EXT),
                  ctx(k_ctx), ctx(v_ctx_ext), row(D_MODEL),
                  pl.BlockSpec((1, N_MODS, D_MODEL), lambda b, j: (1 + b, 0, 0)),
                  _const_spec(ln.shape), _const_spec(w_o.shape)],
        out_specs=row(D_MODEL),
        scratch_shapes=[pltpu.VMEM((blk, Q_WIDTH), BF16),
                        pltpu.VMEM((2, HEADS_PER_KV * blk, k_ctx.shape[1] + LOCAL_KEYS), F32),
                        pltpu.VMEM((2, HEADS_PER_KV * blk, 128), F32),
                        pltpu.VMEM((blk, LOCAL_KEYS), F32)],
        compiler_params=_params(2),
        name="attn_lat",
    )(sink, qp, qr, kr, v_ext, k_ctx, v_ctx_ext, x2, mods, ln, w_o)


def _ffn_kernel(x_ref, mod_ref, ln_ref, w1_ref, w3_ref, w2_ref, out_ref, acc_ref):
    x = x_ref[...]
    h = (x * (1.0 + mod_ref[0, 4:5, :]) + mod_ref[0, 3:4, :]).astype(BF16)
    for c in range(D_FF // FF_CHUNK):
        lo, hi = c * FF_CHUNK, (c + 1) * FF_CHUNK
        a = _dot(h, w1_ref[:, lo:hi])
        b = _dot(h, w3_ref[:, lo:hi])
        y = _dot((_silu(a) * b).astype(BF16), w2_ref[lo:hi, :])
        if c == 0:
            acc_ref[...] = y
        else:
            acc_ref[...] += y
    z = DN_ALPHA * x + mod_ref[0, 5:6, :] * acc_ref[...]
    out_ref[...] = _layer_norm(z, ln_ref[0:1, :], ln_ref[1:2, :])


def _ffn(x2, mods, mod_base, rows_per_mod, ln, w1, w3, w2):
    r = x2.shape[0]
    tm = _row_tile(rows_per_mod, 512)
    per_mod = rows_per_mod // tm
    return pl.pallas_call(
        _ffn_kernel,
        out_shape=jax.ShapeDtypeStruct((r, D_MODEL), F32),
        grid=(r // tm,),
        in_specs=[pl.BlockSpec((tm, D_MODEL), lambda i: (i, 0)),
                  pl.BlockSpec((1, N_MODS, D_MODEL), lambda i: (mod_base + i // per_mod, 0, 0)),
                  _const_spec(ln.shape), _const_spec(w1.shape), _const_spec(w3.shape),
                  _const_spec(w2.shape)],
        out_specs=pl.BlockSpec((tm, D_MODEL), lambda i: (i, 0)),
        scratch_shapes=[pltpu.VMEM((tm, D_MODEL), F32)],
        compiler_params=_params(1),
        name="ffn",
    )(x2, mods, ln, w1, w3, w2)


LANE_BLOCKS = 128 // SSM_GROUP
LANE_TILES = D_MODEL // 128


def _block_transpose(xs):
    xs = list(xs)
    blk = lax.broadcasted_iota(jnp.int32, xs[0].shape, 1) // SSM_GROUP
    k = LANE_BLOCKS // 2
    while k >= 1:
        keep_lo = (blk & k) == 0
        for i in range(LANE_BLOCKS):
            if i & k:
                continue
            lo, hi = xs[i], xs[i + k]
            xs[i] = jnp.where(keep_lo, lo, pltpu.roll(hi, SSM_GROUP * k, 1))
            xs[i + k] = jnp.where(keep_lo, pltpu.roll(lo, 128 - SSM_GROUP * k, 1), hi)
        k //= 2
    return xs


def _ssm_in_kernel(x_ref, mod_ref, w_ref, o_ref, u_scr):
    h = (x_ref[...] * (1.0 + mod_ref[0, 1:2, :]) + mod_ref[0, 0:1, :]).astype(BF16)
    u = _dot(h, w_ref[...])
    for c in range(LANE_TILES):
        u_scr[c] = u[:, c * 128:(c + 1) * 128]
    n_chunks = x_ref.shape[0] // SSM_CHUNK
    for c in range(LANE_TILES):
        for half in range(SSM_CHUNK // LANE_BLOCKS):
            per_token = [u_scr[c, pl.ds(half * LANE_BLOCKS + s, n_chunks, stride=SSM_CHUNK), :]
                         for s in range(LANE_BLOCKS)]
            per_group = _block_transpose(per_token)
            for gl in range(LANE_BLOCKS):
                o_ref[0, c * LANE_BLOCKS + gl, :, half * 128:(half + 1) * 128] = per_group[gl]


def _ssm_in(x2, mods, mod_base, rows_per_mod, w, nb):
    r = x2.shape[0]
    t = r // nb
    tm = _row_tile(math.gcd(rows_per_mod, t), 512)
    per_mod = rows_per_mod // tm
    per_seq = t // tm
    nj = tm // SSM_CHUNK
    return pl.pallas_call(
        _ssm_in_kernel,
        out_shape=jax.ShapeDtypeStruct((nb, N_GROUPS, t // SSM_CHUNK, SSM_FLAT), F32),
        grid=(r // tm,),
        in_specs=[pl.BlockSpec((tm, D_MODEL), lambda i: (i, 0)),
                  pl.BlockSpec((1, N_MODS, D_MODEL), lambda i: (mod_base + i // per_mod, 0, 0)),
                  _const_spec(w.shape)],
        out_specs=pl.BlockSpec((1, N_GROUPS, nj, SSM_FLAT), lambda i: (i // per_seq, 0, i % per_seq, 0)),
        scratch_shapes=[pltpu.VMEM((LANE_TILES, tm, 128), F32)],
        compiler_params=_params(1),
        name="ssm_in",
    )(x2, mods, w)


def _s5_prep_kernel(logdt_ref, lam_row_ref, lam_col_ref, bt_ref, ct_ref,
                    m_ref, s_ref, r_ref, ap_ref, *, n_steps):
    g = pl.program_id(0)
    lc = SSM_CHUNK

    def transition(lam_re, lam_im, shape, axis):
        fwd = lax.broadcasted_iota(jnp.int32, shape, axis) < STATE
        dt = jnp.exp(jnp.where(fwd, jnp.full(shape, logdt_ref[0, g], F32), jnp.full(shape, logdt_ref[1, g], F32)))
        mag = jnp.exp(lam_re * dt)
        return mag * jnp.cos(lam_im * dt), mag * jnp.sin(lam_im * dt), fwd

    def powers(a_re, a_im, n):
        pr, pi = [jnp.ones_like(a_re)], [jnp.zeros_like(a_im)]
        for _ in range(n):
            r, i = pr[-1], pi[-1]
            pr.append(r * a_re - i * a_im)
            pi.append(r * a_im + i * a_re)
        return pr, pi

    row_shape = (SSM_GROUP, 2 * STATE)
    lam_re = jnp.broadcast_to(lam_row_ref[0, 0:1, :], row_shape)
    lam_im = jnp.broadcast_to(lam_row_ref[0, 1:2, :], row_shape)
    a_re, a_im, fwd_lane = transition(lam_re, lam_im, row_shape, 1)
    den = lam_re * lam_re + lam_im * lam_im
    nr, ni = a_re - 1.0, a_im
    coef_re = (nr * lam_re + ni * lam_im) / den
    coef_im = (ni * lam_re - nr * lam_im) / den
    bt_re, bt_im = bt_ref[0, 0], bt_ref[0, 1]
    bb_re = coef_re * bt_re - coef_im * bt_im
    bb_im = coef_re * bt_im + coef_im * bt_re
    pr, pi = powers(a_re, a_im, lc)
    ab_re = [pr[k] * bb_re - pi[k] * bb_im for k in range(lc)]
    ab_im = [pr[k] * bb_im + pi[k] * bb_re for k in range(lc)]
    s_re = jnp.concatenate([jnp.where(fwd_lane, ab_re[lc - 1 - s], ab_re[s]) for s in range(lc)], axis=0)
    s_im = jnp.concatenate([jnp.where(fwd_lane, ab_im[lc - 1 - s], ab_im[s]) for s in range(lc)], axis=0)
    s_ref[0] = jnp.concatenate([s_re, s_im], axis=1).astype(BF16)

    ct_re, ct_im = ct_ref[0, 0], ct_ref[0, 1]
    fwd_row = lax.broadcasted_iota(jnp.int32, ct_re.shape, 0) < STATE
    dot_hi = lambda a, b: jnp.dot(a, b, precision=HIGHEST, preferred_element_type=F32)
    zero = jnp.zeros_like(ct_re)
    kf = dot_hi(s_re, jnp.where(fwd_row, ct_re, zero)) - dot_hi(s_im, jnp.where(fwd_row, ct_im, zero))
    kb = dot_hi(s_re, jnp.where(fwd_row, zero, ct_re)) - dot_hi(s_im, jnp.where(fwd_row, zero, ct_im))
    t_blk = lax.broadcasted_iota(jnp.int32, kf.shape, 1) // SSM_GROUP
    m_op = jnp.zeros_like(kf)
    for t in range(lc):
        up = (lc - 1 - t) * SSM_GROUP
        f_sh = kf if up == 0 else jnp.concatenate([kf[up:], jnp.zeros((up, SSM_FLAT), F32)], axis=0)
        down = t * SSM_GROUP
        b_sh = kb if down == 0 else jnp.concatenate([jnp.zeros((down, SSM_FLAT), F32), kb[:SSM_FLAT - down]], axis=0)
        m_op = jnp.where(t_blk == t, f_sh + b_sh, m_op)
    m_ref[0] = m_op.astype(BF16)

    sq_re, sq_im = [pr[lc]], [pi[lc]]
    for _ in range(n_steps - 1):
        r, i = sq_re[-1], sq_im[-1]
        sq_re.append(r * r - i * i)
        sq_im.append(2.0 * r * i)
    for i in range(n_steps):
        ap_ref[0, i:i + 1, :] = sq_re[i][0:1, :]
        ap_ref[0, n_steps + i:n_steps + i + 1, :] = sq_im[i][0:1, :]

    lam_re_c = jnp.broadcast_to(lam_col_ref[0, :, 0:1], ct_re.shape)
    lam_im_c = jnp.broadcast_to(lam_col_ref[0, :, 1:2], ct_re.shape)
    ac_re, ac_im, fwd_sub = transition(lam_re_c, lam_im_c, ct_re.shape, 0)
    pcr, pci = powers(ac_re, ac_im, lc)
    pw_re = jnp.zeros_like(ct_re)
    pw_im = jnp.zeros_like(ct_re)
    t_blk_r = lax.broadcasted_iota(jnp.int32, ct_re.shape, 1) // SSM_GROUP
    for t in range(lc):
        sel_re = jnp.where(fwd_sub, pcr[t + 1], pcr[lc - t])
        sel_im = jnp.where(fwd_sub, pci[t + 1], pci[lc - t])
        pw_re = jnp.where(t_blk_r == t, sel_re, pw_re)
        pw_im = jnp.where(t_blk_r == t, sel_im, pw_im)
    w_re = ct_re * pw_re - ct_im * pw_im
    w_im = ct_re * pw_im + ct_im * pw_re
    r_ref[0] = jnp.concatenate([w_re, -w_im], axis=0).astype(BF16)


def _s5_operators(lam_re, lam_im, log_dt, b_re, b_im, c_re, c_im, d_skip, n_steps):
    both = lambda x: jnp.concatenate([x[0], x[1]], -1)
    lam_row = jnp.stack([both(lam_re), both(lam_im)], 1)
    lam_col = lam_row.transpose(0, 2, 1)
    bt = jnp.stack([both(b_re.swapaxes(-1, -2)), both(b_im.swapaxes(-1, -2))], 1)
    ct_cols = lambda x: jnp.tile(jnp.concatenate([x[0], x[1]], -1).swapaxes(-1, -2), (1, 1, SSM_CHUNK))
    ct = jnp.stack([ct_cols(c_re), ct_cols(c_im)], 1)
    grp = lambda a: pl.BlockSpec((1,) + a.shape[1:], lambda g: (g,) + (0,) * (a.ndim - 1))
    out_sq = jax.ShapeDtypeStruct((N_GROUPS, SSM_FLAT, SSM_FLAT), BF16)
    out_ap = jax.ShapeDtypeStruct((N_GROUPS, 2 * n_steps, 2 * STATE), F32)
    m_op, s_op, r_op, a_pow = pl.pallas_call(
        functools.partial(_s5_prep_kernel, n_steps=n_steps),
        out_shape=(out_sq, out_sq, out_sq, out_ap),
        grid=(N_GROUPS,),
        in_specs=[pl.BlockSpec(memory_space=pltpu.SMEM), grp(lam_row), grp(lam_col), grp(bt), grp(ct)],
        out_specs=(grp(out_sq), grp(out_sq), grp(out_sq), grp(out_ap)),
        compiler_params=_params(1),
        name="s5_prep",
    )(log_dt, lam_row, lam_col, bt, ct)
    d_row = jnp.tile(d_skip.reshape(N_GROUPS, 1, SSM_GROUP), (1, 1, SSM_CHUNK))
    return m_op, s_op, r_op, a_pow, d_row


def _pair_operators(ops):
    m_op, s_op, r_op, a_pow, d_row = ops
    pairs = N_GROUPS // 2
    n_steps = a_pow.shape[1] // 2
    s5 = s_op.reshape(pairs, 2, SSM_FLAT, 4, STATE)
    s_pair = jnp.zeros((pairs, 2, SSM_FLAT, 4, 2, STATE), BF16)
    r5 = r_op.reshape(pairs, 2, 4, STATE, SSM_FLAT)
    r_pair = jnp.zeros((pairs, 4, 2, STATE, 2, SSM_FLAT), BF16)
    for gl in range(2):
        s_pair = s_pair.at[:, gl, :, :, gl, :].set(s5[:, gl])
        r_pair = r_pair.at[:, :, gl, :, gl, :].set(r5[:, gl])
    s_pair = s_pair.reshape(pairs, 2 * SSM_FLAT, 8 * STATE)
    r_pair = r_pair.reshape(pairs, 8 * STATE, 2 * SSM_FLAT)
    a6 = a_pow.reshape(pairs, 2, 2, n_steps, 2, STATE)
    a_pair = a6.transpose(0, 4, 2, 3, 1, 5).reshape(pairs, 4, n_steps, 2 * STATE)
    return m_op, s_pair, r_pair, a_pair, d_row


def _ssm_core_kernel(*refs, seq_chunks, n_steps, pairs, has_h0, emit_state):
    refs = list(refs)
    u_ref, m_ref, s_ref, r_ref, ap_ref, d_ref = refs[:6]
    refs = refs[6:]
    h0_ref = refs.pop(0) if has_h0 else None
    y_ref = refs.pop(0)
    e_ref = refs.pop(0) if emit_state else None
    rows = u_ref.shape[2]
    n_seq = rows // seq_chunks
    lanes = 2 * STATE
    pos = lax.broadcasted_iota(jnp.int32, (rows, lanes), 0) % seq_chunks

    def upstream(x, d, fwd):
        return pltpu.roll(x, d if fwd else rows - d, 0)

    def has_upstream(d, fwd):
        return (pos >= d) if fwd else (pos < seq_chunks - d)

    def scan(hr, hi, pi, fwd):
        q = 0 if fwd else 2
        for i in range(n_steps):
            d = 1 << i
            p_re, p_im = ap_ref[pi, q, i:i + 1, :], ap_ref[pi, q + 1, i:i + 1, :]
            if d < 8:
                ok = has_upstream(d, fwd)
                sr = jnp.where(ok, upstream(hr, d, fwd), 0.0)
                si = jnp.where(ok, upstream(hi, d, fwd), 0.0)
                hr, hi = hr + p_re * sr - p_im * si, hi + p_re * si + p_im * sr
            else:
                r3, i3 = hr.reshape(n_seq, seq_chunks, lanes), hi.reshape(n_seq, seq_chunks, lanes)
                keep = seq_chunks - d
                if fwd:
                    sr, si, tr, ti = r3[:, :keep], i3[:, :keep], r3[:, d:], i3[:, d:]
                else:
                    sr, si, tr, ti = r3[:, d:], i3[:, d:], r3[:, :keep], i3[:, :keep]
                nr = tr + p_re * sr - p_im * si
                ni = ti + p_re * si + p_im * sr
                if fwd:
                    r3 = jnp.concatenate([r3[:, :d], nr], axis=1)
                    i3 = jnp.concatenate([i3[:, :d], ni], axis=1)
                else:
                    r3 = jnp.concatenate([nr, r3[:, keep:]], axis=1)
                    i3 = jnp.concatenate([ni, i3[:, keep:]], axis=1)
                hr, hi = r3.reshape(rows, lanes), i3.reshape(rows, lanes)
        return hr, hi

    for pi in range(pairs):
        u0, u1 = u_ref[0, 2 * pi], u_ref[0, 2 * pi + 1]
        ub = jnp.concatenate([u0, u1], axis=1).astype(BF16)
        y0 = _dot(ub[:, :SSM_FLAT], m_ref[2 * pi]) + d_ref[2 * pi] * u0
        y1 = _dot(ub[:, SSM_FLAT:], m_ref[2 * pi + 1]) + d_ref[2 * pi + 1] * u1
        h = _dot(ub, s_ref[pi])
        state = {}
        for fwd, off in ((True, 0), (False, lanes)):
            hr, hi = h[:, off:off + lanes], h[:, 2 * lanes + off:3 * lanes + off]
            q = 0 if fwd else 2
            if has_h0:
                h0r = h0_ref[0, pi, 0:1, off:off + lanes]
                h0i = h0_ref[0, pi, 0:1, 2 * lanes + off:3 * lanes + off]
                a_re, a_im = ap_ref[pi, q, 0:1, :], ap_ref[pi, q + 1, 0:1, :]
                first = jnp.logical_not(has_upstream(1, fwd))
                hr = hr + jnp.where(first, a_re * h0r - a_im * h0i, 0.0)
                hi = hi + jnp.where(first, a_re * h0i + a_im * h0r, 0.0)
            else:
                h0r = h0i = 0.0
            hr, hi = scan(hr, hi, pi, fwd)
            ok = has_upstream(1, fwd)
            state[fwd] = (hr, hi, jnp.where(ok, upstream(hr, 1, fwd), h0r), jnp.where(ok, upstream(hi, 1, fwd), h0i))
        if emit_state:
            e_ref[0, pi] = jnp.concatenate([state[True][0], state[False][0], state[True][1], state[False][1]], axis=1)
        h_in = jnp.concatenate([state[True][2], state[False][2], state[True][3], state[False][3]], axis=1)
        y_state = _dot(h_in.astype(BF16), r_ref[pi])
        y_ref[0, 2 * pi] = y0 + y_state[:, :SSM_FLAT]
        y_ref[0, 2 * pi + 1] = y1 + y_state[:, SSM_FLAT:]


def _ssm_core(u_flat, pair_ops, h0, seq_chunks, emit_state):
    nb, _, rows, _ = u_flat.shape
    m_op, s_pair, r_pair, a_pair, d_row = pair_ops
    n_steps = a_pair.shape[2]
    assert (1 << n_steps) == seq_chunks
    pt = 4
    blk = pl.BlockSpec((1, 2 * pt, rows, SSM_FLAT), lambda b, g: (b, g, 0, 0))
    lead = lambda a, n: pl.BlockSpec((n,) + a.shape[1:], lambda b, g: (g,) + (0,) * (a.ndim - 1))
    in_specs = [blk, lead(m_op, 2 * pt), lead(s_pair, pt), lead(r_pair, pt), lead(a_pair, pt), lead(d_row, 2 * pt)]
    args = [u_flat, m_op, s_pair, r_pair, a_pair, d_row]
    if h0 is not None:
        in_specs.append(pl.BlockSpec((1, pt, 1, 8 * STATE), lambda b, g: (b, g, 0, 0)))
        args.append(h0)
    out_shape = [jax.ShapeDtypeStruct(u_flat.shape, F32)]
    out_specs = [blk]
    if emit_state:
        out_shape.append(jax.ShapeDtypeStruct((nb, N_GROUPS // 2, rows, 8 * STATE), F32))
        out_specs.append(pl.BlockSpec((1, pt, rows, 8 * STATE), lambda b, g: (b, g, 0, 0)))
    kern = functools.partial(_ssm_core_kernel, seq_chunks=seq_chunks, n_steps=n_steps, pairs=pt,
                             has_h0=h0 is not None, emit_state=emit_state)
    return pl.pallas_call(
        kern, out_shape=tuple(out_shape), grid=(nb, N_GROUPS // (2 * pt)),
        in_specs=in_specs, out_specs=tuple(out_specs),
        compiler_params=_params(2), name="ssm_core",
    )(*args)


def _ssm_out_kernel(y_ref, x_ref, mod_ref, ln_ref, wg_ref, wo_ref, out_ref, y_scr):
    n_chunks = x_ref.shape[0] // SSM_CHUNK
    for c in range(LANE_TILES):
        for half in range(SSM_CHUNK // LANE_BLOCKS):
            per_group = [y_ref[0, c * LANE_BLOCKS + gl, :, half * 128:(half + 1) * 128]
                         for gl in range(LANE_BLOCKS)]
            per_token = _block_transpose(per_group)
            for s in range(LANE_BLOCKS):
                y_scr[c, pl.ds(half * LANE_BLOCKS + s, n_chunks, stride=SSM_CHUNK), :] = per_token[s]
    y = jnp.concatenate([y_scr[c] for c in range(LANE_TILES)], axis=1)
    vg = _dot(_gelu_tanh(y).astype(BF16), wg_ref[...])
    glu = (vg[:, :D_MODEL] * jax.nn.sigmoid(vg[:, D_MODEL:])).astype(BF16)
    z = DN_ALPHA * x_ref[...] + mod_ref[0, 2:3, :] * _dot(glu, wo_ref[...])
    out_ref[...] = _layer_norm(z, ln_ref[0:1, :], ln_ref[1:2, :])


def _ssm_out(y_flat, x2, mods, mod_base, rows_per_mod, ln, w_glu, w_out):
    r = x2.shape[0]
    nb = y_flat.shape[0]
    t = r // nb
    tm = _row_tile(math.gcd(rows_per_mod, t), 512)
    per_mod = rows_per_mod // tm
    per_seq = t // tm
    nj = tm // SSM_CHUNK
    row = pl.BlockSpec((tm, D_MODEL), lambda i: (i, 0))
    return pl.pallas_call(
        _ssm_out_kernel,
        out_shape=jax.ShapeDtypeStruct((r, D_MODEL), F32),
        grid=(r // tm,),
        in_specs=[pl.BlockSpec((1, N_GROUPS, nj, SSM_FLAT), lambda i: (i // per_seq, 0, i % per_seq, 0)),
                  row,
                  pl.BlockSpec((1, N_MODS, D_MODEL), lambda i: (mod_base + i // per_mod, 0, 0)),
                  _const_spec(ln.shape), _const_spec(w_glu.shape), _const_spec(w_out.shape)],
        out_specs=row,
        scratch_shapes=[pltpu.VMEM((LANE_TILES, tm, 128), F32)],
        compiler_params=_params(1),
        name="ssm_out",
    )(y_flat, x2, mods, ln, w_glu, w_out)


def _ssm_mixer(x2, nb, seq_chunks, mods, mod_base, rows_per_mod, ln, w_in, ops, w_glu, w_out, h0, emit_state):
    u_flat = _ssm_in(x2, mods, mod_base, rows_per_mod, w_in, nb)
    outs = _ssm_core(u_flat, ops, h0, seq_chunks, emit_state)
    x_new = _ssm_out(outs[0], x2, mods, mod_base, rows_per_mod, ln, w_glu, w_out)
    return x_new, (outs[1] if emit_state else None)


def _rope_tables(t):
    n_rows = t // GRID_W
    row = jnp.repeat(jnp.arange(n_rows, dtype=F32), GRID_W)
    col = (jnp.arange(t) % GRID_W).astype(F32)
    nfreq = HEAD_DIM // 4
    inv = jnp.power(ROPE_BASE, -jnp.arange(nfreq, dtype=F32) / nfreq)
    ang_r, ang_c = row[:, None] * inv, col[:, None] * inv
    cos = jnp.concatenate([jnp.cos(ang_r)] * 2 + [jnp.cos(ang_c)] * 2, -1)
    sin = jnp.concatenate([-jnp.sin(ang_r), jnp.sin(ang_r), -jnp.sin(ang_c), jnp.sin(ang_c)], -1)
    return jnp.tile(cos, (1, 2)), jnp.tile(sin, (1, 2))


def kernel(x_prompt, x_sample, cache_k, cache_v, state_ssm_re, state_ssm_im, c, c_ctx, w_ada, b_ada, ln_g, ln_b, w_qkv, w_o, attn_sink, ssm_w_in, ssm_lam_re, ssm_lam_im, ssm_log_dt, ssm_b_re, ssm_b_im, ssm_c_re, ssm_c_im, ssm_d, ssm_w_glu, ssm_w_out, ffn_w1, ffn_w3, ffn_w2):
    nc, tc, _ = x_prompt.shape
    nl, tl, _ = x_sample.shape
    n_attn = w_qkv.shape[0]
    n_ssm = ssm_w_in.shape[0]

    n_mod_rows = -(-(1 + nl) // 8) * 8
    cvec = jnp.zeros((n_mod_rows, D_MODEL), F32).at[0].set(c_ctx).at[1:1 + nl].set(c)
    mods_all = _ada_mods(cvec, w_ada, b_ada)
    ln_all = jnp.stack([ln_g, ln_b], axis=2)

    bf = lambda w: w.astype(BF16)
    w_v = w_qkv[:, :, Q_WIDTH + KV_WIDTH:].reshape(n_attn, D_MODEL, N_KV_HEADS, HEAD_DIM)
    w_v_ext = jnp.pad(w_v, ((0, 0), (0, 0), (0, 0), (0, 128 - HEAD_DIM))).reshape(n_attn, D_MODEL, V_EXT)
    w_qkv_ctx = bf(jnp.concatenate([w_qkv, w_v_ext], axis=-1))
    w_qkv_lat = bf(jnp.concatenate([w_qkv[:, :, :Q_WIDTH + KV_WIDTH], w_v_ext], axis=-1))
    w_o_b = bf(w_o)
    w_in_b, w_glu_b, w_out_b = bf(ssm_w_in), bf(ssm_w_glu), bf(ssm_w_out)
    w1_b, w3_b, w2_b = bf(ffn_w1), bf(ffn_w3), bf(ffn_w2)
    cos, sin = _rope_tables(tl)
    kx = bf(cache_k).reshape(nl, n_attn, -1, KV_WIDTH)
    vx = jnp.concatenate([bf(cache_v), jnp.ones(cache_v.shape[:-1] + (128 - HEAD_DIM,), BF16)], -1)
    vx = vx.reshape(nl, n_attn, -1, V_EXT)

    steps_c = (tc // SSM_CHUNK).bit_length() - 1
    steps_l = (tl // SSM_CHUNK).bit_length() - 1
    ssm_ops = []
    for s in range(n_ssm):
        prm = (ssm_lam_re[s], ssm_lam_im[s], ssm_log_dt[s], ssm_b_re[s], ssm_b_im[s],
               ssm_c_re[s], ssm_c_im[s], ssm_d[s])
        ssm_ops.append(_pair_operators(_s5_operators(*prm, max(steps_c, steps_l))))

    def ops_for(s, n_steps):
        m_op, s_pair, r_pair, a_pair, d_row = ssm_ops[s]
        return m_op, s_pair, r_pair, a_pair[:, :, :n_steps], d_row

    xc = x_prompt.reshape(nc * tc, D_MODEL)
    xl = x_sample.reshape(nl * tl, D_MODEL)
    new_k, new_v, new_sr, new_si = [], [], [], []
    for l in range(DEPTH):
        mods = mods_all[l]
        ln1, ln2 = ln_all[l, 0], ln_all[l, 1]
        if l % 2 == 0:
            a = l // 2
            q, kb, vb, k32, v32 = _qkv_ctx(xc, mods, w_qkv_ctx[a])
            new_k.append(k32.reshape(nc, tc, N_KV_HEADS, HEAD_DIM))
            new_v.append(v32.reshape(nc, tc, N_KV_HEADS, HEAD_DIM))
            xc = _attn_ctx(attn_sink[a], q, kb, vb, xc, mods, ln1, w_o_b[a], tc)
            qp, qr, kr, v = _qkv_lat(xl, mods, w_qkv_lat[a], cos, sin, tl)
            xl = _attn_lat(attn_sink[a], qp, qr, kr, v, kx[:, a], vx[:, a], xl, mods, ln1, w_o_b[a], nl, tl)
        else:
            s = l // 2
            xc, e_c = _ssm_mixer(xc, 1, tc // SSM_CHUNK, mods, 0, nc * tc, ln1, w_in_b[s],
                                 ops_for(s, steps_c), w_glu_b[s], w_out_b[s], None, True)
            e6 = e_c.reshape(N_GROUPS // 2, nc, tc // SSM_CHUNK, 4, 2, STATE)
            fin = jnp.stack([e6[:, :, -1, 0], e6[:, :, 0, 1], e6[:, :, -1, 2], e6[:, :, 0, 3]], 0)
            fin = fin.transpose(0, 2, 1, 3, 4).reshape(4, nc, N_GROUPS, STATE)
            new_sr.append(jnp.stack([fin[0], fin[1]], 1))
            new_si.append(jnp.stack([fin[2], fin[3]], 1))
            h0 = jnp.stack([state_ssm_re[:, s, 0], state_ssm_re[:, s, 1],
                            state_ssm_im[:, s, 0], state_ssm_im[:, s, 1]], 1)
            h0 = h0.reshape(nl, 4, N_GROUPS // 2, 2 * STATE).transpose(0, 2, 1, 3)
            h0 = h0.reshape(nl, N_GROUPS // 2, 8 * STATE)
            xl, _ = _ssm_mixer(xl, nl, tl // SSM_CHUNK, mods, 1, tl, ln1, w_in_b[s],
                               ops_for(s, steps_l), w_glu_b[s], w_out_b[s], h0[:, :, None, :], False)
        xc = _ffn(xc, mods, 0, nc * tc, ln2, w1_b[l], w3_b[l], w2_b[l])
        xl = _ffn(xl, mods, 1, tl, ln2, w1_b[l], w3_b[l], w2_b[l])

    return (xc.reshape(nc, tc, D_MODEL), xl.reshape(nl, tl, D_MODEL),
            jnp.stack(new_k, axis=1), jnp.stack(new_v, axis=1),
            jnp.stack(new_sr, axis=1), jnp.stack(new_si, axis=1))
```

```python
import functools
import math

import jax
import jax.numpy as jnp
from jax import lax
from jax.experimental import pallas as pl
from jax.experimental.pallas import tpu as pltpu

F32 = jnp.float32
BF16 = jnp.bfloat16

D_MODEL = 1024
DEPTH = 4
N_HEADS = 16
N_KV_HEADS = 4
HEAD_DIM = 64
HEADS_PER_KV = N_HEADS // N_KV_HEADS
Q_WIDTH = N_HEADS * HEAD_DIM
KV_WIDTH = N_KV_HEADS * HEAD_DIM
GRID_W = 64
ATT_BLOCK = 128
ROPE_BASE = 10000.0
SSM_GROUP = 16
N_GROUPS = D_MODEL // SSM_GROUP
STATE = 64
SSM_CHUNK = 16
SSM_FLAT = SSM_CHUNK * SSM_GROUP
D_FF = -(-8 * D_MODEL // (3 * 256)) * 256
FF_CHUNK = 256
DN_ALPHA = (2 * DEPTH) ** 0.25
LN_EPS = 1e-5
NEG_INF = -1e30
N_MODS = 6
VMEM_LIMIT = 56 * 1024 * 1024
HIGHEST = lax.Precision.HIGHEST


def _params(n_axes):
    return pltpu.CompilerParams(dimension_semantics=("arbitrary",) * n_axes,
                                vmem_limit_bytes=VMEM_LIMIT)


def _const_spec(shape):
    return pl.BlockSpec(shape, lambda *_: (0,) * len(shape))


def _layer_norm(y, g, b):
    mu = jnp.mean(y, -1, keepdims=True)
    d = y - mu
    var = jnp.mean(d * d, -1, keepdims=True)
    return d * lax.rsqrt(var + LN_EPS) * g + b


def _silu(x):
    return x * jax.nn.sigmoid(x)


def _gelu_tanh(x):
    return 0.5 * x * (1.0 + jnp.tanh(math.sqrt(2.0 / math.pi) * (x + 0.044715 * (x * x * x))))


def _dot(a, b):
    return jnp.dot(a, b, preferred_element_type=F32)


def _dot_nt(a, b):
    return lax.dot_general(a, b, (((1,), (1,)), ((), ())), preferred_element_type=F32)


def _row_tile(rows_per_mod, want):
    tm = min(want, rows_per_mod)
    assert rows_per_mod % tm == 0
    return tm


def _mods_kernel(c_ref, w_ref, b_ref, o_ref):
    a = _silu(c_ref[...]).astype(BF16)
    o_ref[0] = _dot(a, w_ref[0].astype(BF16)) + b_ref[0]


def _ada_mods(cvec, w_ada, b_ada):
    r = cvec.shape[0]
    tn = 1536
    out = pl.pallas_call(
        _mods_kernel,
        out_shape=jax.ShapeDtypeStruct((DEPTH, r, N_MODS * D_MODEL), F32),
        grid=(DEPTH, N_MODS * D_MODEL // tn),
        in_specs=[pl.BlockSpec((r, D_MODEL), lambda l, j: (0, 0)),
                  pl.BlockSpec((1, D_MODEL, tn), lambda l, j: (l, 0, j)),
                  pl.BlockSpec((1, 1, tn), lambda l, j: (l, 0, j))],
        out_specs=pl.BlockSpec((1, r, tn), lambda l, j: (l, 0, j)),
        compiler_params=_params(2),
        name="ada_mods",
    )(cvec, w_ada, b_ada.reshape(DEPTH, 1, N_MODS * D_MODEL))
    return out.reshape(DEPTH, r, N_MODS, D_MODEL)


Q_SCALE = HEAD_DIM ** -0.5 * math.log2(math.e)
V_EXT = N_KV_HEADS * 128


def _rope(x, cos, sin, first_half):
    outs = []
    for c in range(x.shape[1] // 128):
        xc = x[:, c * 128:(c + 1) * 128]
        partner = jnp.where(first_half, pltpu.roll(xc, 128 - 16, 1), pltpu.roll(xc, 16, 1))
        outs.append(xc * cos + partner * sin)
    return jnp.concatenate(outs, axis=1)


def _ones_lanes(width):
    return jnp.where((lax.broadcasted_iota(jnp.int32, (1, width), 1) % 128) >= HEAD_DIM, 1.0, 0.0)


def _qkv_ctx_kernel(x_ref, mod_ref, w_ref, q_ref, kb_ref, vb_ref, k_ref, v_ref):
    h = (x_ref[...] * (1.0 + mod_ref[0, 1:2, :]) + mod_ref[0, 0:1, :]).astype(BF16)
    qkv = _dot(h, w_ref[...])
    q_ref[...] = (qkv[:, :Q_WIDTH] * Q_SCALE).astype(BF16)
    k = qkv[:, Q_WIDTH:Q_WIDTH + KV_WIDTH]
    k_ref[...] = k
    v_ref[...] = qkv[:, Q_WIDTH + KV_WIDTH:Q_WIDTH + 2 * KV_WIDTH]
    kb_ref[...] = k.astype(BF16)
    vb_ref[...] = (qkv[:, Q_WIDTH + 2 * KV_WIDTH:] + _ones_lanes(V_EXT)).astype(BF16)


def _qkv_lat_kernel(x_ref, mod_ref, w_ref, cos_ref, sin_ref, qp_ref, qr_ref, kr_ref, v_ref):
    h = (x_ref[...] * (1.0 + mod_ref[0, 1:2, :]) + mod_ref[0, 0:1, :]).astype(BF16)
    qkv = _dot(h, w_ref[...])
    cos = cos_ref[...]
    sin = sin_ref[...]
    first_half = (lax.broadcasted_iota(jnp.int32, cos.shape, 1) & 16) == 0
    q = qkv[:, :Q_WIDTH] * Q_SCALE
    qp_ref[...] = q.astype(BF16)
    qr_ref[...] = _rope(q, cos, sin, first_half).astype(BF16)
    kr_ref[...] = _rope(qkv[:, Q_WIDTH:Q_WIDTH + KV_WIDTH], cos, sin, first_half).astype(BF16)
    v_ref[...] = (qkv[:, Q_WIDTH + KV_WIDTH:] + _ones_lanes(V_EXT)).astype(BF16)


def _qkv_ctx(x2, mods, w_ext):
    r = x2.shape[0]
    tm = _row_tile(r, 512)
    row = lambda w: pl.BlockSpec((tm, w), lambda i: (i, 0))
    return pl.pallas_call(
        _qkv_ctx_kernel,
        out_shape=(jax.ShapeDtypeStruct((r, Q_WIDTH), BF16),
                   jax.ShapeDtypeStruct((r, KV_WIDTH), BF16),
                   jax.ShapeDtypeStruct((r, V_EXT), BF16),
                   jax.ShapeDtypeStruct((r, KV_WIDTH), F32),
                   jax.ShapeDtypeStruct((r, KV_WIDTH), F32)),
        grid=(r // tm,),
        in_specs=[row(D_MODEL),
                  pl.BlockSpec((1, N_MODS, D_MODEL), lambda i: (0, 0, 0)),
                  _const_spec(w_ext.shape)],
        out_specs=(row(Q_WIDTH), row(KV_WIDTH), row(V_EXT), row(KV_WIDTH), row(KV_WIDTH)),
        compiler_params=_params(1),
        name="qkv_ctx",
    )(x2, mods, w_ext)


def _qkv_lat(x2, mods, w_ext, cos, sin, t):
    r = x2.shape[0]
    tm = _row_tile(t, 512)
    per_seq = t // tm
    row = lambda w: pl.BlockSpec((tm, w), lambda i: (i, 0))
    tab = pl.BlockSpec((tm, 128), lambda i: (i % per_seq, 0))
    return pl.pallas_call(
        _qkv_lat_kernel,
        out_shape=(jax.ShapeDtypeStruct((r, Q_WIDTH), BF16),
                   jax.ShapeDtypeStruct((r, Q_WIDTH), BF16),
                   jax.ShapeDtypeStruct((r, KV_WIDTH), BF16),
                   jax.ShapeDtypeStruct((r, V_EXT), BF16)),
        grid=(r // tm,),
        in_specs=[row(D_MODEL),
                  pl.BlockSpec((1, N_MODS, D_MODEL), lambda i: (1 + i // per_seq, 0, 0)),
                  _const_spec(w_ext.shape), tab, tab],
        out_specs=(row(Q_WIDTH), row(Q_WIDTH), row(KV_WIDTH), row(V_EXT)),
        compiler_params=_params(1),
        name="qkv_lat",
    )(x2, mods, w_ext, cos, sin)


def _stack_heads(ref, kv):
    base = kv * HEADS_PER_KV * HEAD_DIM
    return jnp.concatenate(
        [ref[:, base + g * HEAD_DIM: base + (g + 1) * HEAD_DIM] for g in range(HEADS_PER_KV)], axis=0)


def _sink_column(sink_ref, kv, rows):
    return jnp.concatenate(
        [jnp.full((rows, 1), sink_ref[kv * HEADS_PER_KV + g] * math.log2(math.e), F32)
         for g in range(HEADS_PER_KV)], axis=0)


def _normalise_and_store(o_scr, o_ext, sink_weight, kv, rows):
    den = pltpu.roll(o_ext, HEAD_DIM, 1) + sink_weight
    o = o_ext / den
    base = kv * HEADS_PER_KV * HEAD_DIM
    for g in range(HEADS_PER_KV):
        o_scr[:, base + g * HEAD_DIM: base + (g + 1) * HEAD_DIM] = (
            o[g * rows:(g + 1) * rows, :HEAD_DIM].astype(BF16))


def _attn_epilogue(o_scr, x_ref, mod_ref, ln_ref, wo_ref, out_ref):
    y = _dot(o_scr[...], wo_ref[...])
    z = DN_ALPHA * x_ref[...] + mod_ref[0, 2:3, :] * y
    out_ref[...] = _layer_norm(z, ln_ref[0:1, :], ln_ref[1:2, :])


def _attn_ctx_kernel(sink_ref, q_ref, k_ref, v_ref, x_ref, mod_ref, ln_ref, wo_ref, out_ref, o_scr):
    rows = q_ref.shape[0]
    for kv in range(N_KV_HEADS):
        q4 = _stack_heads(q_ref, kv)
        sink = _sink_column(sink_ref, kv, rows)
        s = _dot_nt(q4, k_ref[:, kv * HEAD_DIM:(kv + 1) * HEAD_DIM])
        m = jnp.maximum(jnp.max(s, -1, keepdims=True), sink)
        p = jnp.exp2(s - m).astype(BF16)
        o_ext = _dot(p, v_ref[:, kv * 128:(kv + 1) * 128])
        _normalise_and_store(o_scr, o_ext, jnp.exp2(sink - m), kv, rows)
    _attn_epilogue(o_scr, x_ref, mod_ref, ln_ref, wo_ref, out_ref)


def _attn_ctx(sink, q, k, v_ext, x2, mods, ln, w_o, seq):
    r = x2.shape[0]
    row = lambda w: pl.BlockSpec((seq, w), lambda i: (i, 0))
    return pl.pallas_call(
        _attn_ctx_kernel,
        out_shape=jax.ShapeDtypeStruct((r, D_MODEL), F32),
        grid=(r // seq,),
        in_specs=[pl.BlockSpec(memory_space=pltpu.SMEM),
                  row(Q_WIDTH), row(KV_WIDTH), row(V_EXT), row(D_MODEL),
                  pl.BlockSpec((1, N_MODS, D_MODEL), lambda i: (0, 0, 0)),
                  _const_spec(ln.shape), _const_spec(w_o.shape)],
        out_specs=row(D_MODEL),
        scratch_shapes=[pltpu.VMEM((seq, Q_WIDTH), BF16)],
        compiler_params=_params(1),
        name="attn_ctx",
    )(sink, q, k, v_ext, x2, mods, ln, w_o)


LOCAL_KEYS = 3 * ATT_BLOCK
ATT_STEP_BLOCKS = 2


def _attn_lat_kernel(sink_ref, qp_ref, qr_ref, k_ref, v_ref, kx_ref, vx_ref, x_ref, mod_ref, ln_ref,
                     wo_ref, out_ref, o_scr, s_scr, m_scr, bias_scr):
    rows = ATT_BLOCK
    seq = k_ref.shape[0]
    n_ctx = kx_ref.shape[1]
    n_tiles = (n_ctx + LOCAL_KEYS) // 128
    n_sub = qp_ref.shape[0] // rows
    r_idx = lax.broadcasted_iota(jnp.int32, (rows, LOCAL_KEYS), 0)
    c_idx = lax.broadcasted_iota(jnp.int32, (rows, LOCAL_KEYS), 1)
    starts = []
    for sub in range(n_sub):
        j = pl.program_id(1) * n_sub + sub
        start = pl.multiple_of(jnp.clip((j - 1) * rows, 0, seq - LOCAL_KEYS), rows)
        bias_scr[sub] = jnp.where(jnp.abs(r_idx - c_idx + (j * rows - start)) <= ATT_BLOCK, 0.0, NEG_INF)
        starts.append(start)

    def scores(sub, kv, slot):
        lo, hi = kv * HEAD_DIM, (kv + 1) * HEAD_DIM
        q0 = sub * rows
        kx = kx_ref[0, :, lo:hi]
        kl = k_ref[pl.ds(starts[sub], LOCAL_KEYS), lo:hi]
        for g in range(HEADS_PER_KV):
            h = kv * HEADS_PER_KV + g
            r0, r1 = g * rows, (g + 1) * rows
            s_x = _dot_nt(qp_ref[q0:q0 + rows, h * HEAD_DIM:(h + 1) * HEAD_DIM], kx)
            s_l = _dot_nt(qr_ref[q0:q0 + rows, h * HEAD_DIM:(h + 1) * HEAD_DIM], kl) + bias_scr[sub]
            m = jnp.maximum(jnp.maximum(jnp.max(s_x, -1, keepdims=True), jnp.max(s_l, -1, keepdims=True)),
                            sink_ref[h] * math.log2(math.e))
            s_scr[slot, r0:r1, :n_ctx] = s_x
            s_scr[slot, r0:r1, n_ctx:] = s_l
            m_scr[slot, r0:r1, :] = jnp.broadcast_to(m, (rows, 128))

    def values(sub, kv, slot):
        q0 = sub * rows
        vx = vx_ref[0, :, kv * 128:(kv + 1) * 128]
        vl = v_ref[pl.ds(starts[sub], LOCAL_KEYS), kv * 128:(kv + 1) * 128]
        for g in range(HEADS_PER_KV):
            h = kv * HEADS_PER_KV + g
            r0, r1 = g * rows, (g + 1) * rows
            m = m_scr[slot, r0:r1, :]
            p = [jnp.exp2(s_scr[slot, r0:r1, c * 128:(c + 1) * 128] - m).astype(BF16) for c in range(n_tiles)]
            o_ext = (_dot(jnp.concatenate(p[:n_ctx // 128], axis=1), vx)
                     + _dot(jnp.concatenate(p[n_ctx // 128:], axis=1), vl))
            den = pltpu.roll(o_ext, HEAD_DIM, 1) + jnp.exp2(sink_ref[h] * math.log2(math.e) - m)
            o_scr[q0:q0 + rows, h * HEAD_DIM:(h + 1) * HEAD_DIM] = (o_ext / den)[:, :HEAD_DIM].astype(BF16)

    units = [(sub, kv) for sub in range(n_sub) for kv in range(N_KV_HEADS)]
    scores(*units[0], 0)
    for u, unit in enumerate(units):
        if u + 1 < len(units):
            scores(*units[u + 1], (u + 1) % 2)
        values(*unit, u % 2)
    _attn_epilogue(o_scr, x_ref, mod_ref, ln_ref, wo_ref, out_ref)


def _attn_lat(sink, qp, qr, kr, v_ext, k_ctx, v_ctx_ext, x2, mods, ln, w_o, n, t):
    assert t >= LOCAL_KEYS
    blk = ATT_STEP_BLOCKS * ATT_BLOCK
    nblk = t // blk
    row = lambda w: pl.BlockSpec((blk, w), lambda b, j: (b * nblk + j, 0))
    seq = lambda w: pl.BlockSpec((t, w), lambda b, j: (b, 0))
    ctx = lambda a: pl.BlockSpec((1,) + a.shape[1:], lambda b, j: (b, 0, 0))
    return pl.pallas_call(
        _attn_lat_kernel,
        out_shape=jax.ShapeDtypeStruct((n * t, D_MODEL), F32),
        grid=(n, nblk),
        in_specs=[pl.BlockSpec(memory_space=pltpu.SMEM),
                  row(Q_WIDTH), row(Q_WIDTH), seq(KV_WIDTH), seq(V_EXT),
                  ctx(k_ctx), ctx(v_ctx_ext), row(D_MODEL),
                  pl.BlockSpec((1, N_MODS, D_MODEL), lambda b, j: (1 + b, 0, 0)),
                  _const_spec(ln.shape), _const_spec(w_o.shape)],
        out_specs=row(D_MODEL),
        scratch_shapes=[pltpu.VMEM((blk, Q_WIDTH), BF16),
                        pltpu.VMEM((2, HEADS_PER_KV * ATT_BLOCK, k_ctx.shape[1] + LOCAL_KEYS), F32),
                        pltpu.VMEM((2, HEADS_PER_KV * ATT_BLOCK, 128), F32),
                        pltpu.VMEM((ATT_STEP_BLOCKS, ATT_BLOCK, LOCAL_KEYS), F32)],
        compiler_params=_params(2),
        name="attn_lat",
    )(sink, qp, qr, kr, v_ext, k_ctx, v_ctx_ext, x2, mods, ln, w_o)


def _ffn_kernel(x_ref, mod_ref, ln_ref, w1_ref, w3_ref, w2_ref, out_ref, acc_ref):
    x = x_ref[...]
    h = (x * (1.0 + mod_ref[0, 4:5, :]) + mod_ref[0, 3:4, :]).astype(BF16)
    for c in range(D_FF // FF_CHUNK):
        lo, hi = c * FF_CHUNK, (c + 1) * FF_CHUNK
        a = _dot(h, w1_ref[:, lo:hi])
        b = _dot(h, w3_ref[:, lo:hi])
        y = _dot((_silu(a) * b).astype(BF16), w2_ref[lo:hi, :])
        if c == 0:
            acc_ref[...] = y
        else:
            acc_ref[...] += y
    z = DN_ALPHA * x + mod_ref[0, 5:6, :] * acc_ref[...]
    out_ref[...] = _layer_norm(z, ln_ref[0:1, :], ln_ref[1:2, :])


def _ffn(x2, mods, mod_base, rows_per_mod, ln, w1, w3, w2):
    r = x2.shape[0]
    tm = _row_tile(rows_per_mod, 512)
    per_mod = rows_per_mod // tm
    return pl.pallas_call(
        _ffn_kernel,
        out_shape=jax.ShapeDtypeStruct((r, D_MODEL), F32),
        grid=(r // tm,),
        in_specs=[pl.BlockSpec((tm, D_MODEL), lambda i: (i, 0)),
                  pl.BlockSpec((1, N_MODS, D_MODEL), lambda i: (mod_base + i // per_mod, 0, 0)),
                  _const_spec(ln.shape), _const_spec(w1.shape), _const_spec(w3.shape),
                  _const_spec(w2.shape)],
        out_specs=pl.BlockSpec((tm, D_MODEL), lambda i: (i, 0)),
        scratch_shapes=[pltpu.VMEM((tm, D_MODEL), F32)],
        compiler_params=_params(1),
        name="ffn",
    )(x2, mods, ln, w1, w3, w2)


LANE_BLOCKS = 128 // SSM_GROUP
LANE_TILES = D_MODEL // 128


def _block_transpose(xs):
    xs = list(xs)
    blk = lax.broadcasted_iota(jnp.int32, xs[0].shape, 1) // SSM_GROUP
    k = LANE_BLOCKS // 2
    while k >= 1:
        keep_lo = (blk & k) == 0
        for i in range(LANE_BLOCKS):
            if i & k:
                continue
            lo, hi = xs[i], xs[i + k]
            xs[i] = jnp.where(keep_lo, lo, pltpu.roll(hi, SSM_GROUP * k, 1))
            xs[i + k] = jnp.where(keep_lo, pltpu.roll(lo, 128 - SSM_GROUP * k, 1), hi)
        k //= 2
    return xs


def _ssm_in_kernel(x_ref, mod_ref, w_ref, o_ref, u_scr):
    h = (x_ref[...] * (1.0 + mod_ref[0, 1:2, :]) + mod_ref[0, 0:1, :]).astype(BF16)
    u = _dot(h, w_ref[...])
    for c in range(LANE_TILES):
        u_scr[c] = u[:, c * 128:(c + 1) * 128]
    n_chunks = x_ref.shape[0] // SSM_CHUNK
    for c in range(LANE_TILES):
        for half in range(SSM_CHUNK // LANE_BLOCKS):
            per_token = [u_scr[c, pl.ds(half * LANE_BLOCKS + s, n_chunks, stride=SSM_CHUNK), :]
                         for s in range(LANE_BLOCKS)]
            per_group = _block_transpose(per_token)
            for gl in range(LANE_BLOCKS):
                o_ref[0, c * LANE_BLOCKS + gl, :, half * 128:(half + 1) * 128] = per_group[gl]


def _ssm_in(x2, mods, mod_base, rows_per_mod, w, nb):
    r = x2.shape[0]
    t = r // nb
    tm = _row_tile(math.gcd(rows_per_mod, t), 512)
    per_mod = rows_per_mod // tm
    per_seq = t // tm
    nj = tm // SSM_CHUNK
    return pl.pallas_call(
        _ssm_in_kernel,
        out_shape=jax.ShapeDtypeStruct((nb, N_GROUPS, t // SSM_CHUNK, SSM_FLAT), F32),
        grid=(r // tm,),
        in_specs=[pl.BlockSpec((tm, D_MODEL), lambda i: (i, 0)),
                  pl.BlockSpec((1, N_MODS, D_MODEL), lambda i: (mod_base + i // per_mod, 0, 0)),
                  _const_spec(w.shape)],
        out_specs=pl.BlockSpec((1, N_GROUPS, nj, SSM_FLAT), lambda i: (i // per_seq, 0, i % per_seq, 0)),
        scratch_shapes=[pltpu.VMEM((LANE_TILES, tm, 128), F32)],
        compiler_params=_params(1),
        name="ssm_in",
    )(x2, mods, w)


def _s5_prep_kernel(logdt_ref, lam_row_ref, lam_col_ref, bt_ref, ct_ref,
                    m_ref, s_ref, r_ref, ap_ref, *, n_steps):
    pair = pl.program_id(0)
    lc = SSM_CHUNK
    lanes = 2 * STATE

    def transition(lam_re, lam_im, d, axis):
        first = lax.broadcasted_iota(jnp.int32, lam_re.shape, axis) < STATE
        log_dt = jnp.where(first, jnp.full(lam_re.shape, logdt_ref[d, 2 * pair], F32),
                           jnp.full(lam_re.shape, logdt_ref[d, 2 * pair + 1], F32))
        dt = jnp.exp(log_dt)
        mag = jnp.exp(lam_re * dt)
        return mag * jnp.cos(lam_im * dt), mag * jnp.sin(lam_im * dt)

    def powers(a_re, a_im, n):
        pr, pi = [jnp.ones_like(a_re)], [jnp.zeros_like(a_im)]
        for _ in range(n):
            r, i = pr[-1], pi[-1]
            pr.append(r * a_re - i * a_im)
            pi.append(r * a_im + i * a_re)
        return pr, pi

    dot_hi = lambda a, b: jnp.dot(a, b, precision=HIGHEST, preferred_element_type=F32)
    row_shape = (SSM_GROUP, lanes)
    x_group = lax.broadcasted_iota(jnp.int32, (SSM_FLAT, lanes), 1) // STATE
    w_group = lax.broadcasted_iota(jnp.int32, (lanes, SSM_FLAT), 0) // STATE
    t_blk = lax.broadcasted_iota(jnp.int32, (SSM_FLAT, SSM_FLAT), 1) // SSM_GROUP
    t_blk_w = lax.broadcasted_iota(jnp.int32, (lanes, SSM_FLAT), 1) // SSM_GROUP
    m_acc = [jnp.zeros((SSM_FLAT, SSM_FLAT), F32) for _ in range(2)]

    for d in range(2):
        fwd = d == 0
        lam_re = jnp.broadcast_to(lam_row_ref[0, 2 * d:2 * d + 1, :], row_shape)
        lam_im = jnp.broadcast_to(lam_row_ref[0, 2 * d + 1:2 * d + 2, :], row_shape)
        a_re, a_im = transition(lam_re, lam_im, d, 1)
        den = lam_re * lam_re + lam_im * lam_im
        nr, ni = a_re - 1.0, a_im
        coef_re = (nr * lam_re + ni * lam_im) / den
        coef_im = (ni * lam_re - nr * lam_im) / den
        bt_re, bt_im = bt_ref[0, 2 * d], bt_ref[0, 2 * d + 1]
        bb_re = coef_re * bt_re - coef_im * bt_im
        bb_im = coef_re * bt_im + coef_im * bt_re
        pr, pi = powers(a_re, a_im, lc)
        order = [lc - 1 - s for s in range(lc)] if fwd else list(range(lc))
        x_re = jnp.concatenate([pr[k] * bb_re - pi[k] * bb_im for k in order], axis=0)
        x_im = jnp.concatenate([pr[k] * bb_im + pi[k] * bb_re for k in order], axis=0)
        sq_re, sq_im = pr[lc], pi[lc]
        for i in range(n_steps):
            ap_ref[0, 2 * d, i:i + 1, :] = sq_re[0:1, :]
            ap_ref[0, 2 * d + 1, i:i + 1, :] = sq_im[0:1, :]
            sq_re, sq_im = sq_re * sq_re - sq_im * sq_im, 2.0 * sq_re * sq_im
        ct_re, ct_im = ct_ref[0, 2 * d], ct_ref[0, 2 * d + 1]
        for gl in range(2):
            xg_re = jnp.where(x_group == gl, x_re, 0.0)
            xg_im = jnp.where(x_group == gl, x_im, 0.0)
            r0 = gl * SSM_FLAT
            s_ref[0, r0:r0 + SSM_FLAT, d * lanes:(d + 1) * lanes] = xg_re.astype(BF16)
            s_ref[0, r0:r0 + SSM_FLAT, (2 + d) * lanes:(3 + d) * lanes] = xg_im.astype(BF16)
            kw = dot_hi(xg_re, ct_re) - dot_hi(xg_im, ct_im)
            for t in range(lc):
                if fwd:
                    sh = (lc - 1 - t) * SSM_GROUP
                    moved = kw if sh == 0 else jnp.concatenate([kw[sh:], jnp.zeros((sh, SSM_FLAT), F32)], axis=0)
                else:
                    sh = t * SSM_GROUP
                    moved = kw if sh == 0 else jnp.concatenate(
                        [jnp.zeros((sh, SSM_FLAT), F32), kw[:SSM_FLAT - sh]], axis=0)
                m_acc[gl] = m_acc[gl] + jnp.where(t_blk == t, moved, 0.0)

        lam_re_c = jnp.broadcast_to(lam_col_ref[0, :, 2 * d:2 * d + 1], ct_re.shape)
        lam_im_c = jnp.broadcast_to(lam_col_ref[0, :, 2 * d + 1:2 * d + 2], ct_re.shape)
        ac_re, ac_im = transition(lam_re_c, lam_im_c, d, 0)
        pcr, pci = powers(ac_re, ac_im, lc)
        pw_re = jnp.zeros_like(ct_re)
        pw_im = jnp.zeros_like(ct_re)
        for t in range(lc):
            k = t + 1 if fwd else lc - t
            pw_re = jnp.where(t_blk_w == t, pcr[k], pw_re)
            pw_im = jnp.where(t_blk_w == t, pci[k], pw_im)
        w_re = ct_re * pw_re - ct_im * pw_im
        w_im = ct_re * pw_im + ct_im * pw_re
        for gl in range(2):
            c0 = gl * SSM_FLAT
            r_ref[0, d * lanes:(d + 1) * lanes, c0:c0 + SSM_FLAT] = jnp.where(w_group == gl, w_re, 0.0).astype(BF16)
            r_ref[0, (2 + d) * lanes:(3 + d) * lanes, c0:c0 + SSM_FLAT] = jnp.where(w_group == gl, -w_im, 0.0).astype(BF16)

    for gl in range(2):
        m_ref[gl] = m_acc[gl].astype(BF16)


def _s5_operators(lam_re, lam_im, log_dt, b_re, b_im, c_re, c_im, d_skip, n_steps):
    pairs = N_GROUPS // 2
    quantities = lambda re, im: jnp.stack([re[0], im[0], re[1], im[1]], 1)
    lam_row = quantities(lam_re.reshape(2, pairs, 2 * STATE), lam_im.reshape(2, pairs, 2 * STATE))
    lam_col = lam_row.transpose(0, 2, 1)
    bt_lanes = lambda x: (x.reshape(2, pairs, 2, STATE, SSM_GROUP).transpose(0, 1, 4, 2, 3)
                          .reshape(2, pairs, SSM_GROUP, 2 * STATE))
    bt = quantities(bt_lanes(b_re), bt_lanes(b_im))
    ct_rows = lambda x: jnp.tile(x.reshape(2, pairs, 2, SSM_GROUP, STATE).transpose(0, 1, 2, 4, 3)
                                 .reshape(2, pairs, 2 * STATE, SSM_GROUP), (1, 1, 1, SSM_CHUNK))
    ct = quantities(ct_rows(c_re), ct_rows(c_im))
    lead = lambda a, n: pl.BlockSpec((n,) + a.shape[1:], lambda g: (g,) + (0,) * (len(a.shape) - 1))
    out_m = jax.ShapeDtypeStruct((N_GROUPS, SSM_FLAT, SSM_FLAT), BF16)
    out_sq = jax.ShapeDtypeStruct((pairs, 2 * SSM_FLAT, 2 * SSM_FLAT), BF16)
    out_ap = jax.ShapeDtypeStruct((pairs, 4, n_steps, 2 * STATE), F32)
    m_op, s_pair, r_pair, a_pair = pl.pallas_call(
        functools.partial(_s5_prep_kernel, n_steps=n_steps),
        out_shape=(out_m, out_sq, out_sq, out_ap),
        grid=(pairs,),
        in_specs=[pl.BlockSpec(memory_space=pltpu.SMEM), lead(lam_row, 1), lead(lam_col, 1), lead(bt, 1), lead(ct, 1)],
        out_specs=(lead(out_m, 2), lead(out_sq, 1), lead(out_sq, 1), lead(out_ap, 1)),
        compiler_params=_params(1),
        name="s5_prep",
    )(log_dt, lam_row, lam_col, bt, ct)
    d_row = jnp.tile(d_skip.reshape(N_GROUPS, 1, SSM_GROUP), (1, 1, SSM_CHUNK))
    return m_op, s_pair, r_pair, a_pair, d_row


def _ssm_core_kernel(*refs, seq_chunks, n_steps, pairs, has_h0, emit_state):
    refs = list(refs)
    u_ref, m_ref, s_ref, r_ref, ap_ref, d_ref = refs[:6]
    refs = refs[6:]
    h0_ref = refs.pop(0) if has_h0 else None
    y_ref = refs.pop(0)
    e_ref = refs.pop(0) if emit_state else None
    rows = u_ref.shape[2]
    n_seq = rows // seq_chunks
    lanes = 2 * STATE
    pos = lax.broadcasted_iota(jnp.int32, (rows, lanes), 0) % seq_chunks

    def upstream(x, d, fwd):
        return pltpu.roll(x, d if fwd else rows - d, 0)

    def has_upstream(d, fwd):
        return (pos >= d) if fwd else (pos < seq_chunks - d)

    def scan(hr, hi, pi, fwd):
        q = 0 if fwd else 2
        for i in range(n_steps):
            d = 1 << i
            p_re, p_im = ap_ref[pi, q, i:i + 1, :], ap_ref[pi, q + 1, i:i + 1, :]
            if d < 8:
                ok = has_upstream(d, fwd)
                sr = jnp.where(ok, upstream(hr, d, fwd), 0.0)
                si = jnp.where(ok, upstream(hi, d, fwd), 0.0)
                hr, hi = hr + p_re * sr - p_im * si, hi + p_re * si + p_im * sr
            else:
                r3, i3 = hr.reshape(n_seq, seq_chunks, lanes), hi.reshape(n_seq, seq_chunks, lanes)
                keep = seq_chunks - d
                if fwd:
                    sr, si, tr, ti = r3[:, :keep], i3[:, :keep], r3[:, d:], i3[:, d:]
                else:
                    sr, si, tr, ti = r3[:, d:], i3[:, d:], r3[:, :keep], i3[:, :keep]
                nr = tr + p_re * sr - p_im * si
                ni = ti + p_re * si + p_im * sr
                if fwd:
                    r3 = jnp.concatenate([r3[:, :d], nr], axis=1)
                    i3 = jnp.concatenate([i3[:, :d], ni], axis=1)
                else:
                    r3 = jnp.concatenate([nr, r3[:, keep:]], axis=1)
                    i3 = jnp.concatenate([ni, i3[:, keep:]], axis=1)
                hr, hi = r3.reshape(rows, lanes), i3.reshape(rows, lanes)
        return hr, hi

    for pi in range(pairs):
        u0, u1 = u_ref[0, 2 * pi], u_ref[0, 2 * pi + 1]
        ub = jnp.concatenate([u0, u1], axis=1).astype(BF16)
        y0 = _dot(ub[:, :SSM_FLAT], m_ref[2 * pi]) + d_ref[2 * pi] * u0
        y1 = _dot(ub[:, SSM_FLAT:], m_ref[2 * pi + 1]) + d_ref[2 * pi + 1] * u1
        h = _dot(ub, s_ref[pi])
        state = {}
        for fwd, off in ((True, 0), (False, lanes)):
            hr, hi = h[:, off:off + lanes], h[:, 2 * lanes + off:3 * lanes + off]
            q = 0 if fwd else 2
            if has_h0:
                h0r = h0_ref[0, pi, 0:1, off:off + lanes]
                h0i = h0_ref[0, pi, 0:1, 2 * lanes + off:3 * lanes + off]
                a_re, a_im = ap_ref[pi, q, 0:1, :], ap_ref[pi, q + 1, 0:1, :]
                first = jnp.logical_not(has_upstream(1, fwd))
                hr = hr + jnp.where(first, a_re * h0r - a_im * h0i, 0.0)
                hi = hi + jnp.where(first, a_re * h0i + a_im * h0r, 0.0)
            else:
                h0r = h0i = 0.0
            hr, hi = scan(hr, hi, pi, fwd)
            ok = has_upstream(1, fwd)
            state[fwd] = (hr, hi, jnp.where(ok, upstream(hr, 1, fwd), h0r), jnp.where(ok, upstream(hi, 1, fwd), h0i))
        if emit_state:
            e_ref[0, pi] = jnp.concatenate([state[True][0], state[False][0], state[True][1], state[False][1]], axis=1)
        h_in = jnp.concatenate([state[True][2], state[False][2], state[True][3], state[False][3]], axis=1)
        y_state = _dot(h_in.astype(BF16), r_ref[pi])
        y_ref[0, 2 * pi] = y0 + y_state[:, :SSM_FLAT]
        y_ref[0, 2 * pi + 1] = y1 + y_state[:, SSM_FLAT:]


def _ssm_core(u_flat, pair_ops, h0, seq_chunks, emit_state):
    nb, _, rows, _ = u_flat.shape
    m_op, s_pair, r_pair, a_pair, d_row = pair_ops
    n_steps = a_pair.shape[2]
    assert (1 << n_steps) == seq_chunks
    pt = 4
    blk = pl.BlockSpec((1, 2 * pt, rows, SSM_FLAT), lambda b, g: (b, g, 0, 0))
    lead = lambda a, n: pl.BlockSpec((n,) + a.shape[1:], lambda b, g: (g,) + (0,) * (a.ndim - 1))
    in_specs = [blk, lead(m_op, 2 * pt), lead(s_pair, pt), lead(r_pair, pt), lead(a_pair, pt), lead(d_row, 2 * pt)]
    args = [u_flat, m_op, s_pair, r_pair, a_pair, d_row]
    if h0 is not None:
        in_specs.append(pl.BlockSpec((1, pt, 1, 8 * STATE), lambda b, g: (b, g, 0, 0)))
        args.append(h0)
    out_shape = [jax.ShapeDtypeStruct(u_flat.shape, F32)]
    out_specs = [blk]
    if emit_state:
        out_shape.append(jax.ShapeDtypeStruct((nb, N_GROUPS // 2, rows, 8 * STATE), F32))
        out_specs.append(pl.BlockSpec((1, pt, rows, 8 * STATE), lambda b, g: (b, g, 0, 0)))
    kern = functools.partial(_ssm_core_kernel, seq_chunks=seq_chunks, n_steps=n_steps, pairs=pt,
                             has_h0=h0 is not None, emit_state=emit_state)
    return pl.pallas_call(
        kern, out_shape=tuple(out_shape), grid=(nb, N_GROUPS // (2 * pt)),
        in_specs=in_specs, out_specs=tuple(out_specs),
        compiler_params=_params(2), name="ssm_core",
    )(*args)


def _ssm_out_kernel(y_ref, x_ref, mod_ref, ln_ref, wg_ref, wo_ref, out_ref, y_scr):
    i = pl.program_id(0)
    n_chunks = x_ref.shape[0] // SSM_CHUNK

    @pl.when(i == 0)
    def _():
        y_scr[1] = jnp.zeros(y_scr.shape[1:], F32)

    def step(fill_slot):
        fill, ready = y_scr.at[fill_slot], y_scr.at[1 - fill_slot]
        for c in range(LANE_TILES):
            for half in range(SSM_CHUNK // LANE_BLOCKS):
                per_group = [y_ref[0, c * LANE_BLOCKS + gl, :, half * 128:(half + 1) * 128]
                             for gl in range(LANE_BLOCKS)]
                per_token = _block_transpose(per_group)
                for s in range(LANE_BLOCKS):
                    fill[c, pl.ds(half * LANE_BLOCKS + s, n_chunks, stride=SSM_CHUNK), :] = per_token[s]
        y = jnp.concatenate([ready[c] for c in range(LANE_TILES)], axis=1)
        vg = _dot(_gelu_tanh(y).astype(BF16), wg_ref[...])
        glu = (vg[:, :D_MODEL] * jax.nn.sigmoid(vg[:, D_MODEL:])).astype(BF16)
        z = DN_ALPHA * x_ref[...] + mod_ref[0, 2:3, :] * _dot(glu, wo_ref[...])
        out_ref[...] = _layer_norm(z, ln_ref[0:1, :], ln_ref[1:2, :])

    for parity in range(2):
        pl.when(i % 2 == parity)(functools.partial(step, parity))


def _ssm_out(y_flat, x2, mods, mod_base, rows_per_mod, ln, w_glu, w_out):
    r = x2.shape[0]
    nb = y_flat.shape[0]
    t = r // nb
    tm = _row_tile(math.gcd(rows_per_mod, t), 512)
    per_mod = rows_per_mod // tm
    per_seq = t // tm
    nj = tm // SSM_CHUNK
    n_tiles = r // tm
    fill_tile = lambda i: jnp.minimum(i, n_tiles - 1)
    done_tile = lambda i: jnp.maximum(i - 1, 0)
    row = pl.BlockSpec((tm, D_MODEL), lambda i: (done_tile(i), 0))
    return pl.pallas_call(
        _ssm_out_kernel,
        out_shape=jax.ShapeDtypeStruct((r, D_MODEL), F32),
        grid=(n_tiles + 1,),
        in_specs=[pl.BlockSpec((1, N_GROUPS, nj, SSM_FLAT),
                               lambda i: (fill_tile(i) // per_seq, 0, fill_tile(i) % per_seq, 0)),
                  row,
                  pl.BlockSpec((1, N_MODS, D_MODEL), lambda i: (mod_base + done_tile(i) // per_mod, 0, 0)),
                  _const_spec(ln.shape), _const_spec(w_glu.shape), _const_spec(w_out.shape)],
        out_specs=row,
        scratch_shapes=[pltpu.VMEM((2, LANE_TILES, tm, 128), F32)],
        compiler_params=_params(1),
        name="ssm_out",
    )(y_flat, x2, mods, ln, w_glu, w_out)


def _ssm_mixer(x2, nb, seq_chunks, mods, mod_base, rows_per_mod, ln, w_in, ops, w_glu, w_out, h0, emit_state):
    u_flat = _ssm_in(x2, mods, mod_base, rows_per_mod, w_in, nb)
    outs = _ssm_core(u_flat, ops, h0, seq_chunks, emit_state)
    x_new = _ssm_out(outs[0], x2, mods, mod_base, rows_per_mod, ln, w_glu, w_out)
    return x_new, (outs[1] if emit_state else None)


def _rope_tables(t):
    n_rows = t // GRID_W
    row = jnp.repeat(jnp.arange(n_rows, dtype=F32), GRID_W)
    col = (jnp.arange(t) % GRID_W).astype(F32)
    nfreq = HEAD_DIM // 4
    inv = jnp.power(ROPE_BASE, -jnp.arange(nfreq, dtype=F32) / nfreq)
    ang_r, ang_c = row[:, None] * inv, col[:, None] * inv
    cos = jnp.concatenate([jnp.cos(ang_r)] * 2 + [jnp.cos(ang_c)] * 2, -1)
    sin = jnp.concatenate([-jnp.sin(ang_r), jnp.sin(ang_r), -jnp.sin(ang_c), jnp.sin(ang_c)], -1)
    return jnp.tile(cos, (1, 2)), jnp.tile(sin, (1, 2))


def kernel(x_prompt, x_sample, cache_k, cache_v, state_ssm_re, state_ssm_im, c, c_ctx, w_ada, b_ada, ln_g, ln_b, w_qkv, w_o, attn_sink, ssm_w_in, ssm_lam_re, ssm_lam_im, ssm_log_dt, ssm_b_re, ssm_b_im, ssm_c_re, ssm_c_im, ssm_d, ssm_w_glu, ssm_w_out, ffn_w1, ffn_w3, ffn_w2):
    nc, tc, _ = x_prompt.shape
    nl, tl, _ = x_sample.shape
    n_attn = w_qkv.shape[0]
    n_ssm = ssm_w_in.shape[0]

    n_mod_rows = -(-(1 + nl) // 8) * 8
    cvec = jnp.zeros((n_mod_rows, D_MODEL), F32).at[0].set(c_ctx).at[1:1 + nl].set(c)
    mods_all = _ada_mods(cvec, w_ada, b_ada)
    ln_all = jnp.stack([ln_g, ln_b], axis=2)

    bf = lambda w: w.astype(BF16)
    w_v = w_qkv[:, :, Q_WIDTH + KV_WIDTH:].reshape(n_attn, D_MODEL, N_KV_HEADS, HEAD_DIM)
    w_v_ext = jnp.pad(w_v, ((0, 0), (0, 0), (0, 0), (0, 128 - HEAD_DIM))).reshape(n_attn, D_MODEL, V_EXT)
    w_qkv_ctx = bf(jnp.concatenate([w_qkv, w_v_ext], axis=-1))
    w_qkv_lat = bf(jnp.concatenate([w_qkv[:, :, :Q_WIDTH + KV_WIDTH], w_v_ext], axis=-1))
    w_o_b = bf(w_o)
    w_in_b, w_glu_b, w_out_b = bf(ssm_w_in), bf(ssm_w_glu), bf(ssm_w_out)
    w1_b, w3_b, w2_b = bf(ffn_w1), bf(ffn_w3), bf(ffn_w2)
    cos, sin = _rope_tables(tl)
    kx = bf(cache_k).reshape(nl, n_attn, -1, KV_WIDTH)
    vx = jnp.concatenate([bf(cache_v), jnp.ones(cache_v.shape[:-1] + (128 - HEAD_DIM,), BF16)], -1)
    vx = vx.reshape(nl, n_attn, -1, V_EXT)

    steps_c = (tc // SSM_CHUNK).bit_length() - 1
    steps_l = (tl // SSM_CHUNK).bit_length() - 1
    ssm_ops = []
    for s in range(n_ssm):
        prm = (ssm_lam_re[s], ssm_lam_im[s], ssm_log_dt[s], ssm_b_re[s], ssm_b_im[s],
               ssm_c_re[s], ssm_c_im[s], ssm_d[s])
        ssm_ops.append(_s5_operators(*prm, max(steps_c, steps_l)))

    def ops_for(s, n_steps):
        m_op, s_pair, r_pair, a_pair, d_row = ssm_ops[s]
        return m_op, s_pair, r_pair, a_pair[:, :, :n_steps], d_row

    xc = x_prompt.reshape(nc * tc, D_MODEL)
    xl = x_sample.reshape(nl * tl, D_MODEL)
    new_k, new_v, new_sr, new_si = [], [], [], []
    for l in range(DEPTH):
        mods = mods_all[l]
        ln1, ln2 = ln_all[l, 0], ln_all[l, 1]
        if l % 2 == 0:
            a = l // 2
            q, kb, vb, k32, v32 = _qkv_ctx(xc, mods, w_qkv_ctx[a])
            new_k.append(k32.reshape(nc, tc, N_KV_HEADS, HEAD_DIM))
            new_v.append(v32.reshape(nc, tc, N_KV_HEADS, HEAD_DIM))
            xc = _attn_ctx(attn_sink[a], q, kb, vb, xc, mods, ln1, w_o_b[a], tc)
            qp, qr, kr, v = _qkv_lat(xl, mods, w_qkv_lat[a], cos, sin, tl)
            xl = _attn_lat(attn_sink[a], qp, qr, kr, v, kx[:, a], vx[:, a], xl, mods, ln1, w_o_b[a], nl, tl)
        else:
            s = l // 2
            xc, e_c = _ssm_mixer(xc, 1, tc // SSM_CHUNK, mods, 0, nc * tc, ln1, w_in_b[s],
                                 ops_for(s, steps_c), w_glu_b[s], w_out_b[s], None, True)
            e6 = e_c.reshape(N_GROUPS // 2, nc, tc // SSM_CHUNK, 4, 2, STATE)
            fin = jnp.stack([e6[:, :, -1, 0], e6[:, :, 0, 1], e6[:, :, -1, 2], e6[:, :, 0, 3]], 0)
            fin = fin.transpose(0, 2, 1, 3, 4).reshape(4, nc, N_GROUPS, STATE)
            new_sr.append(jnp.stack([fin[0], fin[1]], 1))
            new_si.append(jnp.stack([fin[2], fin[3]], 1))
            h0 = jnp.stack([state_ssm_re[:, s, 0], state_ssm_re[:, s, 1],
                            state_ssm_im[:, s, 0], state_ssm_im[:, s, 1]], 1)
            h0 = h0.reshape(nl, 4, N_GROUPS // 2, 2 * STATE).transpose(0, 2, 1, 3)
            h0 = h0.reshape(nl, N_GROUPS // 2, 8 * STATE)
            xl, _ = _ssm_mixer(xl, nl, tl // SSM_CHUNK, mods, 1, tl, ln1, w_in_b[s],
                               ops_for(s, steps_l), w_glu_b[s], w_out_b[s], h0[:, :, None, :], False)
        xc = _ffn(xc, mods, 0, nc * tc, ln2, w1_b[l], w3_b[l], w2_b[l])
        xl = _ffn(xl, mods, 1, tl, ln2, w1_b[l], w3_b[l], w2_b[l])

    return (xc.reshape(nc, tc, D_MODEL), xl.reshape(nl, tl, D_MODEL),
            jnp.stack(new_k, axis=1), jnp.stack(new_v, axis=1),
            jnp.stack(new_sr, axis=1), jnp.stack(new_si, axis=1))
```

```python
import functools
import math

import jax
import jax.numpy as jnp
from jax import lax
from jax.experimental import pallas as pl
from jax.experimental.pallas import tpu as pltpu

F32 = jnp.float32
BF16 = jnp.bfloat16

D_MODEL = 1024
DEPTH = 4
N_HEADS = 16
N_KV_HEADS = 4
HEAD_DIM = 64
HEADS_PER_KV = N_HEADS // N_KV_HEADS
Q_WIDTH = N_HEADS * HEAD_DIM
KV_WIDTH = N_KV_HEADS * HEAD_DIM
GRID_W = 64
ATT_BLOCK = 128
ROPE_BASE = 10000.0
SSM_GROUP = 16
N_GROUPS = D_MODEL // SSM_GROUP
STATE = 64
SSM_CHUNK = 16
SSM_FLAT = SSM_CHUNK * SSM_GROUP
D_FF = -(-8 * D_MODEL // (3 * 256)) * 256
FF_CHUNK = 256
FFN_ROWS = 1024
FFN_SPLIT = 2
DN_ALPHA = (2 * DEPTH) ** 0.25
LN_EPS = 1e-5
NEG_INF = -1e30
N_MODS = 6
VMEM_LIMIT = 56 * 1024 * 1024
HIGHEST = lax.Precision.HIGHEST


def _params(n_axes):
    return pltpu.CompilerParams(dimension_semantics=("arbitrary",) * n_axes,
                                vmem_limit_bytes=VMEM_LIMIT)


def _layer_spec(param, single=False):
    stacked, layer = param
    rest = stacked.shape[1:]
    mode = dict(pipeline_mode=pl.Buffered(1)) if single else {}
    return pl.BlockSpec((None,) + rest, lambda *_: (layer,) + (0,) * len(rest), **mode)


def _mods_spec(mods, row_of):
    return pl.BlockSpec((None, 1, N_MODS, D_MODEL), lambda *ids: (mods[1], row_of(*ids), 0, 0))


def _ln_spec(ln):
    return pl.BlockSpec((None, None, 2, D_MODEL), lambda *_: (ln[1], ln[2], 0, 0))


def _layer_norm(y, g, b):
    mu = jnp.mean(y, -1, keepdims=True)
    d = y - mu
    var = jnp.mean(d * d, -1, keepdims=True)
    return d * lax.rsqrt(var + LN_EPS) * g + b


def _silu(x):
    return x * jax.nn.sigmoid(x)


def _gelu_tanh(x):
    return 0.5 * x * (1.0 + jnp.tanh(math.sqrt(2.0 / math.pi) * (x + 0.044715 * (x * x * x))))


def _dot(a, b):
    return jnp.dot(a, b, preferred_element_type=F32)


def _dot_nt(a, b):
    return lax.dot_general(a, b, (((1,), (1,)), ((), ())), preferred_element_type=F32)


def _row_tile(rows_per_mod, want):
    tm = min(want, rows_per_mod)
    assert rows_per_mod % tm == 0
    return tm


def _mods_kernel(c_ref, w_ref, b_ref, o_ref):
    a = _silu(c_ref[...]).astype(BF16)
    o_ref[0] = _dot(a, w_ref[0].astype(BF16)) + b_ref[0]


def _ada_mods(cvec, w_ada, b_ada):
    r = cvec.shape[0]
    tn = 1536
    out = pl.pallas_call(
        _mods_kernel,
        out_shape=jax.ShapeDtypeStruct((DEPTH, r, N_MODS * D_MODEL), F32),
        grid=(DEPTH, N_MODS * D_MODEL // tn),
        in_specs=[pl.BlockSpec((r, D_MODEL), lambda l, j: (0, 0)),
                  pl.BlockSpec((1, D_MODEL, tn), lambda l, j: (l, 0, j)),
                  pl.BlockSpec((1, 1, tn), lambda l, j: (l, 0, j))],
        out_specs=pl.BlockSpec((1, r, tn), lambda l, j: (l, 0, j)),
        compiler_params=_params(2),
        name="ada_mods",
    )(cvec, w_ada, b_ada.reshape(DEPTH, 1, N_MODS * D_MODEL))
    return out.reshape(DEPTH, r, N_MODS, D_MODEL)


Q_SCALE = HEAD_DIM ** -0.5 * math.log2(math.e)
V_EXT = N_KV_HEADS * 128


def _rope(x, cos, sin, first_half):
    outs = []
    for c in range(x.shape[1] // 128):
        xc = x[:, c * 128:(c + 1) * 128]
        partner = jnp.where(first_half, pltpu.roll(xc, 128 - 16, 1), pltpu.roll(xc, 16, 1))
        outs.append(xc * cos + partner * sin)
    return jnp.concatenate(outs, axis=1)


def _ones_lanes(width):
    return jnp.where((lax.broadcasted_iota(jnp.int32, (1, width), 1) % 128) >= HEAD_DIM, 1.0, 0.0)


def _qkv_ctx_kernel(x_ref, mod_ref, w_ref, q_ref, kb_ref, vb_ref, k_ref, v_ref):
    h = (x_ref[...] * (1.0 + mod_ref[0, 1:2, :]) + mod_ref[0, 0:1, :]).astype(BF16)
    qkv = _dot(h, w_ref[...])
    q_ref[...] = (qkv[:, :Q_WIDTH] * Q_SCALE).astype(BF16)
    k = qkv[:, Q_WIDTH:Q_WIDTH + KV_WIDTH]
    k_ref[...] = k
    v_ref[...] = qkv[:, Q_WIDTH + KV_WIDTH:Q_WIDTH + 2 * KV_WIDTH]
    kb_ref[...] = k.astype(BF16)
    vb_ref[...] = (qkv[:, Q_WIDTH + 2 * KV_WIDTH:] + _ones_lanes(V_EXT)).astype(BF16)


def _qkv_lat_kernel(x_ref, mod_ref, w_ref, cos_ref, sin_ref, qp_ref, qr_ref, kr_ref, v_ref):
    h = (x_ref[...] * (1.0 + mod_ref[0, 1:2, :]) + mod_ref[0, 0:1, :]).astype(BF16)
    qkv = _dot(h, w_ref[...])
    cos = cos_ref[...]
    sin = sin_ref[...]
    first_half = (lax.broadcasted_iota(jnp.int32, cos.shape, 1) & 16) == 0
    q = qkv[:, :Q_WIDTH] * Q_SCALE
    qp_ref[...] = q.astype(BF16)
    qr_ref[...] = _rope(q, cos, sin, first_half).astype(BF16)
    kr_ref[...] = _rope(qkv[:, Q_WIDTH:Q_WIDTH + KV_WIDTH], cos, sin, first_half).astype(BF16)
    v_ref[...] = (qkv[:, Q_WIDTH + KV_WIDTH:] + _ones_lanes(V_EXT)).astype(BF16)


def _qkv_ctx(x2, mods, w_ext):
    r = x2.shape[0]
    tm = _row_tile(r, 512)
    row = lambda w: pl.BlockSpec((tm, w), lambda i: (i, 0))
    return pl.pallas_call(
        _qkv_ctx_kernel,
        out_shape=(jax.ShapeDtypeStruct((r, Q_WIDTH), BF16),
                   jax.ShapeDtypeStruct((r, KV_WIDTH), BF16),
                   jax.ShapeDtypeStruct((r, V_EXT), BF16),
                   jax.ShapeDtypeStruct((r, KV_WIDTH), F32),
                   jax.ShapeDtypeStruct((r, KV_WIDTH), F32)),
        grid=(r // tm,),
        in_specs=[row(D_MODEL), _mods_spec(mods, lambda i: 0), _layer_spec(w_ext)],
        out_specs=(row(Q_WIDTH), row(KV_WIDTH), row(V_EXT), row(KV_WIDTH), row(KV_WIDTH)),
        compiler_params=_params(1),
        name="qkv_ctx",
    )(x2, mods[0], w_ext[0])


def _qkv_lat(x2, mods, w_ext, cos, sin, t):
    r = x2.shape[0]
    tm = _row_tile(t, 512)
    per_seq = t // tm
    row = lambda w: pl.BlockSpec((tm, w), lambda i: (i, 0))
    tab = pl.BlockSpec((tm, 128), lambda i: (i % per_seq, 0))
    return pl.pallas_call(
        _qkv_lat_kernel,
        out_shape=(jax.ShapeDtypeStruct((r, Q_WIDTH), BF16),
                   jax.ShapeDtypeStruct((r, Q_WIDTH), BF16),
                   jax.ShapeDtypeStruct((r, KV_WIDTH), BF16),
                   jax.ShapeDtypeStruct((r, V_EXT), BF16)),
        grid=(r // tm,),
        in_specs=[row(D_MODEL), _mods_spec(mods, lambda i: 1 + i // per_seq), _layer_spec(w_ext), tab, tab],
        out_specs=(row(Q_WIDTH), row(Q_WIDTH), row(KV_WIDTH), row(V_EXT)),
        compiler_params=_params(1),
        name="qkv_lat",
    )(x2, mods[0], w_ext[0], cos, sin)


def _attn_epilogue(o_scr, x_ref, mod_ref, ln_ref, wo_ref, out_ref):
    y = _dot(o_scr[...], wo_ref[...])
    z = DN_ALPHA * x_ref[...] + mod_ref[0, 2:3, :] * y
    out_ref[...] = _layer_norm(z, ln_ref[0:1, :], ln_ref[1:2, :])


def _attn_ctx_kernel(sink_ref, q_ref, k_ref, v_ref, x_ref, mod_ref, ln_ref, wo_ref, out_ref, o_scr, s_scr, m_scr):
    rows = q_ref.shape[0]
    n_tiles = k_ref.shape[0] // 128

    def scores(kv, slot):
        k = k_ref[:, kv * HEAD_DIM:(kv + 1) * HEAD_DIM]
        for g in range(HEADS_PER_KV):
            h = kv * HEADS_PER_KV + g
            s = _dot_nt(q_ref[:, h * HEAD_DIM:(h + 1) * HEAD_DIM], k)
            m = jnp.maximum(jnp.max(s, -1, keepdims=True), sink_ref[h] * math.log2(math.e))
            s_scr[slot, g * rows:(g + 1) * rows, :] = s
            m_scr[slot, g * rows:(g + 1) * rows, :] = jnp.broadcast_to(m, (rows, 128))

    def values(kv, slot):
        v = v_ref[:, kv * 128:(kv + 1) * 128]
        for g in range(HEADS_PER_KV):
            h = kv * HEADS_PER_KV + g
            r0, r1 = g * rows, (g + 1) * rows
            m = m_scr[slot, r0:r1, :]
            p = [jnp.exp2(s_scr[slot, r0:r1, c * 128:(c + 1) * 128] - m).astype(BF16) for c in range(n_tiles)]
            o_ext = _dot(jnp.concatenate(p, axis=1), v)
            den = pltpu.roll(o_ext, HEAD_DIM, 1) + jnp.exp2(sink_ref[h] * math.log2(math.e) - m)
            o_scr[:, h * HEAD_DIM:(h + 1) * HEAD_DIM] = (o_ext / den)[:, :HEAD_DIM].astype(BF16)

    scores(0, 0)
    for kv in range(N_KV_HEADS):
        if kv + 1 < N_KV_HEADS:
            scores(kv + 1, (kv + 1) % 2)
        values(kv, kv % 2)
    _attn_epilogue(o_scr, x_ref, mod_ref, ln_ref, wo_ref, out_ref)


def _attn_ctx(sink, q, k, v_ext, x2, mods, ln, w_o, seq):
    r = x2.shape[0]
    row = lambda w: pl.BlockSpec((seq, w), lambda i: (i, 0))
    return pl.pallas_call(
        _attn_ctx_kernel,
        out_shape=jax.ShapeDtypeStruct((r, D_MODEL), F32),
        grid=(r // seq,),
        in_specs=[pl.BlockSpec(memory_space=pltpu.SMEM),
                  row(Q_WIDTH), row(KV_WIDTH), row(V_EXT), row(D_MODEL),
                  _mods_spec(mods, lambda i: 0), _ln_spec(ln), _layer_spec(w_o)],
        out_specs=row(D_MODEL),
        scratch_shapes=[pltpu.VMEM((seq, Q_WIDTH), BF16),
                        pltpu.VMEM((2, HEADS_PER_KV * seq, seq), F32),
                        pltpu.VMEM((2, HEADS_PER_KV * seq, 128), F32)],
        compiler_params=_params(1),
        name="attn_ctx",
    )(sink, q, k, v_ext, x2, mods[0], ln[0], w_o[0])


LOCAL_KEYS = 3 * ATT_BLOCK
ATT_STEP_BLOCKS = 2


def _attn_lat_kernel(sink_ref, qp_ref, qr_ref, k_ref, v_ref, kx_ref, vx_ref, x_ref, mod_ref, ln_ref,
                     wo_ref, out_ref, o_scr, s_scr, m_scr, bias_scr):
    rows = ATT_BLOCK
    seq = k_ref.shape[0]
    n_ctx = kx_ref.shape[1]
    n_tiles = (n_ctx + LOCAL_KEYS) // 128
    n_sub = qp_ref.shape[0] // rows
    r_idx = lax.broadcasted_iota(jnp.int32, (rows, LOCAL_KEYS), 0)
    c_idx = lax.broadcasted_iota(jnp.int32, (rows, LOCAL_KEYS), 1)
    starts = []
    for sub in range(n_sub):
        j = pl.program_id(1) * n_sub + sub
        start = pl.multiple_of(jnp.clip((j - 1) * rows, 0, seq - LOCAL_KEYS), rows)
        bias_scr[sub] = jnp.where(jnp.abs(r_idx - c_idx + (j * rows - start)) <= ATT_BLOCK, 0.0, NEG_INF)
        starts.append(start)

    def scores(sub, kv, slot):
        lo, hi = kv * HEAD_DIM, (kv + 1) * HEAD_DIM
        q0 = sub * rows
        kx = kx_ref[0, :, lo:hi]
        kl = k_ref[pl.ds(starts[sub], LOCAL_KEYS), lo:hi]
        for g in range(HEADS_PER_KV):
            h = kv * HEADS_PER_KV + g
            r0, r1 = g * rows, (g + 1) * rows
            s_x = _dot_nt(qp_ref[q0:q0 + rows, h * HEAD_DIM:(h + 1) * HEAD_DIM], kx)
            s_l = _dot_nt(qr_ref[q0:q0 + rows, h * HEAD_DIM:(h + 1) * HEAD_DIM], kl) + bias_scr[sub]
            m = jnp.maximum(jnp.maximum(jnp.max(s_x, -1, keepdims=True), jnp.max(s_l, -1, keepdims=True)),
                            sink_ref[h] * math.log2(math.e))
            s_scr[slot, r0:r1, :n_ctx] = s_x
            s_scr[slot, r0:r1, n_ctx:] = s_l
            m_scr[slot, r0:r1, :] = jnp.broadcast_to(m, (rows, 128))

    def values(sub, kv, slot):
        q0 = sub * rows
        vx = vx_ref[0, :, kv * 128:(kv + 1) * 128]
        vl = v_ref[pl.ds(starts[sub], LOCAL_KEYS), kv * 128:(kv + 1) * 128]
        for g in range(HEADS_PER_KV):
            h = kv * HEADS_PER_KV + g
            r0, r1 = g * rows, (g + 1) * rows
            m = m_scr[slot, r0:r1, :]
            p = [jnp.exp2(s_scr[slot, r0:r1, c * 128:(c + 1) * 128] - m).astype(BF16) for c in range(n_tiles)]
            o_ext = (_dot(jnp.concatenate(p[:n_ctx // 128], axis=1), vx)
                     + _dot(jnp.concatenate(p[n_ctx // 128:], axis=1), vl))
            den = pltpu.roll(o_ext, HEAD_DIM, 1) + jnp.exp2(sink_ref[h] * math.log2(math.e) - m)
            o_scr[q0:q0 + rows, h * HEAD_DIM:(h + 1) * HEAD_DIM] = (o_ext / den)[:, :HEAD_DIM].astype(BF16)

    units = [(sub, kv) for sub in range(n_sub) for kv in range(N_KV_HEADS)]
    scores(*units[0], 0)
    for u, unit in enumerate(units):
        if u + 1 < len(units):
            scores(*units[u + 1], (u + 1) % 2)
        values(*unit, u % 2)
    _attn_epilogue(o_scr, x_ref, mod_ref, ln_ref, wo_ref, out_ref)


def _attn_lat(sink, qp, qr, kr, v_ext, k_ctx, v_ctx_ext, x2, mods, ln, w_o, n, t):
    assert t >= LOCAL_KEYS
    blk = ATT_STEP_BLOCKS * ATT_BLOCK
    nblk = t // blk
    row = lambda w: pl.BlockSpec((blk, w), lambda b, j: (b * nblk + j, 0))
    seq = lambda w: pl.BlockSpec((t, w), lambda b, j: (b, 0))
    ctx = lambda a: pl.BlockSpec((1, None) + a[0].shape[2:], lambda b, j: (b, a[1], 0, 0))
    return pl.pallas_call(
        _attn_lat_kernel,
        out_shape=jax.ShapeDtypeStruct((n * t, D_MODEL), F32),
        grid=(n, nblk),
        in_specs=[pl.BlockSpec(memory_space=pltpu.SMEM),
                  row(Q_WIDTH), row(Q_WIDTH), seq(KV_WIDTH), seq(V_EXT),
                  ctx(k_ctx), ctx(v_ctx_ext), row(D_MODEL),
                  _mods_spec(mods, lambda b, j: 1 + b), _ln_spec(ln), _layer_spec(w_o)],
        out_specs=row(D_MODEL),
        scratch_shapes=[pltpu.VMEM((blk, Q_WIDTH), BF16),
                        pltpu.VMEM((2, HEADS_PER_KV * ATT_BLOCK, k_ctx[0].shape[2] + LOCAL_KEYS), F32),
                        pltpu.VMEM((2, HEADS_PER_KV * ATT_BLOCK, 128), F32),
                        pltpu.VMEM((ATT_STEP_BLOCKS, ATT_BLOCK, LOCAL_KEYS), F32)],
        compiler_params=_params(2),
        name="attn_lat",
    )(sink, qp, qr, kr, v_ext, k_ctx[0], v_ctx_ext[0], x2, mods[0], ln[0], w_o[0])


def _ffn_kernel(x_ref, mod_ref, ln_ref, w1_ref, w3_ref, w2_ref, out_ref, acc_ref):
    half = x_ref.shape[0] // FFN_SPLIT
    for part in range(FFN_SPLIT):
        r0, r1 = part * half, (part + 1) * half
        x = x_ref[r0:r1, :]
        h = (x * (1.0 + mod_ref[0, 4:5, :]) + mod_ref[0, 3:4, :]).astype(BF16)
        for c in range(D_FF // FF_CHUNK):
            lo, hi = c * FF_CHUNK, (c + 1) * FF_CHUNK
            a = _dot(h, w1_ref[:, lo:hi])
            b = _dot(h, w3_ref[:, lo:hi])
            y = _dot((_silu(a) * b).astype(BF16), w2_ref[lo:hi, :])
            if c == 0:
                acc_ref[r0:r1, :] = y
            else:
                acc_ref[r0:r1, :] += y
        z = DN_ALPHA * x + mod_ref[0, 5:6, :] * acc_ref[r0:r1, :]
        out_ref[r0:r1, :] = _layer_norm(z, ln_ref[0:1, :], ln_ref[1:2, :])


def _ffn(x2, mods, mod_base, rows_per_mod, ln, w1, w3, w2):
    r = x2.shape[0]
    tm = _row_tile(rows_per_mod, FFN_ROWS)
    per_mod = rows_per_mod // tm
    return pl.pallas_call(
        _ffn_kernel,
        out_shape=jax.ShapeDtypeStruct((r, D_MODEL), F32),
        grid=(r // tm,),
        in_specs=[pl.BlockSpec((tm, D_MODEL), lambda i: (i, 0)),
                  _mods_spec(mods, lambda i: mod_base + i // per_mod),
                  _ln_spec(ln), _layer_spec(w1, single=True), _layer_spec(w3, single=True),
                  _layer_spec(w2, single=True)],
        out_specs=pl.BlockSpec((tm, D_MODEL), lambda i: (i, 0)),
        scratch_shapes=[pltpu.VMEM((tm, D_MODEL), F32)],
        compiler_params=_params(1),
        name="ffn",
    )(x2, mods[0], ln[0], w1[0], w3[0], w2[0])


LANE_BLOCKS = 128 // SSM_GROUP
LANE_TILES = D_MODEL // 128


def _block_transpose(xs):
    xs = list(xs)
    blk = lax.broadcasted_iota(jnp.int32, xs[0].shape, 1) // SSM_GROUP
    k = LANE_BLOCKS // 2
    while k >= 1:
        keep_lo = (blk & k) == 0
        for i in range(LANE_BLOCKS):
            if i & k:
                continue
            lo, hi = xs[i], xs[i + k]
            xs[i] = jnp.where(keep_lo, lo, pltpu.roll(hi, SSM_GROUP * k, 1))
            xs[i + k] = jnp.where(keep_lo, pltpu.roll(lo, 128 - SSM_GROUP * k, 1), hi)
        k //= 2
    return xs


def _ssm_in_kernel(x_ref, mod_ref, w_ref, o_ref, u_scr):
    h = (x_ref[...] * (1.0 + mod_ref[0, 1:2, :]) + mod_ref[0, 0:1, :]).astype(BF16)
    u = _dot(h, w_ref[...])
    for c in range(LANE_TILES):
        u_scr[c] = u[:, c * 128:(c + 1) * 128]
    n_chunks = x_ref.shape[0] // SSM_CHUNK
    for c in range(LANE_TILES):
        for half in range(SSM_CHUNK // LANE_BLOCKS):
            per_token = [u_scr[c, pl.ds(half * LANE_BLOCKS + s, n_chunks, stride=SSM_CHUNK), :]
                         for s in range(LANE_BLOCKS)]
            per_group = _block_transpose(per_token)
            for gl in range(LANE_BLOCKS):
                o_ref[0, c * LANE_BLOCKS + gl, :, half * 128:(half + 1) * 128] = per_group[gl]


def _ssm_in(x2, mods, mod_base, rows_per_mod, w, nb):
    r = x2.shape[0]
    t = r // nb
    tm = _row_tile(math.gcd(rows_per_mod, t), 512)
    per_mod = rows_per_mod // tm
    per_seq = t // tm
    nj = tm // SSM_CHUNK
    return pl.pallas_call(
        _ssm_in_kernel,
        out_shape=jax.ShapeDtypeStruct((nb, N_GROUPS, t // SSM_CHUNK, SSM_FLAT), F32),
        grid=(r // tm,),
        in_specs=[pl.BlockSpec((tm, D_MODEL), lambda i: (i, 0)),
                  _mods_spec(mods, lambda i: mod_base + i // per_mod), _layer_spec(w)],
        out_specs=pl.BlockSpec((1, N_GROUPS, nj, SSM_FLAT), lambda i: (i // per_seq, 0, i % per_seq, 0)),
        scratch_shapes=[pltpu.VMEM((LANE_TILES, tm, 128), F32)],
        compiler_params=_params(1),
        name="ssm_in",
    )(x2, mods[0], w[0])


def _s5_prep_kernel(logdt_ref, lam_row_ref, lam_col_ref, bt_ref, ct_ref,
                    m_ref, s_ref, r_ref, ap_ref, *, n_steps):
    pair = pl.program_id(0)
    lc = SSM_CHUNK
    lanes = 2 * STATE

    def transition(lam_re, lam_im, d, axis):
        first = lax.broadcasted_iota(jnp.int32, lam_re.shape, axis) < STATE
        log_dt = jnp.where(first, jnp.full(lam_re.shape, logdt_ref[d, 2 * pair], F32),
                           jnp.full(lam_re.shape, logdt_ref[d, 2 * pair + 1], F32))
        dt = jnp.exp(log_dt)
        mag = jnp.exp(lam_re * dt)
        return mag * jnp.cos(lam_im * dt), mag * jnp.sin(lam_im * dt)

    def powers(a_re, a_im, n):
        pr, pi = [jnp.ones_like(a_re)], [jnp.zeros_like(a_im)]
        for _ in range(n):
            r, i = pr[-1], pi[-1]
            pr.append(r * a_re - i * a_im)
            pi.append(r * a_im + i * a_re)
        return pr, pi

    dot_hi = lambda a, b: jnp.dot(a, b, precision=HIGHEST, preferred_element_type=F32)
    row_shape = (SSM_GROUP, lanes)
    x_group = lax.broadcasted_iota(jnp.int32, (SSM_FLAT, lanes), 1) // STATE
    w_group = lax.broadcasted_iota(jnp.int32, (lanes, SSM_FLAT), 0) // STATE
    t_blk = lax.broadcasted_iota(jnp.int32, (SSM_FLAT, SSM_FLAT), 1) // SSM_GROUP
    t_blk_w = lax.broadcasted_iota(jnp.int32, (lanes, SSM_FLAT), 1) // SSM_GROUP
    m_acc = [jnp.zeros((SSM_FLAT, SSM_FLAT), F32) for _ in range(2)]
    spread = jnp.where(lax.broadcasted_iota(jnp.int32, (SSM_GROUP, SSM_FLAT), 0)
                       == lax.broadcasted_iota(jnp.int32, (SSM_GROUP, SSM_FLAT), 1) % SSM_GROUP, 1.0, 0.0)

    for d in range(2):
        fwd = d == 0
        lam_re = jnp.broadcast_to(lam_row_ref[0, 2 * d:2 * d + 1, :], row_shape)
        lam_im = jnp.broadcast_to(lam_row_ref[0, 2 * d + 1:2 * d + 2, :], row_shape)
        a_re, a_im = transition(lam_re, lam_im, d, 1)
        den = lam_re * lam_re + lam_im * lam_im
        nr, ni = a_re - 1.0, a_im
        coef_re = (nr * lam_re + ni * lam_im) / den
        coef_im = (ni * lam_re - nr * lam_im) / den
        bt_re, bt_im = bt_ref[0, 2 * d], bt_ref[0, 2 * d + 1]
        bb_re = coef_re * bt_re - coef_im * bt_im
        bb_im = coef_re * bt_im + coef_im * bt_re
        pr, pi = powers(a_re, a_im, lc)
        order = [lc - 1 - s for s in range(lc)] if fwd else list(range(lc))
        x_re = jnp.concatenate([pr[k] * bb_re - pi[k] * bb_im for k in order], axis=0)
        x_im = jnp.concatenate([pr[k] * bb_im + pi[k] * bb_re for k in order], axis=0)
        sq_re, sq_im = pr[lc], pi[lc]
        for i in range(n_steps):
            ap_ref[0, 2 * d, i:i + 1, :] = sq_re[0:1, :]
            ap_ref[0, 2 * d + 1, i:i + 1, :] = sq_im[0:1, :]
            sq_re, sq_im = sq_re * sq_re - sq_im * sq_im, 2.0 * sq_re * sq_im
        ct_re = lax.dot_general(ct_ref[0, 2 * d], spread, (((0,), (0,)), ((), ())), precision=HIGHEST,
                                preferred_element_type=F32)
        ct_im = lax.dot_general(ct_ref[0, 2 * d + 1], spread, (((0,), (0,)), ((), ())), precision=HIGHEST,
                                preferred_element_type=F32)
        for gl in range(2):
            xg_re = jnp.where(x_group == gl, x_re, 0.0)
            xg_im = jnp.where(x_group == gl, x_im, 0.0)
            r0 = gl * SSM_FLAT
            s_ref[0, r0:r0 + SSM_FLAT, d * lanes:(d + 1) * lanes] = xg_re.astype(BF16)
            s_ref[0, r0:r0 + SSM_FLAT, (2 + d) * lanes:(3 + d) * lanes] = xg_im.astype(BF16)
            kw = dot_hi(xg_re, ct_re) - dot_hi(xg_im, ct_im)
            for t in range(lc):
                if fwd:
                    sh = (lc - 1 - t) * SSM_GROUP
                    moved = kw if sh == 0 else jnp.concatenate([kw[sh:], jnp.zeros((sh, SSM_FLAT), F32)], axis=0)
                else:
                    sh = t * SSM_GROUP
                    moved = kw if sh == 0 else jnp.concatenate(
                        [jnp.zeros((sh, SSM_FLAT), F32), kw[:SSM_FLAT - sh]], axis=0)
                m_acc[gl] = m_acc[gl] + jnp.where(t_blk == t, moved, 0.0)

        lam_re_c = jnp.broadcast_to(lam_col_ref[0, :, 2 * d:2 * d + 1], (lanes, 128))
        lam_im_c = jnp.broadcast_to(lam_col_ref[0, :, 2 * d + 1:2 * d + 2], (lanes, 128))
        ac_re, ac_im = transition(lam_re_c, lam_im_c, d, 0)
        pcr, pci = powers(ac_re, ac_im, lc)
        pcr = [jnp.concatenate([p, p], axis=1) for p in pcr]
        pci = [jnp.concatenate([p, p], axis=1) for p in pci]
        pw_re = jnp.zeros_like(ct_re)
        pw_im = jnp.zeros_like(ct_re)
        for t in range(lc):
            k = t + 1 if fwd else lc - t
            pw_re = jnp.where(t_blk_w == t, pcr[k], pw_re)
            pw_im = jnp.where(t_blk_w == t, pci[k], pw_im)
        w_re = ct_re * pw_re - ct_im * pw_im
        w_im = ct_re * pw_im + ct_im * pw_re
        for gl in range(2):
            c0 = gl * SSM_FLAT
            r_ref[0, d * lanes:(d + 1) * lanes, c0:c0 + SSM_FLAT] = jnp.where(w_group == gl, w_re, 0.0).astype(BF16)
            r_ref[0, (2 + d) * lanes:(3 + d) * lanes, c0:c0 + SSM_FLAT] = jnp.where(w_group == gl, -w_im, 0.0).astype(BF16)

    for gl in range(2):
        m_ref[gl] = m_acc[gl].astype(BF16)


def _s5_operators(lam_re, lam_im, log_dt, b_re, b_im, c_re, c_im, d_skip, n_steps):
    pairs = N_GROUPS // 2
    quantities = lambda re, im: jnp.stack([re[0], im[0], re[1], im[1]], 1)
    lam_row = quantities(lam_re.reshape(2, pairs, 2 * STATE), lam_im.reshape(2, pairs, 2 * STATE))
    lam_col = lam_row.transpose(0, 2, 1)
    bt_lanes = lambda x: (x.reshape(2, pairs, 2, STATE, SSM_GROUP).transpose(0, 1, 4, 2, 3)
                          .reshape(2, pairs, SSM_GROUP, 2 * STATE))
    bt = quantities(bt_lanes(b_re), bt_lanes(b_im))
    c_lanes = lambda x: (x.reshape(2, pairs, 2, SSM_GROUP, STATE).transpose(0, 1, 3, 2, 4)
                         .reshape(2, pairs, SSM_GROUP, 2 * STATE))
    ct = quantities(c_lanes(c_re), c_lanes(c_im))
    lead = lambda a, n: pl.BlockSpec((n,) + a.shape[1:], lambda g: (g,) + (0,) * (len(a.shape) - 1))
    out_m = jax.ShapeDtypeStruct((N_GROUPS, SSM_FLAT, SSM_FLAT), BF16)
    out_sq = jax.ShapeDtypeStruct((pairs, 2 * SSM_FLAT, 2 * SSM_FLAT), BF16)
    out_ap = jax.ShapeDtypeStruct((pairs, 4, n_steps, 2 * STATE), F32)
    m_op, s_pair, r_pair, a_pair = pl.pallas_call(
        functools.partial(_s5_prep_kernel, n_steps=n_steps),
        out_shape=(out_m, out_sq, out_sq, out_ap),
        grid=(pairs,),
        in_specs=[pl.BlockSpec(memory_space=pltpu.SMEM), lead(lam_row, 1), lead(lam_col, 1), lead(bt, 1), lead(ct, 1)],
        out_specs=(lead(out_m, 2), lead(out_sq, 1), lead(out_sq, 1), lead(out_ap, 1)),
        compiler_params=_params(1),
        name="s5_prep",
    )(log_dt, lam_row, lam_col, bt, ct)
    d_row = jnp.tile(d_skip.reshape(N_GROUPS, 1, SSM_GROUP), (1, 1, SSM_CHUNK))
    return m_op, s_pair, r_pair, a_pair, d_row


def _ssm_core_kernel(*refs, seq_chunks, n_steps, pairs, has_h0, emit_state):
    refs = list(refs)
    u_ref, m_ref, s_ref, r_ref, ap_ref, d_ref = refs[:6]
    refs = refs[6:]
    h0_ref = refs.pop(0) if has_h0 else None
    y_ref = refs.pop(0)
    if emit_state:
        e_ref, fin_scr = refs
    rows = u_ref.shape[2]
    n_seq = rows // seq_chunks
    lanes = 2 * STATE
    pos = lax.broadcasted_iota(jnp.int32, (rows, lanes), 0) % seq_chunks

    def upstream(x, d, fwd):
        return pltpu.roll(x, d if fwd else rows - d, 0)

    def has_upstream(d, fwd):
        return (pos >= d) if fwd else (pos < seq_chunks - d)

    def scan(hr, hi, pi, fwd):
        q = 0 if fwd else 2
        for i in range(n_steps):
            d = 1 << i
            p_re, p_im = ap_ref[pi, q, i:i + 1, :], ap_ref[pi, q + 1, i:i + 1, :]
            if d < 8:
                ok = has_upstream(d, fwd)
                sr = jnp.where(ok, upstream(hr, d, fwd), 0.0)
                si = jnp.where(ok, upstream(hi, d, fwd), 0.0)
                hr, hi = hr + p_re * sr - p_im * si, hi + p_re * si + p_im * sr
            else:
                r3, i3 = hr.reshape(n_seq, seq_chunks, lanes), hi.reshape(n_seq, seq_chunks, lanes)
                keep = seq_chunks - d
                if fwd:
                    sr, si, tr, ti = r3[:, :keep], i3[:, :keep], r3[:, d:], i3[:, d:]
                else:
                    sr, si, tr, ti = r3[:, d:], i3[:, d:], r3[:, :keep], i3[:, :keep]
                nr = tr + p_re * sr - p_im * si
                ni = ti + p_re * si + p_im * sr
                if fwd:
                    r3 = jnp.concatenate([r3[:, :d], nr], axis=1)
                    i3 = jnp.concatenate([i3[:, :d], ni], axis=1)
                else:
                    r3 = jnp.concatenate([nr, r3[:, keep:]], axis=1)
                    i3 = jnp.concatenate([ni, i3[:, keep:]], axis=1)
                hr, hi = r3.reshape(rows, lanes), i3.reshape(rows, lanes)
        return hr, hi

    for pi in range(pairs):
        u0, u1 = u_ref[0, 2 * pi], u_ref[0, 2 * pi + 1]
        ub = jnp.concatenate([u0, u1], axis=1).astype(BF16)
        y0 = _dot(ub[:, :SSM_FLAT], m_ref[2 * pi]) + d_ref[2 * pi] * u0
        y1 = _dot(ub[:, SSM_FLAT:], m_ref[2 * pi + 1]) + d_ref[2 * pi + 1] * u1
        h = _dot(ub, s_ref[pi])
        state = {}
        for fwd, off in ((True, 0), (False, lanes)):
            hr, hi = h[:, off:off + lanes], h[:, 2 * lanes + off:3 * lanes + off]
            q = 0 if fwd else 2
            if has_h0:
                h0r = h0_ref[0, pi, 0:1, off:off + lanes]
                h0i = h0_ref[0, pi, 0:1, 2 * lanes + off:3 * lanes + off]
                a_re, a_im = ap_ref[pi, q, 0:1, :], ap_ref[pi, q + 1, 0:1, :]
                first = jnp.logical_not(has_upstream(1, fwd))
                hr = hr + jnp.where(first, a_re * h0r - a_im * h0i, 0.0)
                hi = hi + jnp.where(first, a_re * h0i + a_im * h0r, 0.0)
            else:
                h0r = h0i = 0.0
            hr, hi = scan(hr, hi, pi, fwd)
            ok = has_upstream(1, fwd)
            state[fwd] = (hr, hi, jnp.where(ok, upstream(hr, 1, fwd), h0r), jnp.where(ok, upstream(hi, 1, fwd), h0i))
        if emit_state:
            for q, (fwd, part) in enumerate(((True, 0), (False, 0), (True, 1), (False, 1))):
                fin_scr[q] = state[fwd][part]
                last = seq_chunks - 1 if fwd else 0
                e_ref[0, pi, :, q * lanes:(q + 1) * lanes] = fin_scr[q, pl.ds(last, n_seq, stride=seq_chunks), :]
        h_in = jnp.concatenate([state[True][2], state[False][2], state[True][3], state[False][3]], axis=1)
        y_state = _dot(h_in.astype(BF16), r_ref[pi])
        y_ref[0, 2 * pi] = y0 + y_state[:, :SSM_FLAT]
        y_ref[0, 2 * pi + 1] = y1 + y_state[:, SSM_FLAT:]


def _ssm_core(u_flat, pair_ops, h0, seq_chunks, emit_state):
    nb, _, rows, _ = u_flat.shape
    m_op, s_pair, r_pair, a_pair, d_row = pair_ops
    n_steps = a_pair.shape[2]
    assert (1 << n_steps) == seq_chunks
    pt = 4
    blk = pl.BlockSpec((1, 2 * pt, rows, SSM_FLAT), lambda b, g: (b, g, 0, 0))
    lead = lambda a, n: pl.BlockSpec((n,) + a.shape[1:], lambda b, g: (g,) + (0,) * (a.ndim - 1))
    in_specs = [blk, lead(m_op, 2 * pt), lead(s_pair, pt), lead(r_pair, pt), lead(a_pair, pt), lead(d_row, 2 * pt)]
    args = [u_flat, m_op, s_pair, r_pair, a_pair, d_row]
    if h0 is not None:
        in_specs.append(pl.BlockSpec((1, pt, 1, 8 * STATE), lambda b, g: (b, g, 0, 0)))
        args.append(h0)
    out_shape = [jax.ShapeDtypeStruct(u_flat.shape, F32)]
    out_specs = [blk]
    scratch = []
    if emit_state:
        n_seq = rows // seq_chunks
        out_shape.append(jax.ShapeDtypeStruct((nb, N_GROUPS // 2, n_seq, 8 * STATE), F32))
        out_specs.append(pl.BlockSpec((1, pt, n_seq, 8 * STATE), lambda b, g: (b, g, 0, 0)))
        scratch.append(pltpu.VMEM((4, rows, 2 * STATE), F32))
    kern = functools.partial(_ssm_core_kernel, seq_chunks=seq_chunks, n_steps=n_steps, pairs=pt,
                             has_h0=h0 is not None, emit_state=emit_state)
    return pl.pallas_call(
        kern, out_shape=tuple(out_shape), grid=(nb, N_GROUPS // (2 * pt)),
        in_specs=in_specs, out_specs=tuple(out_specs), scratch_shapes=scratch,
        compiler_params=_params(2), name="ssm_core",
    )(*args)


def _ssm_out_kernel(y_ref, x_ref, mod_ref, ln_ref, wg_ref, wo_ref, out_ref, y_scr):
    i = pl.program_id(0)
    n_chunks = x_ref.shape[0] // SSM_CHUNK

    @pl.when(i == 0)
    def _():
        y_scr[1] = jnp.zeros(y_scr.shape[1:], F32)

    def step(fill_slot):
        fill, ready = y_scr.at[fill_slot], y_scr.at[1 - fill_slot]
        for c in range(LANE_TILES):
            for half in range(SSM_CHUNK // LANE_BLOCKS):
                per_group = [y_ref[0, c * LANE_BLOCKS + gl, :, half * 128:(half + 1) * 128]
                             for gl in range(LANE_BLOCKS)]
                per_token = _block_transpose(per_group)
                for s in range(LANE_BLOCKS):
                    fill[c, pl.ds(half * LANE_BLOCKS + s, n_chunks, stride=SSM_CHUNK), :] = per_token[s]
        y = jnp.concatenate([ready[c] for c in range(LANE_TILES)], axis=1)
        vg = _dot(_gelu_tanh(y).astype(BF16), wg_ref[...])
        glu = (vg[:, :D_MODEL] * jax.nn.sigmoid(vg[:, D_MODEL:])).astype(BF16)
        z = DN_ALPHA * x_ref[...] + mod_ref[0, 2:3, :] * _dot(glu, wo_ref[...])
        out_ref[...] = _layer_norm(z, ln_ref[0:1, :], ln_ref[1:2, :])

    for parity in range(2):
        pl.when(i % 2 == parity)(functools.partial(step, parity))


def _ssm_out(y_flat, x2, mods, mod_base, rows_per_mod, ln, w_glu, w_out):
    r = x2.shape[0]
    nb = y_flat.shape[0]
    t = r // nb
    tm = _row_tile(math.gcd(rows_per_mod, t), 512)
    per_mod = rows_per_mod // tm
    per_seq = t // tm
    nj = tm // SSM_CHUNK
    n_tiles = r // tm
    fill_tile = lambda i: jnp.minimum(i, n_tiles - 1)
    done_tile = lambda i: jnp.maximum(i - 1, 0)
    row = pl.BlockSpec((tm, D_MODEL), lambda i: (done_tile(i), 0))
    return pl.pallas_call(
        _ssm_out_kernel,
        out_shape=jax.ShapeDtypeStruct((r, D_MODEL), F32),
        grid=(n_tiles + 1,),
        in_specs=[pl.BlockSpec((1, N_GROUPS, nj, SSM_FLAT),
                               lambda i: (fill_tile(i) // per_seq, 0, fill_tile(i) % per_seq, 0)),
                  row,
                  _mods_spec(mods, lambda i: mod_base + done_tile(i) // per_mod),
                  _ln_spec(ln), _layer_spec(w_glu), _layer_spec(w_out)],
        out_specs=row,
        scratch_shapes=[pltpu.VMEM((2, LANE_TILES, tm, 128), F32)],
        compiler_params=_params(1),
        name="ssm_out",
    )(y_flat, x2, mods[0], ln[0], w_glu[0], w_out[0])


def _ssm_mixer(x2, nb, seq_chunks, mods, mod_base, rows_per_mod, ln, w_in, ops, w_glu, w_out, h0, emit_state):
    u_flat = _ssm_in(x2, mods, mod_base, rows_per_mod, w_in, nb)
    outs = _ssm_core(u_flat, ops, h0, seq_chunks, emit_state)
    x_new = _ssm_out(outs[0], x2, mods, mod_base, rows_per_mod, ln, w_glu, w_out)
    return x_new, (outs[1] if emit_state else None)


def _rope_tables(t):
    n_rows = t // GRID_W
    row = jnp.repeat(jnp.arange(n_rows, dtype=F32), GRID_W)
    col = (jnp.arange(t) % GRID_W).astype(F32)
    nfreq = HEAD_DIM // 4
    inv = jnp.power(ROPE_BASE, -jnp.arange(nfreq, dtype=F32) / nfreq)
    ang_r, ang_c = row[:, None] * inv, col[:, None] * inv
    cos = jnp.concatenate([jnp.cos(ang_r)] * 2 + [jnp.cos(ang_c)] * 2, -1)
    sin = jnp.concatenate([-jnp.sin(ang_r), jnp.sin(ang_r), -jnp.sin(ang_c), jnp.sin(ang_c)], -1)
    return jnp.tile(cos, (1, 2)), jnp.tile(sin, (1, 2))


def kernel(x_prompt, x_sample, cache_k, cache_v, state_ssm_re, state_ssm_im, c, c_ctx, w_ada, b_ada, ln_g, ln_b, w_qkv, w_o, attn_sink, ssm_w_in, ssm_lam_re, ssm_lam_im, ssm_log_dt, ssm_b_re, ssm_b_im, ssm_c_re, ssm_c_im, ssm_d, ssm_w_glu, ssm_w_out, ffn_w1, ffn_w3, ffn_w2):
    nc, tc, _ = x_prompt.shape
    nl, tl, _ = x_sample.shape
    n_attn = w_qkv.shape[0]
    n_ssm = ssm_w_in.shape[0]

    n_mod_rows = -(-(1 + nl) // 8) * 8
    cvec = jnp.zeros((n_mod_rows, D_MODEL), F32).at[0].set(c_ctx).at[1:1 + nl].set(c)
    mods_all = _ada_mods(cvec, w_ada, b_ada)
    ln_all = jnp.stack([ln_g, ln_b], axis=2)

    bf = lambda w: w.astype(BF16)
    w_v = w_qkv[:, :, Q_WIDTH + KV_WIDTH:].reshape(n_attn, D_MODEL, N_KV_HEADS, HEAD_DIM)
    w_v_ext = jnp.pad(w_v, ((0, 0), (0, 0), (0, 0), (0, 128 - HEAD_DIM))).reshape(n_attn, D_MODEL, V_EXT)
    w_qkv_ctx = bf(jnp.concatenate([w_qkv, w_v_ext], axis=-1))
    w_qkv_lat = bf(jnp.concatenate([w_qkv[:, :, :Q_WIDTH + KV_WIDTH], w_v_ext], axis=-1))
    w_o_b = bf(w_o)
    w_in_b, w_glu_b, w_out_b = bf(ssm_w_in), bf(ssm_w_glu), bf(ssm_w_out)
    w1_b, w3_b, w2_b = bf(ffn_w1), bf(ffn_w3), bf(ffn_w2)
    cos, sin = _rope_tables(tl)
    kx = bf(cache_k).reshape(nl, n_attn, -1, KV_WIDTH)
    vx = jnp.concatenate([bf(cache_v), jnp.ones(cache_v.shape[:-1] + (128 - HEAD_DIM,), BF16)], -1)
    vx = vx.reshape(nl, n_attn, -1, V_EXT)

    steps_c = (tc // SSM_CHUNK).bit_length() - 1
    steps_l = (tl // SSM_CHUNK).bit_length() - 1
    ssm_ops = []
    for s in range(n_ssm):
        prm = (ssm_lam_re[s], ssm_lam_im[s], ssm_log_dt[s], ssm_b_re[s], ssm_b_im[s],
               ssm_c_re[s], ssm_c_im[s], ssm_d[s])
        ssm_ops.append(_s5_operators(*prm, max(steps_c, steps_l)))

    def ops_for(s, n_steps):
        m_op, s_pair, r_pair, a_pair, d_row = ssm_ops[s]
        return m_op, s_pair, r_pair, a_pair[:, :, :n_steps], d_row

    xc = x_prompt.reshape(nc * tc, D_MODEL)
    xl = x_sample.reshape(nl * tl, D_MODEL)
    new_k, new_v, new_sr, new_si = [], [], [], []
    for l in range(DEPTH):
        mods = (mods_all, l)
        ln1, ln2 = (ln_all, l, 0), (ln_all, l, 1)
        if l % 2 == 0:
            a = l // 2
            q, kb, vb, k32, v32 = _qkv_ctx(xc, mods, (w_qkv_ctx, a))
            new_k.append(k32.reshape(nc, tc, N_KV_HEADS, HEAD_DIM))
            new_v.append(v32.reshape(nc, tc, N_KV_HEADS, HEAD_DIM))
            xc = _attn_ctx(attn_sink[a], q, kb, vb, xc, mods, ln1, (w_o_b, a), tc)
            qp, qr, kr, v = _qkv_lat(xl, mods, (w_qkv_lat, a), cos, sin, tl)
            xl = _attn_lat(attn_sink[a], qp, qr, kr, v, (kx, a), (vx, a), xl, mods, ln1, (w_o_b, a), nl, tl)
        else:
            s = l // 2
            xc, e_c = _ssm_mixer(xc, 1, tc // SSM_CHUNK, mods, 0, nc * tc, ln1, (w_in_b, s),
                                 ops_for(s, steps_c), (w_glu_b, s), (w_out_b, s), None, True)
            fin = e_c.reshape(N_GROUPS // 2, nc, 4, 2, STATE).transpose(2, 1, 0, 3, 4).reshape(4, nc, N_GROUPS, STATE)
            new_sr.append(jnp.stack([fin[0], fin[1]], 1))
            new_si.append(jnp.stack([fin[2], fin[3]], 1))
            h0 = jnp.stack([state_ssm_re[:, s, 0], state_ssm_re[:, s, 1],
                            state_ssm_im[:, s, 0], state_ssm_im[:, s, 1]], 1)
            h0 = h0.reshape(nl, 4, N_GROUPS // 2, 2 * STATE).transpose(0, 2, 1, 3)
            h0 = h0.reshape(nl, N_GROUPS // 2, 8 * STATE)
            xl, _ = _ssm_mixer(xl, nl, tl // SSM_CHUNK, mods, 1, tl, ln1, (w_in_b, s),
                               ops_for(s, steps_l), (w_glu_b, s), (w_out_b, s), h0[:, :, None, :], False)
        xc = _ffn(xc, mods, 0, nc * tc, ln2, (w1_b, l), (w3_b, l), (w2_b, l))
        xl = _ffn(xl, mods, 1, tl, ln2, (w1_b, l), (w3_b, l), (w2_b, l))

    return (xc.reshape(nc, tc, D_MODEL), xl.reshape(nl, tl, D_MODEL),
            jnp.stack(new_k, axis=1), jnp.stack(new_v, axis=1),
            jnp.stack(new_sr, axis=1), jnp.stack(new_si, axis=1))
```

```python
import functools
import math

import jax
import jax.numpy as jnp
from jax import lax
from jax.experimental import pallas as pl
from jax.experimental.pallas import tpu as pltpu

F32 = jnp.float32
BF16 = jnp.bfloat16

D_MODEL = 1024
DEPTH = 4
N_HEADS = 16
N_KV_HEADS = 4
HEAD_DIM = 64
HEADS_PER_KV = N_HEADS // N_KV_HEADS
Q_WIDTH = N_HEADS * HEAD_DIM
KV_WIDTH = N_KV_HEADS * HEAD_DIM
GRID_W = 64
ATT_BLOCK = 128
ROPE_BASE = 10000.0
SSM_GROUP = 16
N_GROUPS = D_MODEL // SSM_GROUP
STATE = 64
SSM_CHUNK = 16
SSM_FLAT = SSM_CHUNK * SSM_GROUP
D_FF = -(-8 * D_MODEL // (3 * 256)) * 256
FF_CHUNK = 256
FFN_ROWS = 1024
FFN_SPLIT = 2
DN_ALPHA = (2 * DEPTH) ** 0.25
LN_EPS = 1e-5
NEG_INF = -1e30
N_MODS = 6
VMEM_LIMIT = 56 * 1024 * 1024
HIGHEST = lax.Precision.HIGHEST


def _params(n_axes):
    return pltpu.CompilerParams(dimension_semantics=("arbitrary",) * n_axes,
                                vmem_limit_bytes=VMEM_LIMIT)


def _layer_spec(param, single=False):
    stacked, layer = param
    rest = stacked.shape[1:]
    mode = dict(pipeline_mode=pl.Buffered(1)) if single else {}
    return pl.BlockSpec((None,) + rest, lambda *_: (layer,) + (0,) * len(rest), **mode)


def _mods_spec(mods, row_of):
    return pl.BlockSpec((None, 1, N_MODS, D_MODEL), lambda *ids: (mods[1], row_of(*ids), 0, 0))


def _ln_spec(ln):
    return pl.BlockSpec((None, None, 2, D_MODEL), lambda *_: (ln[1], ln[2], 0, 0))


def _layer_norm(y, g, b):
    mu = jnp.mean(y, -1, keepdims=True)
    d = y - mu
    var = jnp.mean(d * d, -1, keepdims=True)
    return d * lax.rsqrt(var + LN_EPS) * g + b


def _silu(x):
    return x * jax.nn.sigmoid(x)


def _gelu_tanh(x):
    return 0.5 * x * (1.0 + jnp.tanh(math.sqrt(2.0 / math.pi) * (x + 0.044715 * (x * x * x))))


def _dot(a, b):
    return jnp.dot(a, b, preferred_element_type=F32)


def _dot_nt(a, b):
    return lax.dot_general(a, b, (((1,), (1,)), ((), ())), preferred_element_type=F32)


def _row_tile(rows_per_mod, want):
    tm = min(want, rows_per_mod)
    assert rows_per_mod % tm == 0
    return tm


def _mods_kernel(c_ref, w_ref, b_ref, o_ref):
    a = _silu(c_ref[...]).astype(BF16)
    o_ref[0] = _dot(a, w_ref[0].astype(BF16)) + b_ref[0]


def _ada_mods(cvec, w_ada, b_ada):
    r = cvec.shape[0]
    tn = 1536
    out = pl.pallas_call(
        _mods_kernel,
        out_shape=jax.ShapeDtypeStruct((DEPTH, r, N_MODS * D_MODEL), F32),
        grid=(DEPTH, N_MODS * D_MODEL // tn),
        in_specs=[pl.BlockSpec((r, D_MODEL), lambda l, j: (0, 0)),
                  pl.BlockSpec((1, D_MODEL, tn), lambda l, j: (l, 0, j)),
                  pl.BlockSpec((1, 1, tn), lambda l, j: (l, 0, j))],
        out_specs=pl.BlockSpec((1, r, tn), lambda l, j: (l, 0, j)),
        compiler_params=_params(2),
        name="ada_mods",
    )(cvec, w_ada, b_ada.reshape(DEPTH, 1, N_MODS * D_MODEL))
    return out.reshape(DEPTH, r, N_MODS, D_MODEL)


Q_SCALE = HEAD_DIM ** -0.5 * math.log2(math.e)
V_EXT = N_KV_HEADS * 128


def _rope(x, cos, sin, first_half):
    outs = []
    for c in range(x.shape[1] // 128):
        xc = x[:, c * 128:(c + 1) * 128]
        partner = jnp.where(first_half, pltpu.roll(xc, 128 - 16, 1), pltpu.roll(xc, 16, 1))
        outs.append(xc * cos + partner * sin)
    return jnp.concatenate(outs, axis=1)


def _ones_lanes(width):
    return jnp.where((lax.broadcasted_iota(jnp.int32, (1, width), 1) % 128) >= HEAD_DIM, 1.0, 0.0)


def _qkv_ctx_kernel(x_ref, mod_ref, w_ref, q_ref, kb_ref, vb_ref, k_ref, v_ref):
    h = (x_ref[...] * (1.0 + mod_ref[0, 1:2, :]) + mod_ref[0, 0:1, :]).astype(BF16)
    qkv = _dot(h, w_ref[...])
    q_ref[...] = (qkv[:, :Q_WIDTH] * Q_SCALE).astype(BF16)
    k = qkv[:, Q_WIDTH:Q_WIDTH + KV_WIDTH]
    k_ref[...] = k
    v_ref[...] = qkv[:, Q_WIDTH + KV_WIDTH:Q_WIDTH + 2 * KV_WIDTH]
    kb_ref[...] = k.astype(BF16)
    vb_ref[...] = (qkv[:, Q_WIDTH + 2 * KV_WIDTH:] + _ones_lanes(V_EXT)).astype(BF16)


def _qkv_lat_kernel(x_ref, mod_ref, w_ref, cos_ref, sin_ref, qp_ref, qr_ref, kr_ref, v_ref):
    h = (x_ref[...] * (1.0 + mod_ref[0, 1:2, :]) + mod_ref[0, 0:1, :]).astype(BF16)
    qkv = _dot(h, w_ref[...])
    cos = cos_ref[...]
    sin = sin_ref[...]
    first_half = (lax.broadcasted_iota(jnp.int32, cos.shape, 1) & 16) == 0
    q = qkv[:, :Q_WIDTH] * Q_SCALE
    qp_ref[...] = q.astype(BF16)
    qr_ref[...] = _rope(q, cos, sin, first_half).astype(BF16)
    kr_ref[...] = _rope(qkv[:, Q_WIDTH:Q_WIDTH + KV_WIDTH], cos, sin, first_half).astype(BF16)
    v_ref[...] = (qkv[:, Q_WIDTH + KV_WIDTH:] + _ones_lanes(V_EXT)).astype(BF16)


def _qkv_ctx(x2, mods, w_ext):
    r = x2.shape[0]
    tm = _row_tile(r, 512)
    row = lambda w: pl.BlockSpec((tm, w), lambda i: (i, 0))
    return pl.pallas_call(
        _qkv_ctx_kernel,
        out_shape=(jax.ShapeDtypeStruct((r, Q_WIDTH), BF16),
                   jax.ShapeDtypeStruct((r, KV_WIDTH), BF16),
                   jax.ShapeDtypeStruct((r, V_EXT), BF16),
                   jax.ShapeDtypeStruct((r, KV_WIDTH), F32),
                   jax.ShapeDtypeStruct((r, KV_WIDTH), F32)),
        grid=(r // tm,),
        in_specs=[row(D_MODEL), _mods_spec(mods, lambda i: 0), _layer_spec(w_ext)],
        out_specs=(row(Q_WIDTH), row(KV_WIDTH), row(V_EXT), row(KV_WIDTH), row(KV_WIDTH)),
        compiler_params=_params(1),
        name="qkv_ctx",
    )(x2, mods[0], w_ext[0])


def _qkv_lat(x2, mods, w_ext, cos, sin, t):
    r = x2.shape[0]
    tm = _row_tile(t, 512)
    per_seq = t // tm
    row = lambda w: pl.BlockSpec((tm, w), lambda i: (i, 0))
    tab = pl.BlockSpec((tm, 128), lambda i: (i % per_seq, 0))
    return pl.pallas_call(
        _qkv_lat_kernel,
        out_shape=(jax.ShapeDtypeStruct((r, Q_WIDTH), BF16),
                   jax.ShapeDtypeStruct((r, Q_WIDTH), BF16),
                   jax.ShapeDtypeStruct((r, KV_WIDTH), BF16),
                   jax.ShapeDtypeStruct((r, V_EXT), BF16)),
        grid=(r // tm,),
        in_specs=[row(D_MODEL), _mods_spec(mods, lambda i: 1 + i // per_seq), _layer_spec(w_ext), tab, tab],
        out_specs=(row(Q_WIDTH), row(Q_WIDTH), row(KV_WIDTH), row(V_EXT)),
        compiler_params=_params(1),
        name="qkv_lat",
    )(x2, mods[0], w_ext[0], cos, sin)


def _attn_epilogue(o_scr, x_ref, mod_ref, ln_ref, wo_ref, out_ref):
    y = _dot(o_scr[...], wo_ref[...])
    z = DN_ALPHA * x_ref[...] + mod_ref[0, 2:3, :] * y
    out_ref[...] = _layer_norm(z, ln_ref[0:1, :], ln_ref[1:2, :])


def _attn_ctx_kernel(sink_ref, q_ref, k_ref, v_ref, x_ref, mod_ref, ln_ref, wo_ref, out_ref, o_scr, s_scr, m_scr):
    rows = q_ref.shape[0]
    n_tiles = k_ref.shape[0] // 128

    def scores(kv, slot):
        k = k_ref[:, kv * HEAD_DIM:(kv + 1) * HEAD_DIM]
        for g in range(HEADS_PER_KV):
            h = kv * HEADS_PER_KV + g
            s = _dot_nt(q_ref[:, h * HEAD_DIM:(h + 1) * HEAD_DIM], k)
            m = jnp.maximum(jnp.max(s, -1, keepdims=True), sink_ref[h] * math.log2(math.e))
            s_scr[slot, g * rows:(g + 1) * rows, :] = s
            m_scr[slot, g * rows:(g + 1) * rows, :] = jnp.broadcast_to(m, (rows, 128))

    def values(kv, slot):
        v = v_ref[:, kv * 128:(kv + 1) * 128]
        for g in range(HEADS_PER_KV):
            h = kv * HEADS_PER_KV + g
            r0, r1 = g * rows, (g + 1) * rows
            m = m_scr[slot, r0:r1, :]
            p = [jnp.exp2(s_scr[slot, r0:r1, c * 128:(c + 1) * 128] - m).astype(BF16) for c in range(n_tiles)]
            o_ext = _dot(jnp.concatenate(p, axis=1), v)
            den = pltpu.roll(o_ext, HEAD_DIM, 1) + jnp.exp2(sink_ref[h] * math.log2(math.e) - m)
            o_scr[:, h * HEAD_DIM:(h + 1) * HEAD_DIM] = (o_ext / den)[:, :HEAD_DIM].astype(BF16)

    scores(0, 0)
    for kv in range(N_KV_HEADS):
        if kv + 1 < N_KV_HEADS:
            scores(kv + 1, (kv + 1) % 2)
        values(kv, kv % 2)
    _attn_epilogue(o_scr, x_ref, mod_ref, ln_ref, wo_ref, out_ref)


def _attn_ctx(sink, q, k, v_ext, x2, mods, ln, w_o, seq):
    r = x2.shape[0]
    row = lambda w: pl.BlockSpec((seq, w), lambda i: (i, 0))
    return pl.pallas_call(
        _attn_ctx_kernel,
        out_shape=jax.ShapeDtypeStruct((r, D_MODEL), F32),
        grid=(r // seq,),
        in_specs=[pl.BlockSpec(memory_space=pltpu.SMEM),
                  row(Q_WIDTH), row(KV_WIDTH), row(V_EXT), row(D_MODEL),
                  _mods_spec(mods, lambda i: 0), _ln_spec(ln), _layer_spec(w_o)],
        out_specs=row(D_MODEL),
        scratch_shapes=[pltpu.VMEM((seq, Q_WIDTH), BF16),
                        pltpu.VMEM((2, HEADS_PER_KV * seq, seq), F32),
                        pltpu.VMEM((2, HEADS_PER_KV * seq, 128), F32)],
        compiler_params=_params(1),
        name="attn_ctx",
    )(sink, q, k, v_ext, x2, mods[0], ln[0], w_o[0])


LOCAL_KEYS = 3 * ATT_BLOCK
ATT_STEP_BLOCKS = 4


def _attn_lat_kernel(sink_ref, qp_ref, qr_ref, k_ref, v_ref, kx_ref, vx_ref, x_ref, mod_ref, ln_ref,
                     wo_ref, out_ref, o_scr, s_scr, m_scr, bias_scr):
    rows = ATT_BLOCK
    seq = k_ref.shape[0]
    n_ctx = kx_ref.shape[1]
    n_tiles = (n_ctx + LOCAL_KEYS) // 128
    n_sub = qp_ref.shape[0] // rows
    r_idx = lax.broadcasted_iota(jnp.int32, (rows, LOCAL_KEYS), 0)
    c_idx = lax.broadcasted_iota(jnp.int32, (rows, LOCAL_KEYS), 1)
    starts = []
    for sub in range(n_sub):
        j = pl.program_id(1) * n_sub + sub
        start = pl.multiple_of(jnp.clip((j - 1) * rows, 0, seq - LOCAL_KEYS), rows)
        bias_scr[sub] = jnp.where(jnp.abs(r_idx - c_idx + (j * rows - start)) <= ATT_BLOCK, 0.0, NEG_INF)
        starts.append(start)

    def scores(sub, kv, slot):
        lo, hi = kv * HEAD_DIM, (kv + 1) * HEAD_DIM
        q0 = sub * rows
        kx = kx_ref[0, :, lo:hi]
        kl = k_ref[pl.ds(starts[sub], LOCAL_KEYS), lo:hi]
        for g in range(HEADS_PER_KV):
            h = kv * HEADS_PER_KV + g
            r0, r1 = g * rows, (g + 1) * rows
            s_x = _dot_nt(qp_ref[q0:q0 + rows, h * HEAD_DIM:(h + 1) * HEAD_DIM], kx)
            s_l = _dot_nt(qr_ref[q0:q0 + rows, h * HEAD_DIM:(h + 1) * HEAD_DIM], kl) + bias_scr[sub]
            m = jnp.maximum(jnp.maximum(jnp.max(s_x, -1, keepdims=True), jnp.max(s_l, -1, keepdims=True)),
                            sink_ref[h] * math.log2(math.e))
            s_scr[slot, r0:r1, :n_ctx] = s_x
            s_scr[slot, r0:r1, n_ctx:] = s_l
            m_scr[slot, r0:r1, :] = jnp.broadcast_to(m, (rows, 128))

    def values(sub, kv, slot):
        q0 = sub * rows
        vx = vx_ref[0, :, kv * 128:(kv + 1) * 128]
        vl = v_ref[pl.ds(starts[sub], LOCAL_KEYS), kv * 128:(kv + 1) * 128]
        for g in range(HEADS_PER_KV):
            h = kv * HEADS_PER_KV + g
            r0, r1 = g * rows, (g + 1) * rows
            m = m_scr[slot, r0:r1, :]
            p = [jnp.exp2(s_scr[slot, r0:r1, c * 128:(c + 1) * 128] - m).astype(BF16) for c in range(n_tiles)]
            o_ext = (_dot(jnp.concatenate(p[:n_ctx // 128], axis=1), vx)
                     + _dot(jnp.concatenate(p[n_ctx // 128:], axis=1), vl))
            den = pltpu.roll(o_ext, HEAD_DIM, 1) + jnp.exp2(sink_ref[h] * math.log2(math.e) - m)
            o_scr[q0:q0 + rows, h * HEAD_DIM:(h + 1) * HEAD_DIM] = (o_ext / den)[:, :HEAD_DIM].astype(BF16)

    units = [(sub, kv) for sub in range(n_sub) for kv in range(N_KV_HEADS)]
    scores(*units[0], 0)
    for u, unit in enumerate(units):
        if u + 1 < len(units):
            scores(*units[u + 1], (u + 1) % 2)
        values(*unit, u % 2)
    _attn_epilogue(o_scr, x_ref, mod_ref, ln_ref, wo_ref, out_ref)


def _attn_lat(sink, qp, qr, kr, v_ext, k_ctx, v_ctx_ext, x2, mods, ln, w_o, n, t):
    assert t >= LOCAL_KEYS
    blk = ATT_STEP_BLOCKS * ATT_BLOCK
    nblk = t // blk
    row = lambda w: pl.BlockSpec((blk, w), lambda b, j: (b * nblk + j, 0))
    seq = lambda w: pl.BlockSpec((t, w), lambda b, j: (b, 0))
    ctx = lambda a: pl.BlockSpec((1, None) + a[0].shape[2:], lambda b, j: (b, a[1], 0, 0))
    return pl.pallas_call(
        _attn_lat_kernel,
        out_shape=jax.ShapeDtypeStruct((n * t, D_MODEL), F32),
        grid=(n, nblk),
        in_specs=[pl.BlockSpec(memory_space=pltpu.SMEM),
                  row(Q_WIDTH), row(Q_WIDTH), seq(KV_WIDTH), seq(V_EXT),
                  ctx(k_ctx), ctx(v_ctx_ext), row(D_MODEL),
                  _mods_spec(mods, lambda b, j: 1 + b), _ln_spec(ln), _layer_spec(w_o)],
        out_specs=row(D_MODEL),
        scratch_shapes=[pltpu.VMEM((blk, Q_WIDTH), BF16),
                        pltpu.VMEM((2, HEADS_PER_KV * ATT_BLOCK, k_ctx[0].shape[2] + LOCAL_KEYS), F32),
                        pltpu.VMEM((2, HEADS_PER_KV * ATT_BLOCK, 128), F32),
                        pltpu.VMEM((ATT_STEP_BLOCKS, ATT_BLOCK, LOCAL_KEYS), F32)],
        compiler_params=_params(2),
        name="attn_lat",
    )(sink, qp, qr, kr, v_ext, k_ctx[0], v_ctx_ext[0], x2, mods[0], ln[0], w_o[0])


def _ffn_kernel(x_ref, mod_ref, ln_ref, w1_ref, w3_ref, w2_ref, out_ref, acc_ref):
    half = x_ref.shape[0] // FFN_SPLIT
    for part in range(FFN_SPLIT):
        r0, r1 = part * half, (part + 1) * half
        x = x_ref[r0:r1, :]
        h = (x * (1.0 + mod_ref[0, 4:5, :]) + mod_ref[0, 3:4, :]).astype(BF16)
        for c in range(D_FF // FF_CHUNK):
            lo, hi = c * FF_CHUNK, (c + 1) * FF_CHUNK
            a = _dot(h, w1_ref[:, lo:hi])
            b = _dot(h, w3_ref[:, lo:hi])
            y = _dot((_silu(a) * b).astype(BF16), w2_ref[lo:hi, :])
            if c == 0:
                acc_ref[r0:r1, :] = y
            else:
                acc_ref[r0:r1, :] += y
        z = DN_ALPHA * x + mod_ref[0, 5:6, :] * acc_ref[r0:r1, :]
        out_ref[r0:r1, :] = _layer_norm(z, ln_ref[0:1, :], ln_ref[1:2, :])


def _ffn(x2, mods, mod_base, rows_per_mod, ln, w1, w3, w2):
    r = x2.shape[0]
    tm = _row_tile(rows_per_mod, FFN_ROWS)
    per_mod = rows_per_mod // tm
    return pl.pallas_call(
        _ffn_kernel,
        out_shape=jax.ShapeDtypeStruct((r, D_MODEL), F32),
        grid=(r // tm,),
        in_specs=[pl.BlockSpec((tm, D_MODEL), lambda i: (i, 0)),
                  _mods_spec(mods, lambda i: mod_base + i // per_mod),
                  _ln_spec(ln), _layer_spec(w1, single=True), _layer_spec(w3, single=True),
                  _layer_spec(w2, single=True)],
        out_specs=pl.BlockSpec((tm, D_MODEL), lambda i: (i, 0)),
        scratch_shapes=[pltpu.VMEM((tm, D_MODEL), F32)],
        compiler_params=_params(1),
        name="ffn",
    )(x2, mods[0], ln[0], w1[0], w3[0], w2[0])


LANE_BLOCKS = 128 // SSM_GROUP
LANE_TILES = D_MODEL // 128
SSM_ROWS = 1024


def _block_transpose(xs):
    xs = list(xs)
    blk = lax.broadcasted_iota(jnp.int32, xs[0].shape, 1) // SSM_GROUP
    k = LANE_BLOCKS // 2
    while k >= 1:
        keep_lo = (blk & k) == 0
        for i in range(LANE_BLOCKS):
            if i & k:
                continue
            lo, hi = xs[i], xs[i + k]
            xs[i] = jnp.where(keep_lo, lo, pltpu.roll(hi, SSM_GROUP * k, 1))
            xs[i + k] = jnp.where(keep_lo, pltpu.roll(lo, 128 - SSM_GROUP * k, 1), hi)
        k //= 2
    return xs


def _ssm_in_kernel(x_ref, mod_ref, w_ref, o_ref, u_scr):
    h = (x_ref[...] * (1.0 + mod_ref[0, 1:2, :]) + mod_ref[0, 0:1, :]).astype(BF16)
    u = _dot(h, w_ref[...])
    for c in range(LANE_TILES):
        u_scr[c] = u[:, c * 128:(c + 1) * 128]
    n_chunks = x_ref.shape[0] // SSM_CHUNK
    for c in range(LANE_TILES):
        for half in range(SSM_CHUNK // LANE_BLOCKS):
            per_token = [u_scr[c, pl.ds(half * LANE_BLOCKS + s, n_chunks, stride=SSM_CHUNK), :]
                         for s in range(LANE_BLOCKS)]
            per_group = _block_transpose(per_token)
            for gl in range(LANE_BLOCKS):
                o_ref[0, c * LANE_BLOCKS + gl, :, half * 128:(half + 1) * 128] = per_group[gl]


def _ssm_in(x2, mods, mod_base, rows_per_mod, w, nb):
    r = x2.shape[0]
    t = r // nb
    tm = _row_tile(math.gcd(rows_per_mod, t), SSM_ROWS)
    per_mod = rows_per_mod // tm
    per_seq = t // tm
    nj = tm // SSM_CHUNK
    return pl.pallas_call(
        _ssm_in_kernel,
        out_shape=jax.ShapeDtypeStruct((nb, N_GROUPS, t // SSM_CHUNK, SSM_FLAT), F32),
        grid=(r // tm,),
        in_specs=[pl.BlockSpec((tm, D_MODEL), lambda i: (i, 0)),
                  _mods_spec(mods, lambda i: mod_base + i // per_mod), _layer_spec(w)],
        out_specs=pl.BlockSpec((1, N_GROUPS, nj, SSM_FLAT), lambda i: (i // per_seq, 0, i % per_seq, 0)),
        scratch_shapes=[pltpu.VMEM((LANE_TILES, tm, 128), F32)],
        compiler_params=_params(1),
        name="ssm_in",
    )(x2, mods[0], w[0])


def _s5_prep_kernel(logdt_ref, lam_row_ref, lam_col_ref, bt_ref, ct_ref,
                    m_ref, s_ref, r_ref, ap_ref, *, n_steps):
    pair = pl.program_id(0)
    lc = SSM_CHUNK
    lanes = 2 * STATE

    def transition(lam_re, lam_im, d, axis):
        first = lax.broadcasted_iota(jnp.int32, lam_re.shape, axis) < STATE
        log_dt = jnp.where(first, jnp.full(lam_re.shape, logdt_ref[d, 2 * pair], F32),
                           jnp.full(lam_re.shape, logdt_ref[d, 2 * pair + 1], F32))
        dt = jnp.exp(log_dt)
        mag = jnp.exp(lam_re * dt)
        return mag * jnp.cos(lam_im * dt), mag * jnp.sin(lam_im * dt)

    def powers(a_re, a_im, n):
        pr, pi = [jnp.ones_like(a_re)], [jnp.zeros_like(a_im)]
        for _ in range(n):
            r, i = pr[-1], pi[-1]
            pr.append(r * a_re - i * a_im)
            pi.append(r * a_im + i * a_re)
        return pr, pi

    dot_hi = lambda a, b: jnp.dot(a, b, precision=HIGHEST, preferred_element_type=F32)
    row_shape = (SSM_GROUP, lanes)
    x_group = lax.broadcasted_iota(jnp.int32, (SSM_FLAT, lanes), 1) // STATE
    w_group = lax.broadcasted_iota(jnp.int32, (lanes, SSM_FLAT), 0) // STATE
    t_blk = lax.broadcasted_iota(jnp.int32, (SSM_FLAT, SSM_FLAT), 1) // SSM_GROUP
    t_blk_w = lax.broadcasted_iota(jnp.int32, (lanes, SSM_FLAT), 1) // SSM_GROUP
    m_acc = [jnp.zeros((SSM_FLAT, SSM_FLAT), F32) for _ in range(2)]
    spread = jnp.where(lax.broadcasted_iota(jnp.int32, (SSM_GROUP, SSM_FLAT), 0)
                       == lax.broadcasted_iota(jnp.int32, (SSM_GROUP, SSM_FLAT), 1) % SSM_GROUP, 1.0, 0.0)

    for d in range(2):
        fwd = d == 0
        lam_re = jnp.broadcast_to(lam_row_ref[0, 2 * d:2 * d + 1, :], row_shape)
        lam_im = jnp.broadcast_to(lam_row_ref[0, 2 * d + 1:2 * d + 2, :], row_shape)
        a_re, a_im = transition(lam_re, lam_im, d, 1)
        den = lam_re * lam_re + lam_im * lam_im
        nr, ni = a_re - 1.0, a_im
        coef_re = (nr * lam_re + ni * lam_im) / den
        coef_im = (ni * lam_re - nr * lam_im) / den
        bt_re, bt_im = bt_ref[0, 2 * d], bt_ref[0, 2 * d + 1]
        bb_re = coef_re * bt_re - coef_im * bt_im
        bb_im = coef_re * bt_im + coef_im * bt_re
        pr, pi = powers(a_re, a_im, lc)
        order = [lc - 1 - s for s in range(lc)] if fwd else list(range(lc))
        x_re = jnp.concatenate([pr[k] * bb_re - pi[k] * bb_im for k in order], axis=0)
        x_im = jnp.concatenate([pr[k] * bb_im + pi[k] * bb_re for k in order], axis=0)
        sq_re, sq_im = pr[lc], pi[lc]
        for i in range(n_steps):
            ap_ref[0, 2 * d, i:i + 1, :] = sq_re[0:1, :]
            ap_ref[0, 2 * d + 1, i:i + 1, :] = sq_im[0:1, :]
            sq_re, sq_im = sq_re * sq_re - sq_im * sq_im, 2.0 * sq_re * sq_im
        ct_re = lax.dot_general(ct_ref[0, 2 * d], spread, (((0,), (0,)), ((), ())), precision=HIGHEST,
                                preferred_element_type=F32)
        ct_im = lax.dot_general(ct_ref[0, 2 * d + 1], spread, (((0,), (0,)), ((), ())), precision=HIGHEST,
                                preferred_element_type=F32)
        for gl in range(2):
            xg_re = jnp.where(x_group == gl, x_re, 0.0)
            xg_im = jnp.where(x_group == gl, x_im, 0.0)
            r0 = gl * SSM_FLAT
            s_ref[0, r0:r0 + SSM_FLAT, d * lanes:(d + 1) * lanes] = xg_re.astype(BF16)
            s_ref[0, r0:r0 + SSM_FLAT, (2 + d) * lanes:(3 + d) * lanes] = xg_im.astype(BF16)
            kw = dot_hi(xg_re, ct_re) - dot_hi(xg_im, ct_im)
            for t in range(lc):
                if fwd:
                    sh = (lc - 1 - t) * SSM_GROUP
                    moved = kw if sh == 0 else jnp.concatenate([kw[sh:], jnp.zeros((sh, SSM_FLAT), F32)], axis=0)
                else:
                    sh = t * SSM_GROUP
                    moved = kw if sh == 0 else jnp.concatenate(
                        [jnp.zeros((sh, SSM_FLAT), F32), kw[:SSM_FLAT - sh]], axis=0)
                m_acc[gl] = m_acc[gl] + jnp.where(t_blk == t, moved, 0.0)

        lam_re_c = jnp.broadcast_to(lam_col_ref[0, :, 2 * d:2 * d + 1], (lanes, 128))
        lam_im_c = jnp.broadcast_to(lam_col_ref[0, :, 2 * d + 1:2 * d + 2], (lanes, 128))
        ac_re, ac_im = transition(lam_re_c, lam_im_c, d, 0)
        pcr, pci = powers(ac_re, ac_im, lc)
        pcr = [jnp.concatenate([p, p], axis=1) for p in pcr]
        pci = [jnp.concatenate([p, p], axis=1) for p in pci]
        pw_re = jnp.zeros_like(ct_re)
        pw_im = jnp.zeros_like(ct_re)
        for t in range(lc):
            k = t + 1 if fwd else lc - t
            pw_re = jnp.where(t_blk_w == t, pcr[k], pw_re)
            pw_im = jnp.where(t_blk_w == t, pci[k], pw_im)
        w_re = ct_re * pw_re - ct_im * pw_im
        w_im = ct_re * pw_im + ct_im * pw_re
        for gl in range(2):
            c0 = gl * SSM_FLAT
            r_ref[0, d * lanes:(d + 1) * lanes, c0:c0 + SSM_FLAT] = jnp.where(w_group == gl, w_re, 0.0).astype(BF16)
            r_ref[0, (2 + d) * lanes:(3 + d) * lanes, c0:c0 + SSM_FLAT] = jnp.where(w_group == gl, -w_im, 0.0).astype(BF16)

    for gl in range(2):
        m_ref[gl] = m_acc[gl].astype(BF16)


def _s5_operators(lam_re, lam_im, log_dt, b_re, b_im, c_re, c_im, d_skip, n_steps):
    pairs = N_GROUPS // 2
    quantities = lambda re, im: jnp.stack([re[0], im[0], re[1], im[1]], 1)
    lam_row = quantities(lam_re.reshape(2, pairs, 2 * STATE), lam_im.reshape(2, pairs, 2 * STATE))
    lam_col = lam_row.transpose(0, 2, 1)
    bt_lanes = lambda x: (x.reshape(2, pairs, 2, STATE, SSM_GROUP).transpose(0, 1, 4, 2, 3)
                          .reshape(2, pairs, SSM_GROUP, 2 * STATE))
    bt = quantities(bt_lanes(b_re), bt_lanes(b_im))
    c_lanes = lambda x: (x.reshape(2, pairs, 2, SSM_GROUP, STATE).transpose(0, 1, 3, 2, 4)
                         .reshape(2, pairs, SSM_GROUP, 2 * STATE))
    ct = quantities(c_lanes(c_re), c_lanes(c_im))
    lead = lambda a, n: pl.BlockSpec((n,) + a.shape[1:], lambda g: (g,) + (0,) * (len(a.shape) - 1))
    out_m = jax.ShapeDtypeStruct((N_GROUPS, SSM_FLAT, SSM_FLAT), BF16)
    out_sq = jax.ShapeDtypeStruct((pairs, 2 * SSM_FLAT, 2 * SSM_FLAT), BF16)
    out_ap = jax.ShapeDtypeStruct((pairs, 4, n_steps, 2 * STATE), F32)
    m_op, s_pair, r_pair, a_pair = pl.pallas_call(
        functools.partial(_s5_prep_kernel, n_steps=n_steps),
        out_shape=(out_m, out_sq, out_sq, out_ap),
        grid=(pairs,),
        in_specs=[pl.BlockSpec(memory_space=pltpu.SMEM), lead(lam_row, 1), lead(lam_col, 1), lead(bt, 1), lead(ct, 1)],
        out_specs=(lead(out_m, 2), lead(out_sq, 1), lead(out_sq, 1), lead(out_ap, 1)),
        compiler_params=_params(1),
        name="s5_prep",
    )(log_dt, lam_row, lam_col, bt, ct)
    d_row = jnp.tile(d_skip.reshape(N_GROUPS, 1, SSM_GROUP), (1, 1, SSM_CHUNK))
    return m_op, s_pair, r_pair, a_pair, d_row


def _ssm_core_kernel(*refs, seq_chunks, n_steps, pairs, has_h0, emit_state):
    refs = list(refs)
    u_ref, m_ref, s_ref, r_ref, ap_ref, d_ref = refs[:6]
    refs = refs[6:]
    h0_ref = refs.pop(0) if has_h0 else None
    y_ref = refs.pop(0)
    if emit_state:
        e_ref, fin_scr = refs
    rows = u_ref.shape[2]
    n_seq = rows // seq_chunks
    lanes = 2 * STATE
    pos = lax.broadcasted_iota(jnp.int32, (rows, lanes), 0) % seq_chunks

    def upstream(x, d, fwd):
        return pltpu.roll(x, d if fwd else rows - d, 0)

    def has_upstream(d, fwd):
        return (pos >= d) if fwd else (pos < seq_chunks - d)

    def scan(hr, hi, pi, fwd):
        q = 0 if fwd else 2
        for i in range(n_steps):
            d = 1 << i
            p_re, p_im = ap_ref[pi, q, i:i + 1, :], ap_ref[pi, q + 1, i:i + 1, :]
            if d < 8:
                ok = has_upstream(d, fwd)
                sr = jnp.where(ok, upstream(hr, d, fwd), 0.0)
                si = jnp.where(ok, upstream(hi, d, fwd), 0.0)
                hr, hi = hr + p_re * sr - p_im * si, hi + p_re * si + p_im * sr
            else:
                r3, i3 = hr.reshape(n_seq, seq_chunks, lanes), hi.reshape(n_seq, seq_chunks, lanes)
                keep = seq_chunks - d
                if fwd:
                    sr, si, tr, ti = r3[:, :keep], i3[:, :keep], r3[:, d:], i3[:, d:]
                else:
                    sr, si, tr, ti = r3[:, d:], i3[:, d:], r3[:, :keep], i3[:, :keep]
                nr = tr + p_re * sr - p_im * si
                ni = ti + p_re * si + p_im * sr
                if fwd:
                    r3 = jnp.concatenate([r3[:, :d], nr], axis=1)
                    i3 = jnp.concatenate([i3[:, :d], ni], axis=1)
                else:
                    r3 = jnp.concatenate([nr, r3[:, keep:]], axis=1)
                    i3 = jnp.concatenate([ni, i3[:, keep:]], axis=1)
                hr, hi = r3.reshape(rows, lanes), i3.reshape(rows, lanes)
        return hr, hi

    for pi in range(pairs):
        u0, u1 = u_ref[0, 2 * pi], u_ref[0, 2 * pi + 1]
        ub = jnp.concatenate([u0, u1], axis=1).astype(BF16)
        y0 = _dot(ub[:, :SSM_FLAT], m_ref[2 * pi]) + d_ref[2 * pi] * u0
        y1 = _dot(ub[:, SSM_FLAT:], m_ref[2 * pi + 1]) + d_ref[2 * pi + 1] * u1
        h = _dot(ub, s_ref[pi])
        state = {}
        for fwd, off in ((True, 0), (False, lanes)):
            hr, hi = h[:, off:off + lanes], h[:, 2 * lanes + off:3 * lanes + off]
            q = 0 if fwd else 2
            if has_h0:
                h0r = h0_ref[0, pi, 0:1, off:off + lanes]
                h0i = h0_ref[0, pi, 0:1, 2 * lanes + off:3 * lanes + off]
                a_re, a_im = ap_ref[pi, q, 0:1, :], ap_ref[pi, q + 1, 0:1, :]
                first = jnp.logical_not(has_upstream(1, fwd))
                hr = hr + jnp.where(first, a_re * h0r - a_im * h0i, 0.0)
                hi = hi + jnp.where(first, a_re * h0i + a_im * h0r, 0.0)
            else:
                h0r = h0i = 0.0
            hr, hi = scan(hr, hi, pi, fwd)
            ok = has_upstream(1, fwd)
            state[fwd] = (hr, hi, jnp.where(ok, upstream(hr, 1, fwd), h0r), jnp.where(ok, upstream(hi, 1, fwd), h0i))
        if emit_state:
            for q, (fwd, part) in enumerate(((True, 0), (False, 0), (True, 1), (False, 1))):
                fin_scr[q] = state[fwd][part]
                last = seq_chunks - 1 if fwd else 0
                e_ref[0, pi, :, q * lanes:(q + 1) * lanes] = fin_scr[q, pl.ds(last, n_seq, stride=seq_chunks), :]
        h_in = jnp.concatenate([state[True][2], state[False][2], state[True][3], state[False][3]], axis=1)
        y_state = _dot(h_in.astype(BF16), r_ref[pi])
        y_ref[0, 2 * pi] = y0 + y_state[:, :SSM_FLAT]
        y_ref[0, 2 * pi + 1] = y1 + y_state[:, SSM_FLAT:]


def _ssm_core(u_flat, pair_ops, h0, seq_chunks, emit_state):
    nb, _, rows, _ = u_flat.shape
    m_op, s_pair, r_pair, a_pair, d_row = pair_ops
    n_steps = a_pair.shape[2]
    assert (1 << n_steps) == seq_chunks
    pt = 8
    blk = pl.BlockSpec((1, 2 * pt, rows, SSM_FLAT), lambda b, g: (b, g, 0, 0))
    lead = lambda a, n: pl.BlockSpec((n,) + a.shape[1:], lambda b, g: (g,) + (0,) * (a.ndim - 1))
    in_specs = [blk, lead(m_op, 2 * pt), lead(s_pair, pt), lead(r_pair, pt), lead(a_pair, pt), lead(d_row, 2 * pt)]
    args = [u_flat, m_op, s_pair, r_pair, a_pair, d_row]
    if h0 is not None:
        in_specs.append(pl.BlockSpec((1, pt, 1, 8 * STATE), lambda b, g: (b, g, 0, 0)))
        args.append(h0)
    out_shape = [jax.ShapeDtypeStruct(u_flat.shape, F32)]
    out_specs = [blk]
    scratch = []
    if emit_state:
        n_seq = rows // seq_chunks
        out_shape.append(jax.ShapeDtypeStruct((nb, N_GROUPS // 2, n_seq, 8 * STATE), F32))
        out_specs.append(pl.BlockSpec((1, pt, n_seq, 8 * STATE), lambda b, g: (b, g, 0, 0)))
        scratch.append(pltpu.VMEM((4, rows, 2 * STATE), F32))
    kern = functools.partial(_ssm_core_kernel, seq_chunks=seq_chunks, n_steps=n_steps, pairs=pt,
                             has_h0=h0 is not None, emit_state=emit_state)
    return pl.pallas_call(
        kern, out_shape=tuple(out_shape), grid=(nb, N_GROUPS // (2 * pt)),
        in_specs=in_specs, out_specs=tuple(out_specs), scratch_shapes=scratch,
        compiler_params=_params(2), name="ssm_core",
    )(*args)


def _ssm_out_kernel(y_ref, x_ref, mod_ref, ln_ref, wg_ref, wo_ref, out_ref, y_scr):
    i = pl.program_id(0)
    n_chunks = x_ref.shape[0] // SSM_CHUNK

    @pl.when(i == 0)
    def _():
        y_scr[1] = jnp.zeros(y_scr.shape[1:], F32)

    def step(fill_slot):
        fill, ready = y_scr.at[fill_slot], y_scr.at[1 - fill_slot]
        for c in range(LANE_TILES):
            for half in range(SSM_CHUNK // LANE_BLOCKS):
                per_group = [y_ref[0, c * LANE_BLOCKS + gl, :, half * 128:(half + 1) * 128]
                             for gl in range(LANE_BLOCKS)]
                per_token = _block_transpose(per_group)
                for s in range(LANE_BLOCKS):
                    fill[c, pl.ds(half * LANE_BLOCKS + s, n_chunks, stride=SSM_CHUNK), :] = per_token[s]
        y = jnp.concatenate([ready[c] for c in range(LANE_TILES)], axis=1)
        vg = _dot(_gelu_tanh(y).astype(BF16), wg_ref[...])
        glu = (vg[:, :D_MODEL] * jax.nn.sigmoid(vg[:, D_MODEL:])).astype(BF16)
        z = DN_ALPHA * x_ref[...] + mod_ref[0, 2:3, :] * _dot(glu, wo_ref[...])
        out_ref[...] = _layer_norm(z, ln_ref[0:1, :], ln_ref[1:2, :])

    for parity in range(2):
        pl.when(i % 2 == parity)(functools.partial(step, parity))


def _ssm_out(y_flat, x2, mods, mod_base, rows_per_mod, ln, w_glu, w_out):
    r = x2.shape[0]
    nb = y_flat.shape[0]
    t = r // nb
    tm = _row_tile(math.gcd(rows_per_mod, t), SSM_ROWS)
    per_mod = rows_per_mod // tm
    per_seq = t // tm
    nj = tm // SSM_CHUNK
    n_tiles = r // tm
    fill_tile = lambda i: jnp.minimum(i, n_tiles - 1)
    done_tile = lambda i: jnp.maximum(i - 1, 0)
    row = pl.BlockSpec((tm, D_MODEL), lambda i: (done_tile(i), 0))
    return pl.pallas_call(
        _ssm_out_kernel,
        out_shape=jax.ShapeDtypeStruct((r, D_MODEL), F32),
        grid=(n_tiles + 1,),
        in_specs=[pl.BlockSpec((1, N_GROUPS, nj, SSM_FLAT),
                               lambda i: (fill_tile(i) // per_seq, 0, fill_tile(i) % per_seq, 0)),
                  row,
                  _mods_spec(mods, lambda i: mod_base + done_tile(i) // per_mod),
                  _ln_spec(ln), _layer_spec(w_glu), _layer_spec(w_out)],
        out_specs=row,
        scratch_shapes=[pltpu.VMEM((2, LANE_TILES, tm, 128), F32)],
        compiler_params=_params(1),
        name="ssm_out",
    )(y_flat, x2, mods[0], ln[0], w_glu[0], w_out[0])


def _ssm_mixer(x2, nb, seq_chunks, mods, mod_base, rows_per_mod, ln, w_in, ops, w_glu, w_out, h0, emit_state):
    u_flat = _ssm_in(x2, mods, mod_base, rows_per_mod, w_in, nb)
    outs = _ssm_core(u_flat, ops, h0, seq_chunks, emit_state)
    x_new = _ssm_out(outs[0], x2, mods, mod_base, rows_per_mod, ln, w_glu, w_out)
    return x_new, (outs[1] if emit_state else None)


def _rope_tables(t):
    n_rows = t // GRID_W
    row = jnp.repeat(jnp.arange(n_rows, dtype=F32), GRID_W)
    col = (jnp.arange(t) % GRID_W).astype(F32)
    nfreq = HEAD_DIM // 4
    inv = jnp.power(ROPE_BASE, -jnp.arange(nfreq, dtype=F32) / nfreq)
    ang_r, ang_c = row[:, None] * inv, col[:, None] * inv
    cos = jnp.concatenate([jnp.cos(ang_r)] * 2 + [jnp.cos(ang_c)] * 2, -1)
    sin = jnp.concatenate([-jnp.sin(ang_r), jnp.sin(ang_r), -jnp.sin(ang_c), jnp.sin(ang_c)], -1)
    return jnp.tile(cos, (1, 2)), jnp.tile(sin, (1, 2))


def kernel(x_prompt, x_sample, cache_k, cache_v, state_ssm_re, state_ssm_im, c, c_ctx, w_ada, b_ada, ln_g, ln_b, w_qkv, w_o, attn_sink, ssm_w_in, ssm_lam_re, ssm_lam_im, ssm_log_dt, ssm_b_re, ssm_b_im, ssm_c_re, ssm_c_im, ssm_d, ssm_w_glu, ssm_w_out, ffn_w1, ffn_w3, ffn_w2):
    nc, tc, _ = x_prompt.shape
    nl, tl, _ = x_sample.shape
    n_attn = w_qkv.shape[0]
    n_ssm = ssm_w_in.shape[0]

    n_mod_rows = -(-(1 + nl) // 8) * 8
    cvec = jnp.zeros((n_mod_rows, D_MODEL), F32).at[0].set(c_ctx).at[1:1 + nl].set(c)
    mods_all = _ada_mods(cvec, w_ada, b_ada)
    ln_all = jnp.stack([ln_g, ln_b], axis=2)

    bf = lambda w: w.astype(BF16)
    w_v = w_qkv[:, :, Q_WIDTH + KV_WIDTH:].reshape(n_attn, D_MODEL, N_KV_HEADS, HEAD_DIM)
    w_v_ext = jnp.pad(w_v, ((0, 0), (0, 0), (0, 0), (0, 128 - HEAD_DIM))).reshape(n_attn, D_MODEL, V_EXT)
    w_qkv_ctx = bf(jnp.concatenate([w_qkv, w_v_ext], axis=-1))
    w_qkv_lat = bf(jnp.concatenate([w_qkv[:, :, :Q_WIDTH + KV_WIDTH], w_v_ext], axis=-1))
    w_o_b = bf(w_o)
    w_in_b, w_glu_b, w_out_b = bf(ssm_w_in), bf(ssm_w_glu), bf(ssm_w_out)
    w1_b, w3_b, w2_b = bf(ffn_w1), bf(ffn_w3), bf(ffn_w2)
    cos, sin = _rope_tables(tl)
    kx = bf(cache_k).reshape(nl, n_attn, -1, KV_WIDTH)
    vx = jnp.concatenate([bf(cache_v), jnp.ones(cache_v.shape[:-1] + (128 - HEAD_DIM,), BF16)], -1)
    vx = vx.reshape(nl, n_attn, -1, V_EXT)

    steps_c = (tc // SSM_CHUNK).bit_length() - 1
    steps_l = (tl // SSM_CHUNK).bit_length() - 1
    ssm_ops = []
    for s in range(n_ssm):
        prm = (ssm_lam_re[s], ssm_lam_im[s], ssm_log_dt[s], ssm_b_re[s], ssm_b_im[s],
               ssm_c_re[s], ssm_c_im[s], ssm_d[s])
        ssm_ops.append(_s5_operators(*prm, max(steps_c, steps_l)))

    def ops_for(s, n_steps):
        m_op, s_pair, r_pair, a_pair, d_row = ssm_ops[s]
        return m_op, s_pair, r_pair, a_pair[:, :, :n_steps], d_row

    xc = x_prompt.reshape(nc * tc, D_MODEL)
    xl = x_sample.reshape(nl * tl, D_MODEL)
    new_k, new_v, new_sr, new_si = [], [], [], []
    for l in range(DEPTH):
        mods = (mods_all, l)
        ln1, ln2 = (ln_all, l, 0), (ln_all, l, 1)
        if l % 2 == 0:
            a = l // 2
            q, kb, vb, k32, v32 = _qkv_ctx(xc, mods, (w_qkv_ctx, a))
            new_k.append(k32.reshape(nc, tc, N_KV_HEADS, HEAD_DIM))
            new_v.append(v32.reshape(nc, tc, N_KV_HEADS, HEAD_DIM))
            xc = _attn_ctx(attn_sink[a], q, kb, vb, xc, mods, ln1, (w_o_b, a), tc)
            qp, qr, kr, v = _qkv_lat(xl, mods, (w_qkv_lat, a), cos, sin, tl)
            xl = _attn_lat(attn_sink[a], qp, qr, kr, v, (kx, a), (vx, a), xl, mods, ln1, (w_o_b, a), nl, tl)
        else:
            s = l // 2
            xc, e_c = _ssm_mixer(xc, 1, tc // SSM_CHUNK, mods, 0, nc * tc, ln1, (w_in_b, s),
                                 ops_for(s, steps_c), (w_glu_b, s), (w_out_b, s), None, True)
            fin = e_c.reshape(N_GROUPS // 2, nc, 4, 2, STATE).transpose(2, 1, 0, 3, 4).reshape(4, nc, N_GROUPS, STATE)
            new_sr.append(jnp.stack([fin[0], fin[1]], 1))
            new_si.append(jnp.stack([fin[2], fin[3]], 1))
            h0 = jnp.stack([state_ssm_re[:, s, 0], state_ssm_re[:, s, 1],
                            state_ssm_im[:, s, 0], state_ssm_im[:, s, 1]], 1)
            h0 = h0.reshape(nl, 4, N_GROUPS // 2, 2 * STATE).transpose(0, 2, 1, 3)
            h0 = h0.reshape(nl, N_GROUPS // 2, 8 * STATE)
            xl, _ = _ssm_mixer(xl, nl, tl // SSM_CHUNK, mods, 1, tl, ln1, (w_in_b, s),
                               ops_for(s, steps_l), (w_glu_b, s), (w_out_b, s), h0[:, :, None, :], False)
        xc = _ffn(xc, mods, 0, nc * tc, ln2, (w1_b, l), (w3_b, l), (w2_b, l))
        xl = _ffn(xl, mods, 1, tl, ln2, (w1_b, l), (w3_b, l), (w2_b, l))

    return (xc.reshape(nc, tc, D_MODEL), xl.reshape(nl, tl, D_MODEL),
            jnp.stack(new_k, axis=1), jnp.stack(new_v, axis=1),
            jnp.stack(new_sr, axis=1), jnp.stack(new_si, axis=1))
```

```python
import functools
import math

import jax
import jax.numpy as jnp
from jax import lax
from jax.experimental import pallas as pl
from jax.experimental.pallas import tpu as pltpu

F32 = jnp.float32
BF16 = jnp.bfloat16

D_MODEL = 1024
DEPTH = 4
N_HEADS = 16
N_KV_HEADS = 4
HEAD_DIM = 64
HEADS_PER_KV = N_HEADS // N_KV_HEADS
Q_WIDTH = N_HEADS * HEAD_DIM
KV_WIDTH = N_KV_HEADS * HEAD_DIM
GRID_W = 64
ATT_BLOCK = 128
ROPE_BASE = 10000.0
SSM_GROUP = 16
N_GROUPS = D_MODEL // SSM_GROUP
STATE = 64
SSM_CHUNK = 16
SSM_FLAT = SSM_CHUNK * SSM_GROUP
D_FF = -(-8 * D_MODEL // (3 * 256)) * 256
FF_CHUNK = 256
FFN_ROWS = 1024
FFN_SPLIT = 2
DN_ALPHA = (2 * DEPTH) ** 0.25
LN_EPS = 1e-5
NEG_INF = -1e30
N_MODS = 6
VMEM_LIMIT = 56 * 1024 * 1024
HIGHEST = lax.Precision.HIGHEST


def _params(n_axes):
    return pltpu.CompilerParams(dimension_semantics=("arbitrary",) * n_axes,
                                vmem_limit_bytes=VMEM_LIMIT)


def _layer_spec(param, single=False):
    stacked, layer = param
    rest = stacked.shape[1:]
    mode = dict(pipeline_mode=pl.Buffered(1)) if single else {}
    return pl.BlockSpec((None,) + rest, lambda *_: (layer,) + (0,) * len(rest), **mode)


def _mods_spec(mods, row_of):
    return pl.BlockSpec((None, 1, N_MODS, D_MODEL), lambda *ids: (mods[1], row_of(*ids), 0, 0))


def _ln_spec(ln):
    return pl.BlockSpec((None, None, 2, D_MODEL), lambda *_: (ln[1], ln[2], 0, 0))


def _layer_norm(y, g, b):
    mu = jnp.mean(y, -1, keepdims=True)
    d = y - mu
    var = jnp.mean(d * d, -1, keepdims=True)
    return d * lax.rsqrt(var + LN_EPS) * g + b


def _silu(x):
    return x * jax.nn.sigmoid(x)


def _gelu_tanh(x):
    return 0.5 * x * (1.0 + jnp.tanh(math.sqrt(2.0 / math.pi) * (x + 0.044715 * (x * x * x))))


def _dot(a, b):
    return jnp.dot(a, b, preferred_element_type=F32)


def _dot_nt(a, b):
    return lax.dot_general(a, b, (((1,), (1,)), ((), ())), preferred_element_type=F32)


def _row_tile(rows_per_mod, want):
    tm = min(want, rows_per_mod)
    assert rows_per_mod % tm == 0
    return tm


def _mods_kernel(c_ref, w_ref, b_ref, o_ref):
    a = _silu(c_ref[...]).astype(BF16)
    o_ref[0] = _dot(a, w_ref[0].astype(BF16)) + b_ref[0]


def _ada_mods(cvec, w_ada, b_ada):
    r = cvec.shape[0]
    tn = 1536
    out = pl.pallas_call(
        _mods_kernel,
        out_shape=jax.ShapeDtypeStruct((DEPTH, r, N_MODS * D_MODEL), F32),
        grid=(DEPTH, N_MODS * D_MODEL // tn),
        in_specs=[pl.BlockSpec((r, D_MODEL), lambda l, j: (0, 0)),
                  pl.BlockSpec((1, D_MODEL, tn), lambda l, j: (l, 0, j)),
                  pl.BlockSpec((1, 1, tn), lambda l, j: (l, 0, j))],
        out_specs=pl.BlockSpec((1, r, tn), lambda l, j: (l, 0, j)),
        compiler_params=_params(2),
        name="ada_mods",
    )(cvec, w_ada, b_ada.reshape(DEPTH, 1, N_MODS * D_MODEL))
    return out.reshape(DEPTH, r, N_MODS, D_MODEL)


Q_SCALE = HEAD_DIM ** -0.5 * math.log2(math.e)
V_EXT = N_KV_HEADS * 128


def _rope(x, cos, sin, first_half):
    outs = []
    for c in range(x.shape[1] // 128):
        xc = x[:, c * 128:(c + 1) * 128]
        partner = jnp.where(first_half, pltpu.roll(xc, 128 - 16, 1), pltpu.roll(xc, 16, 1))
        outs.append(xc * cos + partner * sin)
    return jnp.concatenate(outs, axis=1)


def _ones_lanes(width):
    return jnp.where((lax.broadcasted_iota(jnp.int32, (1, width), 1) % 128) >= HEAD_DIM, 1.0, 0.0)


def _qkv_ctx_kernel(x_ref, mod_ref, w_ref, q_ref, kb_ref, vb_ref, k_ref, v_ref):
    h = (x_ref[...] * (1.0 + mod_ref[0, 1:2, :]) + mod_ref[0, 0:1, :]).astype(BF16)
    qkv = _dot(h, w_ref[...])
    q_ref[...] = (qkv[:, :Q_WIDTH] * Q_SCALE).astype(BF16)
    k = qkv[:, Q_WIDTH:Q_WIDTH + KV_WIDTH]
    k_ref[...] = k
    v_ref[...] = qkv[:, Q_WIDTH + KV_WIDTH:Q_WIDTH + 2 * KV_WIDTH]
    kb_ref[...] = k.astype(BF16)
    vb_ref[...] = (qkv[:, Q_WIDTH + 2 * KV_WIDTH:] + _ones_lanes(V_EXT)).astype(BF16)


def _qkv_lat_kernel(x_ref, mod_ref, w_ref, cos_ref, sin_ref, qp_ref, qr_ref, kr_ref, v_ref):
    h = (x_ref[...] * (1.0 + mod_ref[0, 1:2, :]) + mod_ref[0, 0:1, :]).astype(BF16)
    qkv = _dot(h, w_ref[...])
    cos = cos_ref[...]
    sin = sin_ref[...]
    first_half = (lax.broadcasted_iota(jnp.int32, cos.shape, 1) & 16) == 0
    q = qkv[:, :Q_WIDTH] * Q_SCALE
    qp_ref[...] = q.astype(BF16)
    qr_ref[...] = _rope(q, cos, sin, first_half).astype(BF16)
    kr_ref[...] = _rope(qkv[:, Q_WIDTH:Q_WIDTH + KV_WIDTH], cos, sin, first_half).astype(BF16)
    v_ref[...] = (qkv[:, Q_WIDTH + KV_WIDTH:] + _ones_lanes(V_EXT)).astype(BF16)


def _qkv_ctx(x2, mods, w_ext):
    r = x2.shape[0]
    tm = _row_tile(r, 512)
    row = lambda w: pl.BlockSpec((tm, w), lambda i: (i, 0))
    return pl.pallas_call(
        _qkv_ctx_kernel,
        out_shape=(jax.ShapeDtypeStruct((r, Q_WIDTH), BF16),
                   jax.ShapeDtypeStruct((r, KV_WIDTH), BF16),
                   jax.ShapeDtypeStruct((r, V_EXT), BF16),
                   jax.ShapeDtypeStruct((r, KV_WIDTH), F32),
                   jax.ShapeDtypeStruct((r, KV_WIDTH), F32)),
        grid=(r // tm,),
        in_specs=[row(D_MODEL), _mods_spec(mods, lambda i: 0), _layer_spec(w_ext)],
        out_specs=(row(Q_WIDTH), row(KV_WIDTH), row(V_EXT), row(KV_WIDTH), row(KV_WIDTH)),
        compiler_params=_params(1),
        name="qkv_ctx",
    )(x2, mods[0], w_ext[0])


def _qkv_lat(x2, mods, w_ext, cos, sin, t):
    r = x2.shape[0]
    tm = _row_tile(t, 1024)
    per_seq = t // tm
    row = lambda w: pl.BlockSpec((tm, w), lambda i: (i, 0))
    tab = pl.BlockSpec((tm, 128), lambda i: (i % per_seq, 0))
    return pl.pallas_call(
        _qkv_lat_kernel,
        out_shape=(jax.ShapeDtypeStruct((r, Q_WIDTH), BF16),
                   jax.ShapeDtypeStruct((r, Q_WIDTH), BF16),
                   jax.ShapeDtypeStruct((r, KV_WIDTH), BF16),
                   jax.ShapeDtypeStruct((r, V_EXT), BF16)),
        grid=(r // tm,),
        in_specs=[row(D_MODEL), _mods_spec(mods, lambda i: 1 + i // per_seq), _layer_spec(w_ext), tab, tab],
        out_specs=(row(Q_WIDTH), row(Q_WIDTH), row(KV_WIDTH), row(V_EXT)),
        compiler_params=_params(1),
        name="qkv_lat",
    )(x2, mods[0], w_ext[0], cos, sin)


def _attn_epilogue(o_scr, x_ref, mod_ref, ln_ref, wo_ref, out_ref):
    y = _dot(o_scr[...], wo_ref[...])
    z = DN_ALPHA * x_ref[...] + mod_ref[0, 2:3, :] * y
    out_ref[...] = _layer_norm(z, ln_ref[0:1, :], ln_ref[1:2, :])


def _attn_ctx_kernel(sink_ref, q_ref, k_ref, v_ref, x_ref, mod_ref, ln_ref, wo_ref, out_ref, o_scr, s_scr, m_scr):
    rows = q_ref.shape[0]
    n_tiles = k_ref.shape[0] // 128

    def scores(kv, slot):
        k = k_ref[:, kv * HEAD_DIM:(kv + 1) * HEAD_DIM]
        for g in range(HEADS_PER_KV):
            h = kv * HEADS_PER_KV + g
            s = _dot_nt(q_ref[:, h * HEAD_DIM:(h + 1) * HEAD_DIM], k)
            m = jnp.maximum(jnp.max(s, -1, keepdims=True), sink_ref[h] * math.log2(math.e))
            s_scr[slot, g * rows:(g + 1) * rows, :] = s
            m_scr[slot, g * rows:(g + 1) * rows, :] = jnp.broadcast_to(m, (rows, 128))

    def values(kv, slot):
        v = v_ref[:, kv * 128:(kv + 1) * 128]
        for g in range(HEADS_PER_KV):
            h = kv * HEADS_PER_KV + g
            r0, r1 = g * rows, (g + 1) * rows
            m = m_scr[slot, r0:r1, :]
            p = [jnp.exp2(s_scr[slot, r0:r1, c * 128:(c + 1) * 128] - m).astype(BF16) for c in range(n_tiles)]
            o_ext = _dot(jnp.concatenate(p, axis=1), v)
            den = pltpu.roll(o_ext, HEAD_DIM, 1) + jnp.exp2(sink_ref[h] * math.log2(math.e) - m)
            o_scr[:, h * HEAD_DIM:(h + 1) * HEAD_DIM] = (o_ext / den)[:, :HEAD_DIM].astype(BF16)

    scores(0, 0)
    for kv in range(N_KV_HEADS):
        if kv + 1 < N_KV_HEADS:
            scores(kv + 1, (kv + 1) % 2)
        values(kv, kv % 2)
    _attn_epilogue(o_scr, x_ref, mod_ref, ln_ref, wo_ref, out_ref)


def _attn_ctx(sink, q, k, v_ext, x2, mods, ln, w_o, seq):
    r = x2.shape[0]
    row = lambda w: pl.BlockSpec((seq, w), lambda i: (i, 0))
    return pl.pallas_call(
        _attn_ctx_kernel,
        out_shape=jax.ShapeDtypeStruct((r, D_MODEL), F32),
        grid=(r // seq,),
        in_specs=[pl.BlockSpec(memory_space=pltpu.SMEM),
                  row(Q_WIDTH), row(KV_WIDTH), row(V_EXT), row(D_MODEL),
                  _mods_spec(mods, lambda i: 0), _ln_spec(ln), _layer_spec(w_o)],
        out_specs=row(D_MODEL),
        scratch_shapes=[pltpu.VMEM((seq, Q_WIDTH), BF16),
                        pltpu.VMEM((2, HEADS_PER_KV * seq, seq), F32),
                        pltpu.VMEM((2, HEADS_PER_KV * seq, 128), F32)],
        compiler_params=_params(1),
        name="attn_ctx",
    )(sink, q, k, v_ext, x2, mods[0], ln[0], w_o[0])


LOCAL_KEYS = 3 * ATT_BLOCK
ATT_STEP_BLOCKS = 4


def _attn_lat_kernel(sink_ref, qp_ref, qr_ref, k_ref, v_ref, kx_ref, vx_ref, x_ref, mod_ref, ln_ref,
                     wo_ref, out_ref, o_scr, s_scr, m_scr, bias_scr):
    rows = ATT_BLOCK
    seq = k_ref.shape[0]
    n_ctx = kx_ref.shape[1]
    n_tiles = (n_ctx + LOCAL_KEYS) // 128
    n_sub = qp_ref.shape[0] // rows
    r_idx = lax.broadcasted_iota(jnp.int32, (rows, LOCAL_KEYS), 0)
    c_idx = lax.broadcasted_iota(jnp.int32, (rows, LOCAL_KEYS), 1)
    starts = []
    for sub in range(n_sub):
        j = pl.program_id(1) * n_sub + sub
        start = pl.multiple_of(jnp.clip((j - 1) * rows, 0, seq - LOCAL_KEYS), rows)
        bias_scr[sub] = jnp.where(jnp.abs(r_idx - c_idx + (j * rows - start)) <= ATT_BLOCK, 0.0, NEG_INF)
        starts.append(start)

    def scores(sub, kv, slot):
        lo, hi = kv * HEAD_DIM, (kv + 1) * HEAD_DIM
        q0 = sub * rows
        kx = kx_ref[0, :, lo:hi]
        kl = k_ref[pl.ds(starts[sub], LOCAL_KEYS), lo:hi]
        for g in range(HEADS_PER_KV):
            h = kv * HEADS_PER_KV + g
            r0, r1 = g * rows, (g + 1) * rows
            s_x = _dot_nt(qp_ref[q0:q0 + rows, h * HEAD_DIM:(h + 1) * HEAD_DIM], kx)
            s_l = _dot_nt(qr_ref[q0:q0 + rows, h * HEAD_DIM:(h + 1) * HEAD_DIM], kl) + bias_scr[sub]
            m = jnp.maximum(jnp.maximum(jnp.max(s_x, -1, keepdims=True), jnp.max(s_l, -1, keepdims=True)),
                            sink_ref[h] * math.log2(math.e))
            s_scr[slot, r0:r1, :n_ctx] = s_x
            s_scr[slot, r0:r1, n_ctx:] = s_l
            m_scr[slot, r0:r1, :] = jnp.broadcast_to(m, (rows, 128))

    def values(sub, kv, slot):
        q0 = sub * rows
        vx = vx_ref[0, :, kv * 128:(kv + 1) * 128]
        vl = v_ref[pl.ds(starts[sub], LOCAL_KEYS), kv * 128:(kv + 1) * 128]
        for g in range(HEADS_PER_KV):
            h = kv * HEADS_PER_KV + g
            r0, r1 = g * rows, (g + 1) * rows
            m = m_scr[slot, r0:r1, :]
            p = [jnp.exp2(s_scr[slot, r0:r1, c * 128:(c + 1) * 128] - m).astype(BF16) for c in range(n_tiles)]
            o_ext = (_dot(jnp.concatenate(p[:n_ctx // 128], axis=1), vx)
                     + _dot(jnp.concatenate(p[n_ctx // 128:], axis=1), vl))
            den = pltpu.roll(o_ext, HEAD_DIM, 1) + jnp.exp2(sink_ref[h] * math.log2(math.e) - m)
            o_scr[q0:q0 + rows, h * HEAD_DIM:(h + 1) * HEAD_DIM] = (o_ext / den)[:, :HEAD_DIM].astype(BF16)

    units = [(sub, kv) for sub in range(n_sub) for kv in range(N_KV_HEADS)]
    scores(*units[0], 0)
    for u, unit in enumerate(units):
        values(*unit, u % 2)
        if u + 1 < len(units):
            scores(*units[u + 1], (u + 1) % 2)
    _attn_epilogue(o_scr, x_ref, mod_ref, ln_ref, wo_ref, out_ref)


def _attn_lat(sink, qp, qr, kr, v_ext, k_ctx, v_ctx_ext, x2, mods, ln, w_o, n, t):
    assert t >= LOCAL_KEYS
    blk = ATT_STEP_BLOCKS * ATT_BLOCK
    nblk = t // blk
    row = lambda w: pl.BlockSpec((blk, w), lambda b, j: (b * nblk + j, 0))
    seq = lambda w: pl.BlockSpec((t, w), lambda b, j: (b, 0))
    ctx = lambda a: pl.BlockSpec((1, None) + a[0].shape[2:], lambda b, j: (b, a[1], 0, 0))
    return pl.pallas_call(
        _attn_lat_kernel,
        out_shape=jax.ShapeDtypeStruct((n * t, D_MODEL), F32),
        grid=(n, nblk),
        in_specs=[pl.BlockSpec(memory_space=pltpu.SMEM),
                  row(Q_WIDTH), row(Q_WIDTH), seq(KV_WIDTH), seq(V_EXT),
                  ctx(k_ctx), ctx(v_ctx_ext), row(D_MODEL),
                  _mods_spec(mods, lambda b, j: 1 + b), _ln_spec(ln), _layer_spec(w_o)],
        out_specs=row(D_MODEL),
        scratch_shapes=[pltpu.VMEM((blk, Q_WIDTH), BF16),
                        pltpu.VMEM((2, HEADS_PER_KV * ATT_BLOCK, k_ctx[0].shape[2] + LOCAL_KEYS), F32),
                        pltpu.VMEM((2, HEADS_PER_KV * ATT_BLOCK, 128), F32),
                        pltpu.VMEM((ATT_STEP_BLOCKS, ATT_BLOCK, LOCAL_KEYS), F32)],
        compiler_params=_params(2),
        name="attn_lat",
    )(sink, qp, qr, kr, v_ext, k_ctx[0], v_ctx_ext[0], x2, mods[0], ln[0], w_o[0])


def _ffn_kernel(x_ref, mod_ref, ln_ref, w1_ref, w3_ref, w2_ref, out_ref, acc_ref):
    half = x_ref.shape[0] // FFN_SPLIT
    for part in range(FFN_SPLIT):
        r0, r1 = part * half, (part + 1) * half
        x = x_ref[r0:r1, :]
        h = (x * (1.0 + mod_ref[0, 4:5, :]) + mod_ref[0, 3:4, :]).astype(BF16)
        for c in range(D_FF // FF_CHUNK):
            lo, hi = c * FF_CHUNK, (c + 1) * FF_CHUNK
            a = _dot(h, w1_ref[:, lo:hi])
            b = _dot(h, w3_ref[:, lo:hi])
            y = _dot((_silu(a) * b).astype(BF16), w2_ref[lo:hi, :])
            if c == 0:
                acc_ref[r0:r1, :] = y
            else:
                acc_ref[r0:r1, :] += y
        z = DN_ALPHA * x + mod_ref[0, 5:6, :] * acc_ref[r0:r1, :]
        out_ref[r0:r1, :] = _layer_norm(z, ln_ref[0:1, :], ln_ref[1:2, :])


def _ffn(x2, mods, mod_base, rows_per_mod, ln, w1, w3, w2):
    r = x2.shape[0]
    tm = _row_tile(rows_per_mod, FFN_ROWS)
    per_mod = rows_per_mod // tm
    return pl.pallas_call(
        _ffn_kernel,
        out_shape=jax.ShapeDtypeStruct((r, D_MODEL), F32),
        grid=(r // tm,),
        in_specs=[pl.BlockSpec((tm, D_MODEL), lambda i: (i, 0)),
                  _mods_spec(mods, lambda i: mod_base + i // per_mod),
                  _ln_spec(ln), _layer_spec(w1, single=True), _layer_spec(w3, single=True),
                  _layer_spec(w2, single=True)],
        out_specs=pl.BlockSpec((tm, D_MODEL), lambda i: (i, 0)),
        scratch_shapes=[pltpu.VMEM((tm, D_MODEL), F32)],
        compiler_params=_params(1),
        name="ffn",
    )(x2, mods[0], ln[0], w1[0], w3[0], w2[0])


LANE_BLOCKS = 128 // SSM_GROUP
LANE_TILES = D_MODEL // 128
SSM_ROWS = 1024


def _block_transpose(xs):
    xs = list(xs)
    blk = lax.broadcasted_iota(jnp.int32, xs[0].shape, 1) // SSM_GROUP
    k = LANE_BLOCKS // 2
    while k >= 1:
        keep_lo = (blk & k) == 0
        for i in range(LANE_BLOCKS):
            if i & k:
                continue
            lo, hi = xs[i], xs[i + k]
            xs[i] = jnp.where(keep_lo, lo, pltpu.roll(hi, SSM_GROUP * k, 1))
            xs[i + k] = jnp.where(keep_lo, pltpu.roll(lo, 128 - SSM_GROUP * k, 1), hi)
        k //= 2
    return xs


def _ssm_in_kernel(x_ref, mod_ref, w_ref, o_ref, u_scr):
    h = (x_ref[...] * (1.0 + mod_ref[0, 1:2, :]) + mod_ref[0, 0:1, :]).astype(BF16)
    u = _dot(h, w_ref[...])
    for c in range(LANE_TILES):
        u_scr[c] = u[:, c * 128:(c + 1) * 128]
    n_chunks = x_ref.shape[0] // SSM_CHUNK
    for c in range(LANE_TILES):
        for half in range(SSM_CHUNK // LANE_BLOCKS):
            per_token = [u_scr[c, pl.ds(half * LANE_BLOCKS + s, n_chunks, stride=SSM_CHUNK), :]
                         for s in range(LANE_BLOCKS)]
            per_group = _block_transpose(per_token)
            for gl in range(LANE_BLOCKS):
                o_ref[0, c * LANE_BLOCKS + gl, :, half * 128:(half + 1) * 128] = per_group[gl]


def _ssm_in(x2, mods, mod_base, rows_per_mod, w, nb):
    r = x2.shape[0]
    t = r // nb
    tm = _row_tile(math.gcd(rows_per_mod, t), SSM_ROWS)
    per_mod = rows_per_mod // tm
    per_seq = t // tm
    nj = tm // SSM_CHUNK
    return pl.pallas_call(
        _ssm_in_kernel,
        out_shape=jax.ShapeDtypeStruct((nb, N_GROUPS, t // SSM_CHUNK, SSM_FLAT), F32),
        grid=(r // tm,),
        in_specs=[pl.BlockSpec((tm, D_MODEL), lambda i: (i, 0)),
                  _mods_spec(mods, lambda i: mod_base + i // per_mod), _layer_spec(w)],
        out_specs=pl.BlockSpec((1, N_GROUPS, nj, SSM_FLAT), lambda i: (i // per_seq, 0, i % per_seq, 0)),
        scratch_shapes=[pltpu.VMEM((LANE_TILES, tm, 128), F32)],
        compiler_params=_params(1),
        name="ssm_in",
    )(x2, mods[0], w[0])


def _s5_prep_kernel(logdt_ref, lam_row_ref, lam_col_ref, bt_ref, ct_ref,
                    m_ref, s_ref, r_ref, ap_ref, *, n_steps):
    pair = pl.program_id(0)
    lc = SSM_CHUNK
    lanes = 2 * STATE

    def transition(lam_re, lam_im, d, axis):
        first = lax.broadcasted_iota(jnp.int32, lam_re.shape, axis) < STATE
        log_dt = jnp.where(first, jnp.full(lam_re.shape, logdt_ref[d, 2 * pair], F32),
                           jnp.full(lam_re.shape, logdt_ref[d, 2 * pair + 1], F32))
        dt = jnp.exp(log_dt)
        mag = jnp.exp(lam_re * dt)
        return mag * jnp.cos(lam_im * dt), mag * jnp.sin(lam_im * dt)

    def powers(a_re, a_im, n):
        pr, pi = [jnp.ones_like(a_re)], [jnp.zeros_like(a_im)]
        for _ in range(n):
            r, i = pr[-1], pi[-1]
            pr.append(r * a_re - i * a_im)
            pi.append(r * a_im + i * a_re)
        return pr, pi

    dot_hi = lambda a, b: jnp.dot(a, b, precision=HIGHEST, preferred_element_type=F32)
    row_shape = (SSM_GROUP, lanes)
    x_group = lax.broadcasted_iota(jnp.int32, (SSM_FLAT, lanes), 1) // STATE
    w_group = lax.broadcasted_iota(jnp.int32, (lanes, SSM_FLAT), 0) // STATE
    t_blk = lax.broadcasted_iota(jnp.int32, (SSM_FLAT, SSM_FLAT), 1) // SSM_GROUP
    t_blk_w = lax.broadcasted_iota(jnp.int32, (lanes, SSM_FLAT), 1) // SSM_GROUP
    m_acc = [jnp.zeros((SSM_FLAT, SSM_FLAT), F32) for _ in range(2)]
    spread = jnp.where(lax.broadcasted_iota(jnp.int32, (SSM_GROUP, SSM_FLAT), 0)
                       == lax.broadcasted_iota(jnp.int32, (SSM_GROUP, SSM_FLAT), 1) % SSM_GROUP, 1.0, 0.0)

    for d in range(2):
        fwd = d == 0
        lam_re = jnp.broadcast_to(lam_row_ref[0, 2 * d:2 * d + 1, :], row_shape)
        lam_im = jnp.broadcast_to(lam_row_ref[0, 2 * d + 1:2 * d + 2, :], row_shape)
        a_re, a_im = transition(lam_re, lam_im, d, 1)
        den = lam_re * lam_re + lam_im * lam_im
        nr, ni = a_re - 1.0, a_im
        coef_re = (nr * lam_re + ni * lam_im) / den
        coef_im = (ni * lam_re - nr * lam_im) / den
        bt_re, bt_im = bt_ref[0, 2 * d], bt_ref[0, 2 * d + 1]
        bb_re = coef_re * bt_re - coef_im * bt_im
        bb_im = coef_re * bt_im + coef_im * bt_re
        pr, pi = powers(a_re, a_im, lc)
        order = [lc - 1 - s for s in range(lc)] if fwd else list(range(lc))
        x_re = jnp.concatenate([pr[k] * bb_re - pi[k] * bb_im for k in order], axis=0)
        x_im = jnp.concatenate([pr[k] * bb_im + pi[k] * bb_re for k in order], axis=0)
        sq_re, sq_im = pr[lc], pi[lc]
        for i in range(n_steps):
            ap_ref[0, 2 * d, i:i + 1, :] = sq_re[0:1, :]
            ap_ref[0, 2 * d + 1, i:i + 1, :] = sq_im[0:1, :]
            sq_re, sq_im = sq_re * sq_re - sq_im * sq_im, 2.0 * sq_re * sq_im
        ct_re = lax.dot_general(ct_ref[0, 2 * d], spread, (((0,), (0,)), ((), ())), precision=HIGHEST,
                                preferred_element_type=F32)
        ct_im = lax.dot_general(ct_ref[0, 2 * d + 1], spread, (((0,), (0,)), ((), ())), precision=HIGHEST,
                                preferred_element_type=F32)
        for gl in range(2):
            xg_re = jnp.where(x_group == gl, x_re, 0.0)
            xg_im = jnp.where(x_group == gl, x_im, 0.0)
            r0 = gl * SSM_FLAT
            s_ref[0, r0:r0 + SSM_FLAT, d * lanes:(d + 1) * lanes] = xg_re.astype(BF16)
            s_ref[0, r0:r0 + SSM_FLAT, (2 + d) * lanes:(3 + d) * lanes] = xg_im.astype(BF16)
            kw = dot_hi(xg_re, ct_re) - dot_hi(xg_im, ct_im)
            for t in range(lc):
                if fwd:
                    sh = (lc - 1 - t) * SSM_GROUP
                    moved = kw if sh == 0 else jnp.concatenate([kw[sh:], jnp.zeros((sh, SSM_FLAT), F32)], axis=0)
                else:
                    sh = t * SSM_GROUP
                    moved = kw if sh == 0 else jnp.concatenate(
                        [jnp.zeros((sh, SSM_FLAT), F32), kw[:SSM_FLAT - sh]], axis=0)
                m_acc[gl] = m_acc[gl] + jnp.where(t_blk == t, moved, 0.0)

        lam_re_c = jnp.broadcast_to(lam_col_ref[0, :, 2 * d:2 * d + 1], (lanes, 128))
        lam_im_c = jnp.broadcast_to(lam_col_ref[0, :, 2 * d + 1:2 * d + 2], (lanes, 128))
        ac_re, ac_im = transition(lam_re_c, lam_im_c, d, 0)
        pcr, pci = powers(ac_re, ac_im, lc)
        pcr = [jnp.concatenate([p, p], axis=1) for p in pcr]
        pci = [jnp.concatenate([p, p], axis=1) for p in pci]
        pw_re = jnp.zeros_like(ct_re)
        pw_im = jnp.zeros_like(ct_re)
        for t in range(lc):
            k = t + 1 if fwd else lc - t
            pw_re = jnp.where(t_blk_w == t, pcr[k], pw_re)
            pw_im = jnp.where(t_blk_w == t, pci[k], pw_im)
        w_re = ct_re * pw_re - ct_im * pw_im
        w_im = ct_re * pw_im + ct_im * pw_re
        for gl in range(2):
            c0 = gl * SSM_FLAT
            r_ref[0, d * lanes:(d + 1) * lanes, c0:c0 + SSM_FLAT] = jnp.where(w_group == gl, w_re, 0.0).astype(BF16)
            r_ref[0, (2 + d) * lanes:(3 + d) * lanes, c0:c0 + SSM_FLAT] = jnp.where(w_group == gl, -w_im, 0.0).astype(BF16)

    for gl in range(2):
        m_ref[gl] = m_acc[gl].astype(BF16)


def _s5_operators(lam_re, lam_im, log_dt, b_re, b_im, c_re, c_im, d_skip, n_steps):
    pairs = N_GROUPS // 2
    quantities = lambda re, im: jnp.stack([re[0], im[0], re[1], im[1]], 1)
    lam_row = quantities(lam_re.reshape(2, pairs, 2 * STATE), lam_im.reshape(2, pairs, 2 * STATE))
    lam_col = lam_row.transpose(0, 2, 1)
    bt_lanes = lambda x: (x.reshape(2, pairs, 2, STATE, SSM_GROUP).transpose(0, 1, 4, 2, 3)
                          .reshape(2, pairs, SSM_GROUP, 2 * STATE))
    bt = quantities(bt_lanes(b_re), bt_lanes(b_im))
    c_lanes = lambda x: (x.reshape(2, pairs, 2, SSM_GROUP, STATE).transpose(0, 1, 3, 2, 4)
                         .reshape(2, pairs, SSM_GROUP, 2 * STATE))
    ct = quantities(c_lanes(c_re), c_lanes(c_im))
    lead = lambda a, n: pl.BlockSpec((n,) + a.shape[1:], lambda g: (g,) + (0,) * (len(a.shape) - 1))
    out_m = jax.ShapeDtypeStruct((N_GROUPS, SSM_FLAT, SSM_FLAT), BF16)
    out_sq = jax.ShapeDtypeStruct((pairs, 2 * SSM_FLAT, 2 * SSM_FLAT), BF16)
    out_ap = jax.ShapeDtypeStruct((pairs, 4, n_steps, 2 * STATE), F32)
    m_op, s_pair, r_pair, a_pair = pl.pallas_call(
        functools.partial(_s5_prep_kernel, n_steps=n_steps),
        out_shape=(out_m, out_sq, out_sq, out_ap),
        grid=(pairs,),
        in_specs=[pl.BlockSpec(memory_space=pltpu.SMEM), lead(lam_row, 1), lead(lam_col, 1), lead(bt, 1), lead(ct, 1)],
        out_specs=(lead(out_m, 2), lead(out_sq, 1), lead(out_sq, 1), lead(out_ap, 1)),
        compiler_params=_params(1),
        name="s5_prep",
    )(log_dt, lam_row, lam_col, bt, ct)
    d_row = jnp.tile(d_skip.reshape(N_GROUPS, 1, SSM_GROUP), (1, 1, SSM_CHUNK))
    return m_op, s_pair, r_pair, a_pair, d_row


def _ssm_core_kernel(*refs, seq_chunks, n_steps, pairs, has_h0, emit_state):
    refs = list(refs)
    u_ref, m_ref, s_ref, r_ref, ap_ref, d_ref = refs[:6]
    refs = refs[6:]
    h0_ref = refs.pop(0) if has_h0 else None
    y_ref = refs.pop(0)
    if emit_state:
        e_ref, fin_scr = refs
    rows = u_ref.shape[2]
    n_seq = rows // seq_chunks
    lanes = 2 * STATE
    pos = lax.broadcasted_iota(jnp.int32, (rows, lanes), 0) % seq_chunks

    def upstream(x, d, fwd):
        return pltpu.roll(x, d if fwd else rows - d, 0)

    def has_upstream(d, fwd):
        return (pos >= d) if fwd else (pos < seq_chunks - d)

    def scan(hr, hi, pi, fwd):
        q = 0 if fwd else 2
        for i in range(n_steps):
            d = 1 << i
            p_re, p_im = ap_ref[pi, q, i:i + 1, :], ap_ref[pi, q + 1, i:i + 1, :]
            if d < 8:
                ok = has_upstream(d, fwd)
                sr = jnp.where(ok, upstream(hr, d, fwd), 0.0)
                si = jnp.where(ok, upstream(hi, d, fwd), 0.0)
                hr, hi = hr + p_re * sr - p_im * si, hi + p_re * si + p_im * sr
            else:
                r3, i3 = hr.reshape(n_seq, seq_chunks, lanes), hi.reshape(n_seq, seq_chunks, lanes)
                keep = seq_chunks - d
                if fwd:
                    sr, si, tr, ti = r3[:, :keep], i3[:, :keep], r3[:, d:], i3[:, d:]
                else:
                    sr, si, tr, ti = r3[:, d:], i3[:, d:], r3[:, :keep], i3[:, :keep]
                nr = tr + p_re * sr - p_im * si
                ni = ti + p_re * si + p_im * sr
                if fwd:
                    r3 = jnp.concatenate([r3[:, :d], nr], axis=1)
                    i3 = jnp.concatenate([i3[:, :d], ni], axis=1)
                else:
                    r3 = jnp.concatenate([nr, r3[:, keep:]], axis=1)
                    i3 = jnp.concatenate([ni, i3[:, keep:]], axis=1)
                hr, hi = r3.reshape(rows, lanes), i3.reshape(rows, lanes)
        return hr, hi

    for pi in range(pairs):
        u0, u1 = u_ref[0, 2 * pi], u_ref[0, 2 * pi + 1]
        ub = jnp.concatenate([u0, u1], axis=1).astype(BF16)
        y0 = _dot(ub[:, :SSM_FLAT], m_ref[2 * pi]) + d_ref[2 * pi] * u0
        y1 = _dot(ub[:, SSM_FLAT:], m_ref[2 * pi + 1]) + d_ref[2 * pi + 1] * u1
        h = _dot(ub, s_ref[pi])
        state = {}
        for fwd, off in ((True, 0), (False, lanes)):
            hr, hi = h[:, off:off + lanes], h[:, 2 * lanes + off:3 * lanes + off]
            q = 0 if fwd else 2
            if has_h0:
                h0r = h0_ref[0, pi, 0:1, off:off + lanes]
                h0i = h0_ref[0, pi, 0:1, 2 * lanes + off:3 * lanes + off]
                a_re, a_im = ap_ref[pi, q, 0:1, :], ap_ref[pi, q + 1, 0:1, :]
                first = jnp.logical_not(has_upstream(1, fwd))
                hr = hr + jnp.where(first, a_re * h0r - a_im * h0i, 0.0)
                hi = hi + jnp.where(first, a_re * h0i + a_im * h0r, 0.0)
            else:
                h0r = h0i = 0.0
            hr, hi = scan(hr, hi, pi, fwd)
            ok = has_upstream(1, fwd)
            state[fwd] = (hr, hi, jnp.where(ok, upstream(hr, 1, fwd), h0r), jnp.where(ok, upstream(hi, 1, fwd), h0i))
        if emit_state:
            for q, (fwd, part) in enumerate(((True, 0), (False, 0), (True, 1), (False, 1))):
                fin_scr[q] = state[fwd][part]
                last = seq_chunks - 1 if fwd else 0
                e_ref[0, pi, :, q * lanes:(q + 1) * lanes] = fin_scr[q, pl.ds(last, n_seq, stride=seq_chunks), :]
        h_in = jnp.concatenate([state[True][2], state[False][2], state[True][3], state[False][3]], axis=1)
        y_state = _dot(h_in.astype(BF16), r_ref[pi])
        y_ref[0, 2 * pi] = y0 + y_state[:, :SSM_FLAT]
        y_ref[0, 2 * pi + 1] = y1 + y_state[:, SSM_FLAT:]


def _ssm_core(u_flat, pair_ops, h0, seq_chunks, emit_state):
    nb, _, rows, _ = u_flat.shape
    m_op, s_pair, r_pair, a_pair, d_row = pair_ops
    n_steps = a_pair.shape[2]
    assert (1 << n_steps) == seq_chunks
    pt = 8
    blk = pl.BlockSpec((1, 2 * pt, rows, SSM_FLAT), lambda b, g: (b, g, 0, 0))
    lead = lambda a, n: pl.BlockSpec((n,) + a.shape[1:], lambda b, g: (g,) + (0,) * (a.ndim - 1))
    in_specs = [blk, lead(m_op, 2 * pt), lead(s_pair, pt), lead(r_pair, pt), lead(a_pair, pt), lead(d_row, 2 * pt)]
    args = [u_flat, m_op, s_pair, r_pair, a_pair, d_row]
    if h0 is not None:
        in_specs.append(pl.BlockSpec((1, pt, 1, 8 * STATE), lambda b, g: (b, g, 0, 0)))
        args.append(h0)
    out_shape = [jax.ShapeDtypeStruct(u_flat.shape, F32)]
    out_specs = [blk]
    scratch = []
    if emit_state:
        n_seq = rows // seq_chunks
        out_shape.append(jax.ShapeDtypeStruct((nb, N_GROUPS // 2, n_seq, 8 * STATE), F32))
        out_specs.append(pl.BlockSpec((1, pt, n_seq, 8 * STATE), lambda b, g: (b, g, 0, 0)))
        scratch.append(pltpu.VMEM((4, rows, 2 * STATE), F32))
    kern = functools.partial(_ssm_core_kernel, seq_chunks=seq_chunks, n_steps=n_steps, pairs=pt,
                             has_h0=h0 is not None, emit_state=emit_state)
    return pl.pallas_call(
        kern, out_shape=tuple(out_shape), grid=(nb, N_GROUPS // (2 * pt)),
        in_specs=in_specs, out_specs=tuple(out_specs), scratch_shapes=scratch,
        compiler_params=_params(2), name="ssm_core",
    )(*args)


def _ssm_out_kernel(y_ref, x_ref, mod_ref, ln_ref, wg_ref, wo_ref, out_ref, y_scr):
    i = pl.program_id(0)
    n_chunks = x_ref.shape[0] // SSM_CHUNK

    @pl.when(i == 0)
    def _():
        y_scr[1] = jnp.zeros(y_scr.shape[1:], F32)

    def step(fill_slot):
        fill, ready = y_scr.at[fill_slot], y_scr.at[1 - fill_slot]
        for c in range(LANE_TILES):
            for half in range(SSM_CHUNK // LANE_BLOCKS):
                per_group = [y_ref[0, c * LANE_BLOCKS + gl, :, half * 128:(half + 1) * 128]
                             for gl in range(LANE_BLOCKS)]
                per_token = _block_transpose(per_group)
                for s in range(LANE_BLOCKS):
                    fill[c, pl.ds(half * LANE_BLOCKS + s, n_chunks, stride=SSM_CHUNK), :] = per_token[s]
        y = jnp.concatenate([ready[c] for c in range(LANE_TILES)], axis=1)
        vg = _dot(_gelu_tanh(y).astype(BF16), wg_ref[...])
        glu = (vg[:, :D_MODEL] * jax.nn.sigmoid(vg[:, D_MODEL:])).astype(BF16)
        z = DN_ALPHA * x_ref[...] + mod_ref[0, 2:3, :] * _dot(glu, wo_ref[...])
        out_ref[...] = _layer_norm(z, ln_ref[0:1, :], ln_ref[1:2, :])

    for parity in range(2):
        pl.when(i % 2 == parity)(functools.partial(step, parity))


def _ssm_out(y_flat, x2, mods, mod_base, rows_per_mod, ln, w_glu, w_out):
    r = x2.shape[0]
    nb = y_flat.shape[0]
    t = r // nb
    tm = _row_tile(math.gcd(rows_per_mod, t), SSM_ROWS)
    per_mod = rows_per_mod // tm
    per_seq = t // tm
    nj = tm // SSM_CHUNK
    n_tiles = r // tm
    fill_tile = lambda i: jnp.minimum(i, n_tiles - 1)
    done_tile = lambda i: jnp.maximum(i - 1, 0)
    row = pl.BlockSpec((tm, D_MODEL), lambda i: (done_tile(i), 0))
    return pl.pallas_call(
        _ssm_out_kernel,
        out_shape=jax.ShapeDtypeStruct((r, D_MODEL), F32),
        grid=(n_tiles + 1,),
        in_specs=[pl.BlockSpec((1, N_GROUPS, nj, SSM_FLAT),
                               lambda i: (fill_tile(i) // per_seq, 0, fill_tile(i) % per_seq, 0)),
                  row,
                  _mods_spec(mods, lambda i: mod_base + done_tile(i) // per_mod),
                  _ln_spec(ln), _layer_spec(w_glu), _layer_spec(w_out)],
        out_specs=row,
        scratch_shapes=[pltpu.VMEM((2, LANE_TILES, tm, 128), F32)],
        compiler_params=_params(1),
        name="ssm_out",
    )(y_flat, x2, mods[0], ln[0], w_glu[0], w_out[0])


def _ssm_mixer(x2, nb, seq_chunks, mods, mod_base, rows_per_mod, ln, w_in, ops, w_glu, w_out, h0, emit_state):
    u_flat = _ssm_in(x2, mods, mod_base, rows_per_mod, w_in, nb)
    outs = _ssm_core(u_flat, ops, h0, seq_chunks, emit_state)
    x_new = _ssm_out(outs[0], x2, mods, mod_base, rows_per_mod, ln, w_glu, w_out)
    return x_new, (outs[1] if emit_state else None)


def _rope_tables(t):
    n_rows = t // GRID_W
    row = jnp.repeat(jnp.arange(n_rows, dtype=F32), GRID_W)
    col = (jnp.arange(t) % GRID_W).astype(F32)
    nfreq = HEAD_DIM // 4
    inv = jnp.power(ROPE_BASE, -jnp.arange(nfreq, dtype=F32) / nfreq)
    ang_r, ang_c = row[:, None] * inv, col[:, None] * inv
    cos = jnp.concatenate([jnp.cos(ang_r)] * 2 + [jnp.cos(ang_c)] * 2, -1)
    sin = jnp.concatenate([-jnp.sin(ang_r), jnp.sin(ang_r), -jnp.sin(ang_c), jnp.sin(ang_c)], -1)
    return jnp.tile(cos, (1, 2)), jnp.tile(sin, (1, 2))


def kernel(x_prompt, x_sample, cache_k, cache_v, state_ssm_re, state_ssm_im, c, c_ctx, w_ada, b_ada, ln_g, ln_b, w_qkv, w_o, attn_sink, ssm_w_in, ssm_lam_re, ssm_lam_im, ssm_log_dt, ssm_b_re, ssm_b_im, ssm_c_re, ssm_c_im, ssm_d, ssm_w_glu, ssm_w_out, ffn_w1, ffn_w3, ffn_w2):
    nc, tc, _ = x_prompt.shape
    nl, tl, _ = x_sample.shape
    n_attn = w_qkv.shape[0]
    n_ssm = ssm_w_in.shape[0]

    n_mod_rows = -(-(1 + nl) // 8) * 8
    cvec = jnp.zeros((n_mod_rows, D_MODEL), F32).at[0].set(c_ctx).at[1:1 + nl].set(c)
    mods_all = _ada_mods(cvec, w_ada, b_ada)
    ln_all = jnp.stack([ln_g, ln_b], axis=2)

    bf = lambda w: w.astype(BF16)
    w_v = w_qkv[:, :, Q_WIDTH + KV_WIDTH:].reshape(n_attn, D_MODEL, N_KV_HEADS, HEAD_DIM)
    w_v_ext = jnp.pad(w_v, ((0, 0), (0, 0), (0, 0), (0, 128 - HEAD_DIM))).reshape(n_attn, D_MODEL, V_EXT)
    w_qkv_ctx = bf(jnp.concatenate([w_qkv, w_v_ext], axis=-1))
    w_qkv_lat = bf(jnp.concatenate([w_qkv[:, :, :Q_WIDTH + KV_WIDTH], w_v_ext], axis=-1))
    w_o_b = bf(w_o)
    w_in_b, w_glu_b, w_out_b = bf(ssm_w_in), bf(ssm_w_glu), bf(ssm_w_out)
    w1_b, w3_b, w2_b = bf(ffn_w1), bf(ffn_w3), bf(ffn_w2)
    cos, sin = _rope_tables(tl)
    kx = bf(cache_k).reshape(nl, n_attn, -1, KV_WIDTH)
    vx = jnp.concatenate([bf(cache_v), jnp.ones(cache_v.shape[:-1] + (128 - HEAD_DIM,), BF16)], -1)
    vx = vx.reshape(nl, n_attn, -1, V_EXT)

    steps_c = (tc // SSM_CHUNK).bit_length() - 1
    steps_l = (tl // SSM_CHUNK).bit_length() - 1
    ssm_ops = []
    for s in range(n_ssm):
        prm = (ssm_lam_re[s], ssm_lam_im[s], ssm_log_dt[s], ssm_b_re[s], ssm_b_im[s],
               ssm_c_re[s], ssm_c_im[s], ssm_d[s])
        ssm_ops.append(_s5_operators(*prm, max(steps_c, steps_l)))

    def ops_for(s, n_steps):
        m_op, s_pair, r_pair, a_pair, d_row = ssm_ops[s]
        return m_op, s_pair, r_pair, a_pair[:, :, :n_steps], d_row

    xc = x_prompt.reshape(nc * tc, D_MODEL)
    xl = x_sample.reshape(nl * tl, D_MODEL)
    new_k, new_v, new_sr, new_si = [], [], [], []
    for l in range(DEPTH):
        mods = (mods_all, l)
        ln1, ln2 = (ln_all, l, 0), (ln_all, l, 1)
        if l % 2 == 0:
            a = l // 2
            q, kb, vb, k32, v32 = _qkv_ctx(xc, mods, (w_qkv_ctx, a))
            new_k.append(k32.reshape(nc, tc, N_KV_HEADS, HEAD_DIM))
            new_v.append(v32.reshape(nc, tc, N_KV_HEADS, HEAD_DIM))
            xc = _attn_ctx(attn_sink[a], q, kb, vb, xc, mods, ln1, (w_o_b, a), tc)
            qp, qr, kr, v = _qkv_lat(xl, mods, (w_qkv_lat, a), cos, sin, tl)
            xl = _attn_lat(attn_sink[a], qp, qr, kr, v, (kx, a), (vx, a), xl, mods, ln1, (w_o_b, a), nl, tl)
        else:
            s = l // 2
            xc, e_c = _ssm_mixer(xc, 1, tc // SSM_CHUNK, mods, 0, nc * tc, ln1, (w_in_b, s),
                                 ops_for(s, steps_c), (w_glu_b, s), (w_out_b, s), None, True)
            fin = e_c.reshape(N_GROUPS // 2, nc, 4, 2, STATE).transpose(2, 1, 0, 3, 4).reshape(4, nc, N_GROUPS, STATE)
            new_sr.append(jnp.stack([fin[0], fin[1]], 1))
            new_si.append(jnp.stack([fin[2], fin[3]], 1))
            h0 = jnp.stack([state_ssm_re[:, s, 0], state_ssm_re[:, s, 1],
                            state_ssm_im[:, s, 0], state_ssm_im[:, s, 1]], 1)
            h0 = h0.reshape(nl, 4, N_GROUPS // 2, 2 * STATE).transpose(0, 2, 1, 3)
            h0 = h0.reshape(nl, N_GROUPS // 2, 8 * STATE)
            xl, _ = _ssm_mixer(xl, nl, tl // SSM_CHUNK, mods, 1, tl, ln1, (w_in_b, s),
                               ops_for(s, steps_l), (w_glu_b, s), (w_out_b, s), h0[:, :, None, :], False)
        xc = _ffn(xc, mods, 0, nc * tc, ln2, (w1_b, l), (w3_b, l), (w2_b, l))
        xl = _ffn(xl, mods, 1, tl, ln2, (w1_b, l), (w3_b, l), (w2_b, l))

    return (xc.reshape(nc, tc, D_MODEL), xl.reshape(nl, tl, D_MODEL),
            jnp.stack(new_k, axis=1), jnp.stack(new_v, axis=1),
            jnp.stack(new_sr, axis=1), jnp.stack(new_si, axis=1))
```

```python
import functools
import math

import jax
import jax.numpy as jnp
from jax import lax
from jax.experimental import pallas as pl
from jax.experimental.pallas import tpu as pltpu

F32 = jnp.float32
BF16 = jnp.bfloat16

D_MODEL = 1024
DEPTH = 4
N_HEADS = 16
N_KV_HEADS = 4
HEAD_DIM = 64
HEADS_PER_KV = N_HEADS // N_KV_HEADS
Q_WIDTH = N_HEADS * HEAD_DIM
KV_WIDTH = N_KV_HEADS * HEAD_DIM
GRID_W = 64
ATT_BLOCK = 128
ROPE_BASE = 10000.0
SSM_GROUP = 16
N_GROUPS = D_MODEL // SSM_GROUP
STATE = 64
SSM_CHUNK = 16
SSM_FLAT = SSM_CHUNK * SSM_GROUP
D_FF = -(-8 * D_MODEL // (3 * 256)) * 256
FF_CHUNK = 256
FFN_ROWS = 1024
FFN_SPLIT = 2
DN_ALPHA = (2 * DEPTH) ** 0.25
LN_EPS = 1e-5
NEG_INF = -1e30
N_MODS = 6
VMEM_LIMIT = 56 * 1024 * 1024
HIGHEST = lax.Precision.HIGHEST


def _params(n_axes):
    return pltpu.CompilerParams(dimension_semantics=("arbitrary",) * n_axes,
                                vmem_limit_bytes=VMEM_LIMIT)


def _layer_spec(param, single=False):
    stacked, layer = param
    rest = stacked.shape[1:]
    mode = dict(pipeline_mode=pl.Buffered(1)) if single else {}
    return pl.BlockSpec((None,) + rest, lambda *_: (layer,) + (0,) * len(rest), **mode)


def _mods_spec(mods, row_of):
    return pl.BlockSpec((None, 1, N_MODS, D_MODEL), lambda *ids: (mods[1], row_of(*ids), 0, 0))


def _ln_spec(ln):
    return pl.BlockSpec((None, None, 2, D_MODEL), lambda *_: (ln[1], ln[2], 0, 0))


def _layer_norm(y, g, b):
    mu = jnp.mean(y, -1, keepdims=True)
    d = y - mu
    var = jnp.mean(d * d, -1, keepdims=True)
    return d * lax.rsqrt(var + LN_EPS) * g + b


def _silu(x):
    return x * jax.nn.sigmoid(x)


def _gelu_tanh(x):
    return 0.5 * x * (1.0 + jnp.tanh(math.sqrt(2.0 / math.pi) * (x + 0.044715 * (x * x * x))))


def _dot(a, b):
    return jnp.dot(a, b, preferred_element_type=F32)


def _dot_nt(a, b):
    return lax.dot_general(a, b, (((1,), (1,)), ((), ())), preferred_element_type=F32)


def _row_tile(rows_per_mod, want):
    tm = min(want, rows_per_mod)
    assert rows_per_mod % tm == 0
    return tm


def _mods_kernel(c_ref, w_ref, b_ref, o_ref):
    a = _silu(c_ref[...]).astype(BF16)
    o_ref[0] = _dot(a, w_ref[0].astype(BF16)) + b_ref[0]


def _ada_mods(cvec, w_ada, b_ada):
    r = cvec.shape[0]
    tn = 1536
    out = pl.pallas_call(
        _mods_kernel,
        out_shape=jax.ShapeDtypeStruct((DEPTH, r, N_MODS * D_MODEL), F32),
        grid=(DEPTH, N_MODS * D_MODEL // tn),
        in_specs=[pl.BlockSpec((r, D_MODEL), lambda l, j: (0, 0)),
                  pl.BlockSpec((1, D_MODEL, tn), lambda l, j: (l, 0, j)),
                  pl.BlockSpec((1, 1, tn), lambda l, j: (l, 0, j))],
        out_specs=pl.BlockSpec((1, r, tn), lambda l, j: (l, 0, j)),
        compiler_params=_params(2),
        name="ada_mods",
    )(cvec, w_ada, b_ada.reshape(DEPTH, 1, N_MODS * D_MODEL))
    return out.reshape(DEPTH, r, N_MODS, D_MODEL)


Q_SCALE = HEAD_DIM ** -0.5 * math.log2(math.e)
V_EXT = N_KV_HEADS * 128


def _rope(x, cos, sin, first_half):
    outs = []
    for c in range(x.shape[1] // 128):
        xc = x[:, c * 128:(c + 1) * 128]
        partner = jnp.where(first_half, pltpu.roll(xc, 128 - 16, 1), pltpu.roll(xc, 16, 1))
        outs.append(xc * cos + partner * sin)
    return jnp.concatenate(outs, axis=1)


def _ones_lanes(width):
    return jnp.where((lax.broadcasted_iota(jnp.int32, (1, width), 1) % 128) >= HEAD_DIM, 1.0, 0.0)


def _qkv_ctx_kernel(x_ref, mod_ref, w_ref, q_ref, kb_ref, vb_ref, k_ref, v_ref):
    h = (x_ref[...] * (1.0 + mod_ref[0, 1:2, :]) + mod_ref[0, 0:1, :]).astype(BF16)
    qkv = _dot(h, w_ref[...])
    q_ref[...] = (qkv[:, :Q_WIDTH] * Q_SCALE).astype(BF16)
    k = qkv[:, Q_WIDTH:Q_WIDTH + KV_WIDTH]
    k_ref[...] = k
    v_ref[...] = qkv[:, Q_WIDTH + KV_WIDTH:Q_WIDTH + 2 * KV_WIDTH]
    kb_ref[...] = k.astype(BF16)
    vb_ref[...] = (qkv[:, Q_WIDTH + 2 * KV_WIDTH:] + _ones_lanes(V_EXT)).astype(BF16)


def _qkv_lat_kernel(x_ref, mod_ref, w_ref, cos_ref, sin_ref, qp_ref, qr_ref, kr_ref, v_ref):
    h = (x_ref[...] * (1.0 + mod_ref[0, 1:2, :]) + mod_ref[0, 0:1, :]).astype(BF16)
    qkv = _dot(h, w_ref[...])
    cos = cos_ref[...]
    sin = sin_ref[...]
    first_half = (lax.broadcasted_iota(jnp.int32, cos.shape, 1) & 16) == 0
    q = qkv[:, :Q_WIDTH] * Q_SCALE
    qp_ref[...] = q.astype(BF16)
    qr_ref[...] = _rope(q, cos, sin, first_half).astype(BF16)
    kr_ref[...] = _rope(qkv[:, Q_WIDTH:Q_WIDTH + KV_WIDTH], cos, sin, first_half).astype(BF16)
    v_ref[...] = (qkv[:, Q_WIDTH + KV_WIDTH:] + _ones_lanes(V_EXT)).astype(BF16)


def _qkv_ctx(x2, mods, w_ext):
    r = x2.shape[0]
    tm = _row_tile(r, 512)
    row = lambda w: pl.BlockSpec((tm, w), lambda i: (i, 0))
    return pl.pallas_call(
        _qkv_ctx_kernel,
        out_shape=(jax.ShapeDtypeStruct((r, Q_WIDTH), BF16),
                   jax.ShapeDtypeStruct((r, KV_WIDTH), BF16),
                   jax.ShapeDtypeStruct((r, V_EXT), BF16),
                   jax.ShapeDtypeStruct((r, KV_WIDTH), F32),
                   jax.ShapeDtypeStruct((r, KV_WIDTH), F32)),
        grid=(r // tm,),
        in_specs=[row(D_MODEL), _mods_spec(mods, lambda i: 0), _layer_spec(w_ext)],
        out_specs=(row(Q_WIDTH), row(KV_WIDTH), row(V_EXT), row(KV_WIDTH), row(KV_WIDTH)),
        compiler_params=_params(1),
        name="qkv_ctx",
    )(x2, mods[0], w_ext[0])


def _qkv_lat(x2, mods, w_ext, cos, sin, t):
    r = x2.shape[0]
    tm = _row_tile(t, 1024)
    per_seq = t // tm
    row = lambda w: pl.BlockSpec((tm, w), lambda i: (i, 0))
    tab = pl.BlockSpec((tm, 128), lambda i: (i % per_seq, 0))
    return pl.pallas_call(
        _qkv_lat_kernel,
        out_shape=(jax.ShapeDtypeStruct((r, Q_WIDTH), BF16),
                   jax.ShapeDtypeStruct((r, Q_WIDTH), BF16),
                   jax.ShapeDtypeStruct((r, KV_WIDTH), BF16),
                   jax.ShapeDtypeStruct((r, V_EXT), BF16)),
        grid=(r // tm,),
        in_specs=[row(D_MODEL), _mods_spec(mods, lambda i: 1 + i // per_seq), _layer_spec(w_ext), tab, tab],
        out_specs=(row(Q_WIDTH), row(Q_WIDTH), row(KV_WIDTH), row(V_EXT)),
        compiler_params=_params(1),
        name="qkv_lat",
    )(x2, mods[0], w_ext[0], cos, sin)


def _attn_epilogue(o_scr, x_ref, mod_ref, ln_ref, wo_ref, out_ref):
    y = _dot(o_scr[...], wo_ref[...])
    z = DN_ALPHA * x_ref[...] + mod_ref[0, 2:3, :] * y
    out_ref[...] = _layer_norm(z, ln_ref[0:1, :], ln_ref[1:2, :])


def _attn_ctx_kernel(sink_ref, q_ref, k_ref, v_ref, x_ref, mod_ref, ln_ref, wo_ref, out_ref, o_scr, s_scr, m_scr):
    rows = q_ref.shape[0]
    n_tiles = k_ref.shape[0] // 128

    def scores(kv, slot):
        k = k_ref[:, kv * HEAD_DIM:(kv + 1) * HEAD_DIM]
        for g in range(HEADS_PER_KV):
            h = kv * HEADS_PER_KV + g
            s = _dot_nt(q_ref[:, h * HEAD_DIM:(h + 1) * HEAD_DIM], k)
            m = jnp.maximum(jnp.max(s, -1, keepdims=True), sink_ref[h] * math.log2(math.e))
            s_scr[slot, g * rows:(g + 1) * rows, :] = s
            m_scr[slot, g * rows:(g + 1) * rows, :] = jnp.broadcast_to(m, (rows, 128))

    def values(kv, slot):
        v = v_ref[:, kv * 128:(kv + 1) * 128]
        for g in range(HEADS_PER_KV):
            h = kv * HEADS_PER_KV + g
            r0, r1 = g * rows, (g + 1) * rows
            m = m_scr[slot, r0:r1, :]
            p = [jnp.exp2(s_scr[slot, r0:r1, c * 128:(c + 1) * 128] - m).astype(BF16) for c in range(n_tiles)]
            o_ext = _dot(jnp.concatenate(p, axis=1), v)
            den = pltpu.roll(o_ext, HEAD_DIM, 1) + jnp.exp2(sink_ref[h] * math.log2(math.e) - m)
            o_scr[:, h * HEAD_DIM:(h + 1) * HEAD_DIM] = (o_ext / den)[:, :HEAD_DIM].astype(BF16)

    scores(0, 0)
    for kv in range(N_KV_HEADS):
        if kv + 1 < N_KV_HEADS:
            scores(kv + 1, (kv + 1) % 2)
        values(kv, kv % 2)
    _attn_epilogue(o_scr, x_ref, mod_ref, ln_ref, wo_ref, out_ref)


def _attn_ctx(sink, q, k, v_ext, x2, mods, ln, w_o, seq):
    r = x2.shape[0]
    row = lambda w: pl.BlockSpec((seq, w), lambda i: (i, 0))
    return pl.pallas_call(
        _attn_ctx_kernel,
        out_shape=jax.ShapeDtypeStruct((r, D_MODEL), F32),
        grid=(r // seq,),
        in_specs=[pl.BlockSpec(memory_space=pltpu.SMEM),
                  row(Q_WIDTH), row(KV_WIDTH), row(V_EXT), row(D_MODEL),
                  _mods_spec(mods, lambda i: 0), _ln_spec(ln), _layer_spec(w_o)],
        out_specs=row(D_MODEL),
        scratch_shapes=[pltpu.VMEM((seq, Q_WIDTH), BF16),
                        pltpu.VMEM((2, HEADS_PER_KV * seq, seq), F32),
                        pltpu.VMEM((2, HEADS_PER_KV * seq, 128), F32)],
        compiler_params=_params(1),
        name="attn_ctx",
    )(sink, q, k, v_ext, x2, mods[0], ln[0], w_o[0])


LOCAL_KEYS = 3 * ATT_BLOCK
ATT_STEP_BLOCKS = 4


def _attn_lat_kernel(sink_ref, qp_ref, qr_ref, k_ref, v_ref, kx_ref, vx_ref, x_ref, mod_ref, ln_ref,
                     wo_ref, out_ref, o_scr, s_scr, m_scr, bias_scr):
    rows = ATT_BLOCK
    seq = k_ref.shape[0]
    n_ctx = kx_ref.shape[1]
    n_tiles = (n_ctx + LOCAL_KEYS) // 128
    n_sub = qp_ref.shape[0] // rows
    r_idx = lax.broadcasted_iota(jnp.int32, (rows, LOCAL_KEYS), 0)
    c_idx = lax.broadcasted_iota(jnp.int32, (rows, LOCAL_KEYS), 1)
    starts = []
    for sub in range(n_sub):
        j = pl.program_id(1) * n_sub + sub
        start = pl.multiple_of(jnp.clip((j - 1) * rows, 0, seq - LOCAL_KEYS), rows)
        bias_scr[sub] = jnp.where(jnp.abs(r_idx - c_idx + (j * rows - start)) <= ATT_BLOCK, 0.0, NEG_INF)
        starts.append(start)

    def scores(sub, kv, slot):
        lo, hi = kv * HEAD_DIM, (kv + 1) * HEAD_DIM
        q0 = sub * rows
        kx = kx_ref[0, :, lo:hi]
        kl = k_ref[pl.ds(starts[sub], LOCAL_KEYS), lo:hi]
        for g in range(HEADS_PER_KV):
            h = kv * HEADS_PER_KV + g
            r0, r1 = g * rows, (g + 1) * rows
            s_x = _dot_nt(qp_ref[q0:q0 + rows, h * HEAD_DIM:(h + 1) * HEAD_DIM], kx)
            s_l = _dot_nt(qr_ref[q0:q0 + rows, h * HEAD_DIM:(h + 1) * HEAD_DIM], kl) + bias_scr[sub]
            m = jnp.maximum(jnp.maximum(jnp.max(s_x, -1, keepdims=True), jnp.max(s_l, -1, keepdims=True)),
                            sink_ref[h] * math.log2(math.e))
            s_scr[slot, r0:r1, :n_ctx] = s_x
            s_scr[slot, r0:r1, n_ctx:] = s_l
            m_scr[slot, r0:r1, :] = jnp.broadcast_to(m, (rows, 128))

    def values(sub, kv, slot):
        q0 = sub * rows
        vx = vx_ref[0, :, kv * 128:(kv + 1) * 128]
        vl = v_ref[pl.ds(starts[sub], LOCAL_KEYS), kv * 128:(kv + 1) * 128]
        for g in range(HEADS_PER_KV):
            h = kv * HEADS_PER_KV + g
            r0, r1 = g * rows, (g + 1) * rows
            m = m_scr[slot, r0:r1, :]
            p = [jnp.exp2(s_scr[slot, r0:r1, c * 128:(c + 1) * 128] - m).astype(BF16) for c in range(n_tiles)]
            o_ext = (_dot(jnp.concatenate(p[:n_ctx // 128], axis=1), vx)
                     + _dot(jnp.concatenate(p[n_ctx // 128:], axis=1), vl))
            den = pltpu.roll(o_ext, HEAD_DIM, 1) + jnp.exp2(sink_ref[h] * math.log2(math.e) - m)
            o_scr[q0:q0 + rows, h * HEAD_DIM:(h + 1) * HEAD_DIM] = (o_ext / den)[:, :HEAD_DIM].astype(BF16)

    units = [(sub, kv) for sub in range(n_sub) for kv in range(N_KV_HEADS)]
    scores(*units[0], 0)
    for u, unit in enumerate(units):
        values(*unit, u % 2)
        if u + 1 < len(units):
            scores(*units[u + 1], (u + 1) % 2)
    _attn_epilogue(o_scr, x_ref, mod_ref, ln_ref, wo_ref, out_ref)


def _attn_lat(sink, qp, qr, kr, v_ext, k_ctx, v_ctx_ext, x2, mods, ln, w_o, n, t):
    assert t >= LOCAL_KEYS
    blk = ATT_STEP_BLOCKS * ATT_BLOCK
    nblk = t // blk
    row = lambda w: pl.BlockSpec((blk, w), lambda b, j: (b * nblk + j, 0))
    seq = lambda w: pl.BlockSpec((t, w), lambda b, j: (b, 0))
    ctx = lambda a: pl.BlockSpec((1, None) + a[0].shape[2:], lambda b, j: (b, a[1], 0, 0))
    return pl.pallas_call(
        _attn_lat_kernel,
        out_shape=jax.ShapeDtypeStruct((n * t, D_MODEL), F32),
        grid=(n, nblk),
        in_specs=[pl.BlockSpec(memory_space=pltpu.SMEM),
                  row(Q_WIDTH), row(Q_WIDTH), seq(KV_WIDTH), seq(V_EXT),
                  ctx(k_ctx), ctx(v_ctx_ext), row(D_MODEL),
                  _mods_spec(mods, lambda b, j: 1 + b), _ln_spec(ln), _layer_spec(w_o)],
        out_specs=row(D_MODEL),
        scratch_shapes=[pltpu.VMEM((blk, Q_WIDTH), BF16),
                        pltpu.VMEM((2, HEADS_PER_KV * ATT_BLOCK, k_ctx[0].shape[2] + LOCAL_KEYS), F32),
                        pltpu.VMEM((2, HEADS_PER_KV * ATT_BLOCK, 128), F32),
                        pltpu.VMEM((ATT_STEP_BLOCKS, ATT_BLOCK, LOCAL_KEYS), F32)],
        compiler_params=_params(2),
        name="attn_lat",
    )(sink, qp, qr, kr, v_ext, k_ctx[0], v_ctx_ext[0], x2, mods[0], ln[0], w_o[0])


def _ffn_kernel(x_ref, mod_ref, ln_ref, w1_ref, w3_ref, w2_ref, *rest, fuse_next):
    if fuse_next:
        nmod_ref, win_ref, out_ref, u_ref, acc_ref, u_scr = rest
    else:
        out_ref, acc_ref = rest
    half = x_ref.shape[0] // FFN_SPLIT
    for part in range(FFN_SPLIT):
        r0, r1 = part * half, (part + 1) * half
        x = x_ref[r0:r1, :]
        h = (x * (1.0 + mod_ref[0, 4:5, :]) + mod_ref[0, 3:4, :]).astype(BF16)
        for c in range(D_FF // FF_CHUNK):
            lo, hi = c * FF_CHUNK, (c + 1) * FF_CHUNK
            a = _dot(h, w1_ref[:, lo:hi])
            b = _dot(h, w3_ref[:, lo:hi])
            y = _dot((_silu(a) * b).astype(BF16), w2_ref[lo:hi, :])
            if c == 0:
                acc_ref[r0:r1, :] = y
            else:
                acc_ref[r0:r1, :] += y
        z = DN_ALPHA * x + mod_ref[0, 5:6, :] * acc_ref[r0:r1, :]
        out = _layer_norm(z, ln_ref[0:1, :], ln_ref[1:2, :])
        out_ref[r0:r1, :] = out
        if fuse_next:
            _emit_chunk_rows(out, nmod_ref, win_ref, u_ref, u_scr.at[part], r0 // SSM_CHUNK)


def _ffn(x2, mods, mod_base, rows_per_mod, ln, w1, w3, w2, next_s5=None, nb=1):
    r = x2.shape[0]
    tm = _row_tile(rows_per_mod, FFN_ROWS)
    per_mod = rows_per_mod // tm
    row = pl.BlockSpec((tm, D_MODEL), lambda i: (i, 0))
    mod_row = lambda i: mod_base + i // per_mod
    in_specs = [row, _mods_spec(mods, mod_row), _ln_spec(ln), _layer_spec(w1, single=True),
                _layer_spec(w3, single=True), _layer_spec(w2, single=True)]
    args = [x2, mods[0], ln[0], w1[0], w3[0], w2[0]]
    out_shape = [jax.ShapeDtypeStruct((r, D_MODEL), F32)]
    out_specs = [row]
    scratch = [pltpu.VMEM((tm, D_MODEL), F32)]
    if next_s5 is not None:
        nmods, w_in = next_s5
        t = r // nb
        per_seq = t // tm
        in_specs += [_mods_spec(nmods, mod_row), _layer_spec(w_in)]
        args += [nmods[0], w_in[0]]
        out_shape.append(jax.ShapeDtypeStruct((nb, N_GROUPS, t // SSM_CHUNK, SSM_FLAT), F32))
        out_specs.append(pl.BlockSpec((1, N_GROUPS, tm // SSM_CHUNK, SSM_FLAT),
                                      lambda i: (i // per_seq, 0, i % per_seq, 0)))
        scratch.append(pltpu.VMEM((FFN_SPLIT, LANE_TILES, tm // FFN_SPLIT, 128), F32))
    outs = pl.pallas_call(
        functools.partial(_ffn_kernel, fuse_next=next_s5 is not None),
        out_shape=tuple(out_shape), grid=(r // tm,),
        in_specs=in_specs, out_specs=tuple(out_specs), scratch_shapes=scratch,
        compiler_params=_params(1), name="ffn",
    )(*args)
    return outs if next_s5 is not None else (outs[0], None)


LANE_BLOCKS = 128 // SSM_GROUP
LANE_TILES = D_MODEL // 128
SSM_ROWS = 1024


def _block_transpose(xs):
    xs = list(xs)
    blk = lax.broadcasted_iota(jnp.int32, xs[0].shape, 1) // SSM_GROUP
    k = LANE_BLOCKS // 2
    while k >= 1:
        keep_lo = (blk & k) == 0
        for i in range(LANE_BLOCKS):
            if i & k:
                continue
            lo, hi = xs[i], xs[i + k]
            xs[i] = jnp.where(keep_lo, lo, pltpu.roll(hi, SSM_GROUP * k, 1))
            xs[i + k] = jnp.where(keep_lo, pltpu.roll(lo, 128 - SSM_GROUP * k, 1), hi)
        k //= 2
    return xs


def _emit_chunk_rows(x, mod_ref, w_ref, o_ref, u_scr, first_chunk=0):
    h = (x * (1.0 + mod_ref[0, 1:2, :]) + mod_ref[0, 0:1, :]).astype(BF16)
    u = _dot(h, w_ref[...])
    for c in range(LANE_TILES):
        u_scr[c] = u[:, c * 128:(c + 1) * 128]
    n_chunks = x.shape[0] // SSM_CHUNK
    for c in range(LANE_TILES):
        for half in range(SSM_CHUNK // LANE_BLOCKS):
            per_token = [u_scr[c, pl.ds(half * LANE_BLOCKS + s, n_chunks, stride=SSM_CHUNK), :]
                         for s in range(LANE_BLOCKS)]
            per_group = _block_transpose(per_token)
            for gl in range(LANE_BLOCKS):
                o_ref[0, c * LANE_BLOCKS + gl, first_chunk:first_chunk + n_chunks,
                      half * 128:(half + 1) * 128] = per_group[gl]


def _ssm_in_kernel(x_ref, mod_ref, w_ref, o_ref, u_scr):
    _emit_chunk_rows(x_ref[...], mod_ref, w_ref, o_ref, u_scr)


def _ssm_in(x2, mods, mod_base, rows_per_mod, w, nb):
    r = x2.shape[0]
    t = r // nb
    tm = _row_tile(math.gcd(rows_per_mod, t), SSM_ROWS)
    per_mod = rows_per_mod // tm
    per_seq = t // tm
    nj = tm // SSM_CHUNK
    return pl.pallas_call(
        _ssm_in_kernel,
        out_shape=jax.ShapeDtypeStruct((nb, N_GROUPS, t // SSM_CHUNK, SSM_FLAT), F32),
        grid=(r // tm,),
        in_specs=[pl.BlockSpec((tm, D_MODEL), lambda i: (i, 0)),
                  _mods_spec(mods, lambda i: mod_base + i // per_mod), _layer_spec(w)],
        out_specs=pl.BlockSpec((1, N_GROUPS, nj, SSM_FLAT), lambda i: (i // per_seq, 0, i % per_seq, 0)),
        scratch_shapes=[pltpu.VMEM((LANE_TILES, tm, 128), F32)],
        compiler_params=_params(1),
        name="ssm_in",
    )(x2, mods[0], w[0])


def _s5_prep_kernel(logdt_ref, lam_row_ref, lam_col_ref, bt_ref, ct_ref,
                    m_ref, s_ref, r_ref, ap_ref, *, n_steps):
    pair = pl.program_id(0)
    lc = SSM_CHUNK
    lanes = 2 * STATE

    def transition(lam_re, lam_im, d, axis):
        first = lax.broadcasted_iota(jnp.int32, lam_re.shape, axis) < STATE
        log_dt = jnp.where(first, jnp.full(lam_re.shape, logdt_ref[d, 2 * pair], F32),
                           jnp.full(lam_re.shape, logdt_ref[d, 2 * pair + 1], F32))
        dt = jnp.exp(log_dt)
        mag = jnp.exp(lam_re * dt)
        return mag * jnp.cos(lam_im * dt), mag * jnp.sin(lam_im * dt)

    def powers(a_re, a_im, n):
        pr, pi = [jnp.ones_like(a_re)], [jnp.zeros_like(a_im)]
        for _ in range(n):
            r, i = pr[-1], pi[-1]
            pr.append(r * a_re - i * a_im)
            pi.append(r * a_im + i * a_re)
        return pr, pi

    dot_hi = lambda a, b: jnp.dot(a, b, precision=HIGHEST, preferred_element_type=F32)
    row_shape = (SSM_GROUP, lanes)
    x_group = lax.broadcasted_iota(jnp.int32, (SSM_FLAT, lanes), 1) // STATE
    w_group = lax.broadcasted_iota(jnp.int32, (lanes, SSM_FLAT), 0) // STATE
    t_blk = lax.broadcasted_iota(jnp.int32, (SSM_FLAT, SSM_FLAT), 1) // SSM_GROUP
    t_blk_w = lax.broadcasted_iota(jnp.int32, (lanes, SSM_FLAT), 1) // SSM_GROUP
    m_acc = [jnp.zeros((SSM_FLAT, SSM_FLAT), F32) for _ in range(2)]
    spread = jnp.where(lax.broadcasted_iota(jnp.int32, (SSM_GROUP, SSM_FLAT), 0)
                       == lax.broadcasted_iota(jnp.int32, (SSM_GROUP, SSM_FLAT), 1) % SSM_GROUP, 1.0, 0.0)

    for d in range(2):
        fwd = d == 0
        lam_re = jnp.broadcast_to(lam_row_ref[0, 2 * d:2 * d + 1, :], row_shape)
        lam_im = jnp.broadcast_to(lam_row_ref[0, 2 * d + 1:2 * d + 2, :], row_shape)
        a_re, a_im = transition(lam_re, lam_im, d, 1)
        den = lam_re * lam_re + lam_im * lam_im
        nr, ni = a_re - 1.0, a_im
        coef_re = (nr * lam_re + ni * lam_im) / den
        coef_im = (ni * lam_re - nr * lam_im) / den
        bt_re, bt_im = bt_ref[0, 2 * d], bt_ref[0, 2 * d + 1]
        bb_re = coef_re * bt_re - coef_im * bt_im
        bb_im = coef_re * bt_im + coef_im * bt_re
        pr, pi = powers(a_re, a_im, lc)
        order = [lc - 1 - s for s in range(lc)] if fwd else list(range(lc))
        x_re = jnp.concatenate([pr[k] * bb_re - pi[k] * bb_im for k in order], axis=0)
        x_im = jnp.concatenate([pr[k] * bb_im + pi[k] * bb_re for k in order], axis=0)
        sq_re, sq_im = pr[lc], pi[lc]
        for i in range(n_steps):
            ap_ref[0, 2 * d, i:i + 1, :] = sq_re[0:1, :]
            ap_ref[0, 2 * d + 1, i:i + 1, :] = sq_im[0:1, :]
            sq_re, sq_im = sq_re * sq_re - sq_im * sq_im, 2.0 * sq_re * sq_im
        ct_re = lax.dot_general(ct_ref[0, 2 * d], spread, (((0,), (0,)), ((), ())), precision=HIGHEST,
                                preferred_element_type=F32)
        ct_im = lax.dot_general(ct_ref[0, 2 * d + 1], spread, (((0,), (0,)), ((), ())), precision=HIGHEST,
                                preferred_element_type=F32)
        for gl in range(2):
            xg_re = jnp.where(x_group == gl, x_re, 0.0)
            xg_im = jnp.where(x_group == gl, x_im, 0.0)
            r0 = gl * SSM_FLAT
            s_ref[0, r0:r0 + SSM_FLAT, d * lanes:(d + 1) * lanes] = xg_re.astype(BF16)
            s_ref[0, r0:r0 + SSM_FLAT, (2 + d) * lanes:(3 + d) * lanes] = xg_im.astype(BF16)
            kw = dot_hi(xg_re, ct_re) - dot_hi(xg_im, ct_im)
            for t in range(lc):
                if fwd:
                    sh = (lc - 1 - t) * SSM_GROUP
                    moved = kw if sh == 0 else jnp.concatenate([kw[sh:], jnp.zeros((sh, SSM_FLAT), F32)], axis=0)
                else:
                    sh = t * SSM_GROUP
                    moved = kw if sh == 0 else jnp.concatenate(
                        [jnp.zeros((sh, SSM_FLAT), F32), kw[:SSM_FLAT - sh]], axis=0)
                m_acc[gl] = m_acc[gl] + jnp.where(t_blk == t, moved, 0.0)

        lam_re_c = jnp.broadcast_to(lam_col_ref[0, :, 2 * d:2 * d + 1], (lanes, 128))
        lam_im_c = jnp.broadcast_to(lam_col_ref[0, :, 2 * d + 1:2 * d + 2], (lanes, 128))
        ac_re, ac_im = transition(lam_re_c, lam_im_c, d, 0)
        pcr, pci = powers(ac_re, ac_im, lc)
        pcr = [jnp.concatenate([p, p], axis=1) for p in pcr]
        pci = [jnp.concatenate([p, p], axis=1) for p in pci]
        pw_re = jnp.zeros_like(ct_re)
        pw_im = jnp.zeros_like(ct_re)
        for t in range(lc):
            k = t + 1 if fwd else lc - t
            pw_re = jnp.where(t_blk_w == t, pcr[k], pw_re)
            pw_im = jnp.where(t_blk_w == t, pci[k], pw_im)
        w_re = ct_re * pw_re - ct_im * pw_im
        w_im = ct_re * pw_im + ct_im * pw_re
        for gl in range(2):
            c0 = gl * SSM_FLAT
            r_ref[0, d * lanes:(d + 1) * lanes, c0:c0 + SSM_FLAT] = jnp.where(w_group == gl, w_re, 0.0).astype(BF16)
            r_ref[0, (2 + d) * lanes:(3 + d) * lanes, c0:c0 + SSM_FLAT] = jnp.where(w_group == gl, -w_im, 0.0).astype(BF16)

    for gl in range(2):
        m_ref[gl] = m_acc[gl].astype(BF16)


def _s5_operators(lam_re, lam_im, log_dt, b_re, b_im, c_re, c_im, d_skip, n_steps):
    pairs = N_GROUPS // 2
    quantities = lambda re, im: jnp.stack([re[0], im[0], re[1], im[1]], 1)
    lam_row = quantities(lam_re.reshape(2, pairs, 2 * STATE), lam_im.reshape(2, pairs, 2 * STATE))
    lam_col = lam_row.transpose(0, 2, 1)
    bt_lanes = lambda x: (x.reshape(2, pairs, 2, STATE, SSM_GROUP).transpose(0, 1, 4, 2, 3)
                          .reshape(2, pairs, SSM_GROUP, 2 * STATE))
    bt = quantities(bt_lanes(b_re), bt_lanes(b_im))
    c_lanes = lambda x: (x.reshape(2, pairs, 2, SSM_GROUP, STATE).transpose(0, 1, 3, 2, 4)
                         .reshape(2, pairs, SSM_GROUP, 2 * STATE))
    ct = quantities(c_lanes(c_re), c_lanes(c_im))
    lead = lambda a, n: pl.BlockSpec((n,) + a.shape[1:], lambda g: (g,) + (0,) * (len(a.shape) - 1))
    out_m = jax.ShapeDtypeStruct((N_GROUPS, SSM_FLAT, SSM_FLAT), BF16)
    out_sq = jax.ShapeDtypeStruct((pairs, 2 * SSM_FLAT, 2 * SSM_FLAT), BF16)
    out_ap = jax.ShapeDtypeStruct((pairs, 4, n_steps, 2 * STATE), F32)
    m_op, s_pair, r_pair, a_pair = pl.pallas_call(
        functools.partial(_s5_prep_kernel, n_steps=n_steps),
        out_shape=(out_m, out_sq, out_sq, out_ap),
        grid=(pairs,),
        in_specs=[pl.BlockSpec(memory_space=pltpu.SMEM), lead(lam_row, 1), lead(lam_col, 1), lead(bt, 1), lead(ct, 1)],
        out_specs=(lead(out_m, 2), lead(out_sq, 1), lead(out_sq, 1), lead(out_ap, 1)),
        compiler_params=_params(1),
        name="s5_prep",
    )(log_dt, lam_row, lam_col, bt, ct)
    d_row = jnp.tile(d_skip.reshape(N_GROUPS, 1, SSM_GROUP), (1, 1, SSM_CHUNK))
    return m_op, s_pair, r_pair, a_pair, d_row


def _ssm_core_kernel(*refs, seq_chunks, n_steps, pairs, has_h0, emit_state):
    refs = list(refs)
    u_ref, m_ref, s_ref, r_ref, ap_ref, d_ref = refs[:6]
    refs = refs[6:]
    h0_ref = refs.pop(0) if has_h0 else None
    y_ref = refs.pop(0)
    if emit_state:
        e_ref, fin_scr = refs
    rows = u_ref.shape[2]
    n_seq = rows // seq_chunks
    lanes = 2 * STATE
    pos = lax.broadcasted_iota(jnp.int32, (rows, lanes), 0) % seq_chunks

    def upstream(x, d, fwd):
        return pltpu.roll(x, d if fwd else rows - d, 0)

    def has_upstream(d, fwd):
        return (pos >= d) if fwd else (pos < seq_chunks - d)

    def scan(hr, hi, pi, fwd):
        q = 0 if fwd else 2
        for i in range(n_steps):
            d = 1 << i
            p_re, p_im = ap_ref[pi, q, i:i + 1, :], ap_ref[pi, q + 1, i:i + 1, :]
            if d < 8:
                ok = has_upstream(d, fwd)
                sr = jnp.where(ok, upstream(hr, d, fwd), 0.0)
                si = jnp.where(ok, upstream(hi, d, fwd), 0.0)
                hr, hi = hr + p_re * sr - p_im * si, hi + p_re * si + p_im * sr
            else:
                r3, i3 = hr.reshape(n_seq, seq_chunks, lanes), hi.reshape(n_seq, seq_chunks, lanes)
                keep = seq_chunks - d
                if fwd:
                    sr, si, tr, ti = r3[:, :keep], i3[:, :keep], r3[:, d:], i3[:, d:]
                else:
                    sr, si, tr, ti = r3[:, d:], i3[:, d:], r3[:, :keep], i3[:, :keep]
                nr = tr + p_re * sr - p_im * si
                ni = ti + p_re * si + p_im * sr
                if fwd:
                    r3 = jnp.concatenate([r3[:, :d], nr], axis=1)
                    i3 = jnp.concatenate([i3[:, :d], ni], axis=1)
                else:
                    r3 = jnp.concatenate([nr, r3[:, keep:]], axis=1)
                    i3 = jnp.concatenate([ni, i3[:, keep:]], axis=1)
                hr, hi = r3.reshape(rows, lanes), i3.reshape(rows, lanes)
        return hr, hi

    for pi in range(pairs):
        u0, u1 = u_ref[0, 2 * pi], u_ref[0, 2 * pi + 1]
        ub = jnp.concatenate([u0, u1], axis=1).astype(BF16)
        y0 = _dot(ub[:, :SSM_FLAT], m_ref[2 * pi]) + d_ref[2 * pi] * u0
        y1 = _dot(ub[:, SSM_FLAT:], m_ref[2 * pi + 1]) + d_ref[2 * pi + 1] * u1
        h = _dot(ub, s_ref[pi])
        state = {}
        for fwd, off in ((True, 0), (False, lanes)):
            hr, hi = h[:, off:off + lanes], h[:, 2 * lanes + off:3 * lanes + off]
            q = 0 if fwd else 2
            if has_h0:
                h0r = h0_ref[0, pi, 0:1, off:off + lanes]
                h0i = h0_ref[0, pi, 0:1, 2 * lanes + off:3 * lanes + off]
                a_re, a_im = ap_ref[pi, q, 0:1, :], ap_ref[pi, q + 1, 0:1, :]
                first = jnp.logical_not(has_upstream(1, fwd))
                hr = hr + jnp.where(first, a_re * h0r - a_im * h0i, 0.0)
                hi = hi + jnp.where(first, a_re * h0i + a_im * h0r, 0.0)
            else:
                h0r = h0i = 0.0
            hr, hi = scan(hr, hi, pi, fwd)
            ok = has_upstream(1, fwd)
            state[fwd] = (hr, hi, jnp.where(ok, upstream(hr, 1, fwd), h0r), jnp.where(ok, upstream(hi, 1, fwd), h0i))
        if emit_state:
            for q, (fwd, part) in enumerate(((True, 0), (False, 0), (True, 1), (False, 1))):
                fin_scr[q] = state[fwd][part]
                last = seq_chunks - 1 if fwd else 0
                e_ref[0, pi, :, q * lanes:(q + 1) * lanes] = fin_scr[q, pl.ds(last, n_seq, stride=seq_chunks), :]
        h_in = jnp.concatenate([state[True][2], state[False][2], state[True][3], state[False][3]], axis=1)
        y_state = _dot(h_in.astype(BF16), r_ref[pi])
        y_ref[0, 2 * pi] = y0 + y_state[:, :SSM_FLAT]
        y_ref[0, 2 * pi + 1] = y1 + y_state[:, SSM_FLAT:]


def _ssm_core(u_flat, pair_ops, h0, seq_chunks, emit_state):
    nb, _, rows, _ = u_flat.shape
    m_op, s_pair, r_pair, a_pair, d_row = pair_ops
    n_steps = a_pair.shape[2]
    assert (1 << n_steps) == seq_chunks
    pt = 8
    blk = pl.BlockSpec((1, 2 * pt, rows, SSM_FLAT), lambda b, g: (b, g, 0, 0))
    lead = lambda a, n: pl.BlockSpec((n,) + a.shape[1:], lambda b, g: (g,) + (0,) * (a.ndim - 1))
    in_specs = [blk, lead(m_op, 2 * pt), lead(s_pair, pt), lead(r_pair, pt), lead(a_pair, pt), lead(d_row, 2 * pt)]
    args = [u_flat, m_op, s_pair, r_pair, a_pair, d_row]
    if h0 is not None:
        in_specs.append(pl.BlockSpec((1, pt, 1, 8 * STATE), lambda b, g: (b, g, 0, 0)))
        args.append(h0)
    out_shape = [jax.ShapeDtypeStruct(u_flat.shape, F32)]
    out_specs = [blk]
    scratch = []
    if emit_state:
        n_seq = rows // seq_chunks
        out_shape.append(jax.ShapeDtypeStruct((nb, N_GROUPS // 2, n_seq, 8 * STATE), F32))
        out_specs.append(pl.BlockSpec((1, pt, n_seq, 8 * STATE), lambda b, g: (b, g, 0, 0)))
        scratch.append(pltpu.VMEM((4, rows, 2 * STATE), F32))
    kern = functools.partial(_ssm_core_kernel, seq_chunks=seq_chunks, n_steps=n_steps, pairs=pt,
                             has_h0=h0 is not None, emit_state=emit_state)
    return pl.pallas_call(
        kern, out_shape=tuple(out_shape), grid=(nb, N_GROUPS // (2 * pt)),
        in_specs=in_specs, out_specs=tuple(out_specs), scratch_shapes=scratch,
        compiler_params=_params(2), name="ssm_core",
    )(*args)


def _ssm_out_kernel(y_ref, x_ref, mod_ref, ln_ref, wg_ref, wo_ref, out_ref, y_scr):
    i = pl.program_id(0)
    n_chunks = x_ref.shape[0] // SSM_CHUNK

    @pl.when(i == 0)
    def _():
        y_scr[1] = jnp.zeros(y_scr.shape[1:], F32)

    def step(fill_slot):
        fill, ready = y_scr.at[fill_slot], y_scr.at[1 - fill_slot]
        for c in range(LANE_TILES):
            for half in range(SSM_CHUNK // LANE_BLOCKS):
                per_group = [y_ref[0, c * LANE_BLOCKS + gl, :, half * 128:(half + 1) * 128]
                             for gl in range(LANE_BLOCKS)]
                per_token = _block_transpose(per_group)
                for s in range(LANE_BLOCKS):
                    fill[c, pl.ds(half * LANE_BLOCKS + s, n_chunks, stride=SSM_CHUNK), :] = per_token[s]
        y = jnp.concatenate([ready[c] for c in range(LANE_TILES)], axis=1)
        vg = _dot(_gelu_tanh(y).astype(BF16), wg_ref[...])
        glu = (vg[:, :D_MODEL] * jax.nn.sigmoid(vg[:, D_MODEL:])).astype(BF16)
        z = DN_ALPHA * x_ref[...] + mod_ref[0, 2:3, :] * _dot(glu, wo_ref[...])
        out_ref[...] = _layer_norm(z, ln_ref[0:1, :], ln_ref[1:2, :])

    for parity in range(2):
        pl.when(i % 2 == parity)(functools.partial(step, parity))


def _ssm_out(y_flat, x2, mods, mod_base, rows_per_mod, ln, w_glu, w_out):
    r = x2.shape[0]
    nb = y_flat.shape[0]
    t = r // nb
    tm = _row_tile(math.gcd(rows_per_mod, t), SSM_ROWS)
    per_mod = rows_per_mod // tm
    per_seq = t // tm
    nj = tm // SSM_CHUNK
    n_tiles = r // tm
    fill_tile = lambda i: jnp.minimum(i, n_tiles - 1)
    done_tile = lambda i: jnp.maximum(i - 1, 0)
    row = pl.BlockSpec((tm, D_MODEL), lambda i: (done_tile(i), 0))
    return pl.pallas_call(
        _ssm_out_kernel,
        out_shape=jax.ShapeDtypeStruct((r, D_MODEL), F32),
        grid=(n_tiles + 1,),
        in_specs=[pl.BlockSpec((1, N_GROUPS, nj, SSM_FLAT),
                               lambda i: (fill_tile(i) // per_seq, 0, fill_tile(i) % per_seq, 0)),
                  row,
                  _mods_spec(mods, lambda i: mod_base + done_tile(i) // per_mod),
                  _ln_spec(ln), _layer_spec(w_glu), _layer_spec(w_out)],
        out_specs=row,
        scratch_shapes=[pltpu.VMEM((2, LANE_TILES, tm, 128), F32)],
        compiler_params=_params(1),
        name="ssm_out",
    )(y_flat, x2, mods[0], ln[0], w_glu[0], w_out[0])


def _ssm_mixer(x2, nb, seq_chunks, mods, mod_base, rows_per_mod, ln, w_in, ops, w_glu, w_out, h0, emit_state,
               u_flat=None):
    if u_flat is None:
        u_flat = _ssm_in(x2, mods, mod_base, rows_per_mod, w_in, nb)
    outs = _ssm_core(u_flat, ops, h0, seq_chunks, emit_state)
    x_new = _ssm_out(outs[0], x2, mods, mod_base, rows_per_mod, ln, w_glu, w_out)
    return x_new, (outs[1] if emit_state else None)


def _rope_tables(t):
    n_rows = t // GRID_W
    row = jnp.repeat(jnp.arange(n_rows, dtype=F32), GRID_W)
    col = (jnp.arange(t) % GRID_W).astype(F32)
    nfreq = HEAD_DIM // 4
    inv = jnp.power(ROPE_BASE, -jnp.arange(nfreq, dtype=F32) / nfreq)
    ang_r, ang_c = row[:, None] * inv, col[:, None] * inv
    cos = jnp.concatenate([jnp.cos(ang_r)] * 2 + [jnp.cos(ang_c)] * 2, -1)
    sin = jnp.concatenate([-jnp.sin(ang_r), jnp.sin(ang_r), -jnp.sin(ang_c), jnp.sin(ang_c)], -1)
    return jnp.tile(cos, (1, 2)), jnp.tile(sin, (1, 2))


def kernel(x_prompt, x_sample, cache_k, cache_v, state_ssm_re, state_ssm_im, c, c_ctx, w_ada, b_ada, ln_g, ln_b, w_qkv, w_o, attn_sink, ssm_w_in, ssm_lam_re, ssm_lam_im, ssm_log_dt, ssm_b_re, ssm_b_im, ssm_c_re, ssm_c_im, ssm_d, ssm_w_glu, ssm_w_out, ffn_w1, ffn_w3, ffn_w2):
    nc, tc, _ = x_prompt.shape
    nl, tl, _ = x_sample.shape
    n_attn = w_qkv.shape[0]
    n_ssm = ssm_w_in.shape[0]

    n_mod_rows = -(-(1 + nl) // 8) * 8
    cvec = jnp.zeros((n_mod_rows, D_MODEL), F32).at[0].set(c_ctx).at[1:1 + nl].set(c)
    mods_all = _ada_mods(cvec, w_ada, b_ada)
    ln_all = jnp.stack([ln_g, ln_b], axis=2)

    bf = lambda w: w.astype(BF16)
    w_v = w_qkv[:, :, Q_WIDTH + KV_WIDTH:].reshape(n_attn, D_MODEL, N_KV_HEADS, HEAD_DIM)
    w_v_ext = jnp.pad(w_v, ((0, 0), (0, 0), (0, 0), (0, 128 - HEAD_DIM))).reshape(n_attn, D_MODEL, V_EXT)
    w_qkv_ctx = bf(jnp.concatenate([w_qkv, w_v_ext], axis=-1))
    w_qkv_lat = bf(jnp.concatenate([w_qkv[:, :, :Q_WIDTH + KV_WIDTH], w_v_ext], axis=-1))
    w_o_b = bf(w_o)
    w_in_b, w_glu_b, w_out_b = bf(ssm_w_in), bf(ssm_w_glu), bf(ssm_w_out)
    w1_b, w3_b, w2_b = bf(ffn_w1), bf(ffn_w3), bf(ffn_w2)
    cos, sin = _rope_tables(tl)
    kx = bf(cache_k).reshape(nl, n_attn, -1, KV_WIDTH)
    vx = jnp.concatenate([bf(cache_v), jnp.ones(cache_v.shape[:-1] + (128 - HEAD_DIM,), BF16)], -1)
    vx = vx.reshape(nl, n_attn, -1, V_EXT)

    steps_c = (tc // SSM_CHUNK).bit_length() - 1
    steps_l = (tl // SSM_CHUNK).bit_length() - 1
    ssm_ops = []
    for s in range(n_ssm):
        prm = (ssm_lam_re[s], ssm_lam_im[s], ssm_log_dt[s], ssm_b_re[s], ssm_b_im[s],
               ssm_c_re[s], ssm_c_im[s], ssm_d[s])
        ssm_ops.append(_s5_operators(*prm, max(steps_c, steps_l)))

    def ops_for(s, n_steps):
        m_op, s_pair, r_pair, a_pair, d_row = ssm_ops[s]
        return m_op, s_pair, r_pair, a_pair[:, :, :n_steps], d_row

    xc = x_prompt.reshape(nc * tc, D_MODEL)
    xl = x_sample.reshape(nl * tl, D_MODEL)
    new_k, new_v, new_sr, new_si = [], [], [], []
    uc = ul = None
    for l in range(DEPTH):
        mods = (mods_all, l)
        ln1, ln2 = (ln_all, l, 0), (ln_all, l, 1)
        if l % 2 == 0:
            a = l // 2
            q, kb, vb, k32, v32 = _qkv_ctx(xc, mods, (w_qkv_ctx, a))
            new_k.append(k32.reshape(nc, tc, N_KV_HEADS, HEAD_DIM))
            new_v.append(v32.reshape(nc, tc, N_KV_HEADS, HEAD_DIM))
            xc = _attn_ctx(attn_sink[a], q, kb, vb, xc, mods, ln1, (w_o_b, a), tc)
            qp, qr, kr, v = _qkv_lat(xl, mods, (w_qkv_lat, a), cos, sin, tl)
            xl = _attn_lat(attn_sink[a], qp, qr, kr, v, (kx, a), (vx, a), xl, mods, ln1, (w_o_b, a), nl, tl)
        else:
            s = l // 2
            xc, e_c = _ssm_mixer(xc, 1, tc // SSM_CHUNK, mods, 0, nc * tc, ln1, (w_in_b, s),
                                 ops_for(s, steps_c), (w_glu_b, s), (w_out_b, s), None, True, u_flat=uc)
            fin = e_c.reshape(N_GROUPS // 2, nc, 4, 2, STATE).transpose(2, 1, 0, 3, 4).reshape(4, nc, N_GROUPS, STATE)
            new_sr.append(jnp.stack([fin[0], fin[1]], 1))
            new_si.append(jnp.stack([fin[2], fin[3]], 1))
            h0 = jnp.stack([state_ssm_re[:, s, 0], state_ssm_re[:, s, 1],
                            state_ssm_im[:, s, 0], state_ssm_im[:, s, 1]], 1)
            h0 = h0.reshape(nl, 4, N_GROUPS // 2, 2 * STATE).transpose(0, 2, 1, 3)
            h0 = h0.reshape(nl, N_GROUPS // 2, 8 * STATE)
            xl, _ = _ssm_mixer(xl, nl, tl // SSM_CHUNK, mods, 1, tl, ln1, (w_in_b, s),
                               ops_for(s, steps_l), (w_glu_b, s), (w_out_b, s), h0[:, :, None, :], False,
                               u_flat=ul)
        nxt = ((mods_all, l + 1), (w_in_b, (l + 1) // 2)) if (l + 1 < DEPTH and (l + 1) % 2 == 1) else None
        xc, uc = _ffn(xc, mods, 0, nc * tc, ln2, (w1_b, l), (w3_b, l), (w2_b, l), nxt, 1)
        xl, ul = _ffn(xl, mods, 1, tl, ln2, (w1_b, l), (w3_b, l), (w2_b, l), nxt, nl)

    return (xc.reshape(nc, tc, D_MODEL), xl.reshape(nl, tl, D_MODEL),
            jnp.stack(new_k, axis=1), jnp.stack(new_v, axis=1),
            jnp.stack(new_sr, axis=1), jnp.stack(new_si, axis=1))
```
